```python
import math
import jax, jax.numpy as jnp
from jax import lax
import numpy as np

D_MODEL = 2048
BATCH = 8
SEQ = 4096
DEPTH = 2
DEC_BATCH = 32
DEC_SEQ = 32
PAST_LEN = 1024

CHUNK = 64
H_A = 8
DK_A = 64
DV_A = 2 * DK_A
H_B = 8
D_B = 128
BAND_PREV_CHUNKS = 8
BAND_REACH = BAND_PREV_CHUNKS * CHUNK
BAND_LEN = BAND_REACH + CHUNK
REL_CLIP = 128
T5_BUCKETS = 32
T5_MAX_DIST = 128
D_FF = -(-8 * D_MODEL // (3 * 256)) * 256
ALPHA = (2 * DEPTH) ** 0.25
BETA = (8 * DEPTH) ** -0.25
QBLK = 128
NEG = -1e30
LN_EPS = 1e-5
QA_W = H_A * 2 * DK_A
VA_W = H_A * DV_A
B_W = H_B * D_B
IN_WIDTHS = (QA_W, QA_W, VA_W, B_W, B_W, B_W, D_MODEL, D_MODEL)
IN_W = sum(IN_WIDTHS)
IN_SPLITS = tuple(int(v) for v in np.cumsum(IN_WIDTHS)[:-1])

kernel_name = "streaming_hybrid_diffattn_chunkband"


def layer_norm(x, g=None, b=None):
    xf = x.astype(jnp.float32)
    xc = xf - jnp.mean(xf, -1, keepdims=True)
    y = xc * lax.rsqrt(jnp.mean(xc * xc, -1, keepdims=True) + LN_EPS)
    if g is not None:
        y = y * g.astype(jnp.float32) + b.astype(jnp.float32)
    return y.astype(x.dtype)


def head_rms_norm(x, g):
    xf = x.astype(jnp.float32)
    y = xf * lax.rsqrt(jnp.mean(xf * xf, -1, keepdims=True) + LN_EPS) * g.astype(jnp.float32)
    return y.astype(x.dtype)


def t5_bucket(rel):
    nb = T5_BUCKETS // 2
    max_exact = nb // 2
    bucket = (rel > 0).astype(jnp.int32) * nb
    n = jnp.abs(rel)
    nf = jnp.maximum(n, 1).astype(jnp.float32)
    large = max_exact + (jnp.log(nf / max_exact) / math.log(T5_MAX_DIST / max_exact)
                         * (nb - max_exact)).astype(jnp.int32)
    large = jnp.minimum(large, nb - 1)
    return bucket + jnp.where(n < max_exact, n, large)


def diff_attention(q, k, v, q_pos, k_pos, t5_bias, lam, subln_g, lam_init):
    s = jnp.einsum("bqhtd,bkhtd->bthqk", q, k).astype(jnp.float32) * (DK_A ** -0.5)
    rel = k_pos[None, :] - q_pos[:, None]
    bias = jnp.moveaxis(t5_bias[t5_bucket(rel)], -1, 0).astype(jnp.float32)
    visible = (k_pos[None, :] // CHUNK) <= (q_pos[:, None] // CHUNK)
    p = jax.nn.softmax(jnp.where(visible, s + bias, NEG), axis=-1)
    a = (p[:, 0] - lam * p[:, 1]).astype(v.dtype)
    o = jnp.einsum("bhqk,bkhd->bqhd", a, v)
    o = head_rms_norm(o, subln_g) * (1.0 - lam_init)
    return o.reshape(o.shape[0], o.shape[1], H_A * DV_A)


def band_attention(q, k, v, q_pos, k_pos, rel_bias):
    s = jnp.einsum("bqhd,bkhd->bhqk", q, k).astype(jnp.float32) * (D_B ** -0.5)
    rel = jnp.clip(k_pos[None, :] - q_pos[:, None], -REL_CLIP, REL_CLIP) + REL_CLIP
    bias = jnp.moveaxis(rel_bias[rel], -1, 0).astype(jnp.float32)
    dchunk = q_pos[:, None] // CHUNK - k_pos[None, :] // CHUNK
    visible = (dchunk >= 0) & (dchunk <= BAND_PREV_CHUNKS) & (k_pos[None, :] >= 0)
    p = jax.nn.softmax(jnp.where(visible, s + bias, NEG), axis=-1).astype(v.dtype)
    o = jnp.einsum("bhqk,bkhd->bqhd", p, v)
    return o.reshape(o.shape[0], o.shape[1], H_B * D_B)


def diff_attention_prompt(q, k, v, t5_bias, lam, subln_g, lam_init):
    b, s = q.shape[:2]
    nblk = s // QBLK
    pos = jnp.arange(s)
    qb = jnp.moveaxis(q.reshape(b, nblk, QBLK, H_A, 2, DK_A), 1, 0)

    def one_block(args):
        q_blk, i = args
        return diff_attention(q_blk, k, v, i * QBLK + jnp.arange(QBLK), pos,
                              t5_bias, lam, subln_g, lam_init)

    o = lax.map(one_block, (qb, jnp.arange(nblk)))
    return jnp.moveaxis(o, 0, 1).reshape(b, s, H_A * DV_A)


def band_attention_prompt(q, k, v, rel_bias):
    b, s = q.shape[:2]
    nc = s // CHUNK
    pad = ((0, 0), (BAND_REACH, 0), (0, 0), (0, 0))
    k_pad = jnp.pad(k, pad)
    v_pad = jnp.pad(v, pad)
    band_off = jnp.arange(BAND_LEN) - BAND_REACH

    def one_chunk(n):
        start = n * CHUNK
        q_n = lax.dynamic_slice_in_dim(q, start, CHUNK, axis=1)
        k_n = lax.dynamic_slice_in_dim(k_pad, start, BAND_LEN, axis=1)
        v_n = lax.dynamic_slice_in_dim(v_pad, start, BAND_LEN, axis=1)
        return band_attention(q_n, k_n, v_n, start + jnp.arange(CHUNK), start + band_off, rel_bias)

    o = lax.map(one_chunk, jnp.arange(nc))
    return jnp.moveaxis(o, 0, 1).reshape(b, s, H_B * D_B)


def block_forward(x, c, attend, w_mod, b_mod, w_in, w_oa, w_ob, w_out, ln1_g, ln1_b,
                  w1, w3, w2, ln2_g, ln2_b):
    b, t, _ = x.shape
    mod = jax.nn.silu(c) @ w_mod + b_mod
    sh_m, sc_m, g_m, sh_f, sc_f, g_f = jnp.split(mod[:, None, :], 6, axis=-1)
    h = layer_norm(x) * (1 + sc_m) + sh_m
    qa, ka, va, qb, kb, vb, ga, gb = jnp.split(h @ w_in, IN_SPLITS, axis=-1)
    qa = qa.reshape(b, t, H_A, 2, DK_A)
    ka = ka.reshape(b, t, H_A, 2, DK_A)
    va = va.reshape(b, t, H_A, DV_A)
    qb = qb.reshape(b, t, H_B, D_B)
    kb = kb.reshape(b, t, H_B, D_B)
    vb = vb.reshape(b, t, H_B, D_B)
    oa, ob = attend(qa, ka, va, qb, kb, vb)
    merged = jax.nn.sigmoid(ga) * (oa @ w_oa) + jax.nn.sigmoid(gb) * (ob @ w_ob)
    x = layer_norm(ALPHA * x + g_m * (merged @ w_out), ln1_g, ln1_b)
    h = layer_norm(x) * (1 + sc_f) + sh_f
    f = (jax.nn.silu(h @ w1) * (h @ w3)) @ w2
    x = layer_norm(ALPHA * x + g_f * f, ln2_g, ln2_b)
    return x, ka, va, kb, vb


def setup_inputs(seed: int = 0) -> dict:
    key = jax.random.key(seed)
    ks = jax.random.split(key, 32)

    def nrm(k, shape, s):
        return jax.random.normal(k, shape, jnp.float32) * s

    band_past = min(BAND_REACH, PAST_LEN)
    D = D_MODEL
    return {
        "x_prompt": nrm(ks[0], (BATCH, SEQ, D), 1.0),
        "x_sample": nrm(ks[1], (DEC_BATCH, DEC_SEQ, D), 1.0),
        "cache_a_k": nrm(ks[2], (DEPTH, DEC_BATCH, PAST_LEN, H_A, 2, DK_A), 1.0),
        "cache_a_v": nrm(ks[3], (DEPTH, DEC_BATCH, PAST_LEN, H_A, DV_A), 1.0),
        "cache_b_k": nrm(ks[4], (DEPTH, DEC_BATCH, band_past, H_B, D_B), 1.0),
        "cache_b_v": nrm(ks[5], (DEPTH, DEC_BATCH, band_past, H_B, D_B), 1.0),
        "c_prompt": nrm(ks[6], (BATCH, D), 1.0),
        "c_sample": nrm(ks[7], (DEC_BATCH, D), 1.0),
        "w_mod": nrm(ks[8], (DEPTH, D, 6 * D), 0.5 * D ** -0.5),
        "b_mod": nrm(ks[9], (DEPTH, 6 * D), 0.02),
        "w_in": nrm(ks[10], (DEPTH, D, IN_W), D ** -0.5),
        "lambda_q1": nrm(ks[11], (DEPTH, DK_A), 0.1),
        "lambda_k1": nrm(ks[12], (DEPTH, DK_A), 0.1),
        "lambda_q2": nrm(ks[13], (DEPTH, DK_A), 0.1),
        "lambda_k2": nrm(ks[14], (DEPTH, DK_A), 0.1),
        "subln_g": 1.0 + nrm(ks[15], (DEPTH, DV_A), 0.02),
        "t5_bias": nrm(ks[16], (T5_BUCKETS, H_A), 0.5),
        "rel_bias": nrm(ks[17], (DEPTH, 2 * REL_CLIP + 1, H_B), 0.5),
        "w_oa": nrm(ks[18], (DEPTH, VA_W, D), BETA * VA_W ** -0.5),
        "w_ob": nrm(ks[19], (DEPTH, B_W, D), BETA * B_W ** -0.5),
        "w_out": nrm(ks[20], (DEPTH, D, D), BETA * D ** -0.5),
        "ln1_g": 1.0 + nrm(ks[21], (DEPTH, D), 0.02),
        "ln1_b": nrm(ks[22], (DEPTH, D), 0.02),
        "w1": nrm(ks[23], (DEPTH, D, D_FF), D ** -0.5),
        "w3": nrm(ks[24], (DEPTH, D, D_FF), D ** -0.5),
        "w2": nrm(ks[25], (DEPTH, D_FF, D), BETA * D_FF ** -0.5),
        "ln2_g": 1.0 + nrm(ks[26], (DEPTH, D), 0.02),
        "ln2_b": nrm(ks[27], (DEPTH, D), 0.02),
    }


def reference(x_prompt, x_sample, cache_a_k, cache_a_v, cache_b_k, cache_b_v, c_prompt, c_sample,
              w_mod, b_mod, w_in, lambda_q1, lambda_k1, lambda_q2, lambda_k2, subln_g, t5_bias,
              rel_bias, w_oa, w_ob, w_out, ln1_g, ln1_b, w1, w3, w2, ln2_g, ln2_b):
    past = cache_a_k.shape[2]
    band_past = cache_b_k.shape[2]
    t_new = x_sample.shape[1]
    prompt_band = min(BAND_REACH, x_prompt.shape[1])
    q_pos_s = past + jnp.arange(t_new)
    ka_pos_s = jnp.arange(past + t_new)
    kb_pos_s = past - band_past + jnp.arange(band_past + t_new)

    xp, xs = x_prompt, x_sample
    akp, avp, bkp, bvp, aks, avs, bks, bvs = [], [], [], [], [], [], [], []
    for l in range(DEPTH):
        lam_init = 0.8 - 0.6 * math.exp(-0.3 * l)
        f32 = jnp.float32
        lam = (jnp.exp(jnp.sum(lambda_q1[l].astype(f32) * lambda_k1[l].astype(f32)))
               - jnp.exp(jnp.sum(lambda_q2[l].astype(f32) * lambda_k2[l].astype(f32))) + lam_init)
        layer_w = (w_mod[l], b_mod[l], w_in[l], w_oa[l], w_ob[l], w_out[l], ln1_g[l], ln1_b[l],
                   w1[l], w3[l], w2[l], ln2_g[l], ln2_b[l])

        def attend_prompt(qa, ka, va, qb, kb, vb):
            oa = diff_attention_prompt(qa, ka, va, t5_bias, lam, subln_g[l], lam_init)
            ob = band_attention_prompt(qb, kb, vb, rel_bias[l])
            return oa, ob

        def attend_sample(qa, ka, va, qb, kb, vb):
            ka_all = jnp.concatenate([cache_a_k[l].astype(ka.dtype), ka], axis=1)
            va_all = jnp.concatenate([cache_a_v[l].astype(va.dtype), va], axis=1)
            oa = diff_attention(qa, ka_all, va_all, q_pos_s, ka_pos_s, t5_bias, lam, subln_g[l], lam_init)
            kb_all = jnp.concatenate([cache_b_k[l].astype(kb.dtype), kb], axis=1)
            vb_all = jnp.concatenate([cache_b_v[l].astype(vb.dtype), vb], axis=1)
            ob = band_attention(qb, kb_all, vb_all, q_pos_s, kb_pos_s, rel_bias[l])
            return oa, ob

        xp, ka, va, kb, vb = block_forward(xp, c_prompt, attend_prompt, *layer_w)
        akp.append(ka)
        avp.append(va)
        bkp.append(kb[:, kb.shape[1] - prompt_band:])
        bvp.append(vb[:, vb.shape[1] - prompt_band:])
        xs, ka, va, kb, vb = block_forward(xs, c_sample, attend_sample, *layer_w)
        aks.append(ka)
        avs.append(va)
        bks.append(kb)
        bvs.append(vb)

    return (xp, xs, jnp.stack(akp), jnp.stack(avp), jnp.stack(bkp), jnp.stack(bvp),
            jnp.stack(aks), jnp.stack(avs), jnp.stack(bks), jnp.stack(bvs))
```

```python
import functools
import math

import jax
import jax.numpy as jnp
import numpy as np
from jax import lax
from jax.experimental import pallas as pl
from jax.experimental.pallas import tpu as pltpu

F32 = jnp.float32
BF16 = jnp.bfloat16

CHUNK = 64
N_HEADS = 8
HEAD_W = 128
DK_A = 64
BAND_PREV_CHUNKS = 8
BAND_REACH = BAND_PREV_CHUNKS * CHUNK
REL_CLIP = 128
T5_BUCKETS = 32
T5_MAX_DIST = 128
NEG = -1e30
LN_EPS = 1e-5
MIX_W = N_HEADS * HEAD_W

V7X_VMEM_BYTES = 64 * 1024 * 1024
VMEM_LIMIT = 56 * 1024 * 1024

TQ_A = 256
TQ_B = 128
WIN_B = BAND_REACH + TQ_B


def _cparams(sem):
    return pltpu.CompilerParams(dimension_semantics=sem, vmem_limit_bytes=VMEM_LIMIT)


def _sigmoid(x):
    return 1.0 / (1.0 + jnp.exp(-x))


def _ln(x):
    mu = jnp.mean(x, axis=-1, keepdims=True)
    xc = x - mu
    var = jnp.mean(xc * xc, axis=-1, keepdims=True)
    return xc * lax.rsqrt(var + LN_EPS)


def _modulate(y, scale, shift, gpt):
    if gpt == 1:
        return y * (1.0 + scale[0]) + shift[0]
    bm, d = y.shape
    y3 = y.reshape(gpt, bm // gpt, d)
    return (y3 * (1.0 + scale) + shift).reshape(bm, d)


def _gate(y, gate, gpt):
    if gpt == 1:
        return y * gate[0]
    bm, d = y.shape
    return (y.reshape(gpt, bm // gpt, d) * gate).reshape(bm, d)


def _mod_kernel(c_ref, w_ref, b_ref, o_ref):
    c = c_ref[...]
    s = (c * _sigmoid(c)).astype(BF16)
    o_ref[...] = jnp.dot(s, w_ref[...].astype(BF16), preferred_element_type=F32) + b_ref[...]


def _mod_call(c_all, w_mod, b_mod):
    depth, d, n = w_mod.shape
    g = c_all.shape[0]
    tn = math.gcd(n, 1024)
    return pl.pallas_call(
        _mod_kernel,
        out_shape=jax.ShapeDtypeStruct((depth, g, n), F32),
        grid=(depth, n // tn),
        in_specs=[
            pl.BlockSpec((g, d), lambda l, j: (0, 0)),
            pl.BlockSpec((None, d, tn), lambda l, j: (l, 0, j)),
            pl.BlockSpec((None, 1, tn), lambda l, j: (l, 0, j)),
        ],
        out_specs=pl.BlockSpec((None, g, tn), lambda l, j: (l, 0, j)),
        compiler_params=_cparams(("arbitrary", "arbitrary")),
        name="mod",
    )(c_all, w_mod, b_mod.reshape(depth, 1, n))


def _inproj_segments(d, tn):
    widths = (MIX_W,) * 6 + (d, d)
    segs, lo = [], 0
    for w in widths:
        assert w % tn == 0
        segs.append((lo, w // tn))
        lo += w // tn
    return tuple(segs)


def _inproj_kernel(x_ref, sc_ref, sh_ref, w_ref,
                   qa_ref, ka_ref, va_ref, qb_ref, kb_ref, vb_ref, sga_ref, sgb_ref,
                   h_scr, *, gpt, segs):
    j = pl.program_id(1)

    @pl.when(j == 0)
    def _():
        h = _modulate(_ln(x_ref[...]), sc_ref[...], sh_ref[...], gpt)
        h_scr[...] = h.astype(BF16)

    acc = jnp.dot(h_scr[...], w_ref[...], preferred_element_type=F32)

    def plain(v):
        return v

    def to_bf16(v):
        return v.astype(BF16)

    def scaled_bf16(v):
        return (v * (DK_A ** -0.5)).astype(BF16)

    def gate_bf16(v):
        return _sigmoid(v).astype(BF16)

    outs = ((qa_ref, scaled_bf16), (ka_ref, plain), (va_ref, plain), (qb_ref, to_bf16),
            (kb_ref, plain), (vb_ref, plain), (sga_ref, gate_bf16), (sgb_ref, gate_bf16))
    for (lo, cnt), (ref, fn) in zip(segs, outs):
        @pl.when((j >= lo) & (j < lo + cnt))
        def _(ref=ref, fn=fn):
            ref[...] = fn(acc)


def _mod_spec(gpt, rows_per_group, bm, group0, which, d):
    if gpt == 1:
        tiles_per_group = rows_per_group // bm
        return pl.BlockSpec((None, 1, 1, d), lambda i, *_: (0, group0 + i // tiles_per_group, 0, which))
    return pl.BlockSpec((None, gpt, 1, d), lambda i, *_: (0, group0 // gpt + i, 0, which))


def _inproj_call(x, mod_l, w_in, *, rows_per_group, group0, bm, tn):
    m, d = x.shape
    n = w_in.shape[1]
    gpt = max(1, bm // rows_per_group)
    segs = _inproj_segments(d, tn)
    assert n == tn * (segs[-1][0] + segs[-1][1])

    def out_spec(seg):
        lo, cnt = seg
        return pl.BlockSpec((bm, tn), lambda i, j: (i, jnp.clip(j - lo, 0, cnt - 1)))

    out_shape = (
        jax.ShapeDtypeStruct((m, MIX_W), BF16),
        jax.ShapeDtypeStruct((m, MIX_W), F32),
        jax.ShapeDtypeStruct((m, MIX_W), F32),
        jax.ShapeDtypeStruct((m, MIX_W), BF16),
        jax.ShapeDtypeStruct((m, MIX_W), F32),
        jax.ShapeDtypeStruct((m, MIX_W), F32),
        jax.ShapeDtypeStruct((m, d), BF16),
        jax.ShapeDtypeStruct((m, d), BF16),
    )
    return pl.pallas_call(
        functools.partial(_inproj_kernel, gpt=gpt, segs=segs),
        out_shape=out_shape,
        grid=(m // bm, n // tn),
        in_specs=[
            pl.BlockSpec((bm, d), lambda i, j: (i, 0)),
            _mod_spec(gpt, rows_per_group, bm, group0, 1, d),
            _mod_spec(gpt, rows_per_group, bm, group0, 0, d),
            pl.BlockSpec((d, tn), lambda i, j: (0, j)),
        ],
        out_specs=tuple(out_spec(s) for s in segs),
        scratch_shapes=[pltpu.VMEM((bm, d), BF16)],
        compiler_params=_cparams(("arbitrary", "arbitrary")),
        name="inproj",
    )(x, mod_l, mod_l, w_in)


def _t5_bucket(rel):
    nb = T5_BUCKETS // 2
    max_exact = nb // 2
    rel = np.asarray(rel, np.int32)
    bucket = (rel > 0).astype(np.int32) * nb
    n = np.abs(rel)
    nf = np.maximum(n, 1).astype(np.float32)
    ratio = np.log(nf / np.float32(max_exact)) / np.float32(math.log(T5_MAX_DIST / max_exact))
    large = max_exact + (ratio * np.float32(nb - max_exact)).astype(np.int32)
    large = np.minimum(large, nb - 1)
    return bucket + np.where(n < max_exact, n, large)


def _t5_far_bias(t5_bias):
    return t5_bias[int(_t5_bucket(-T5_MAX_DIST))]


def _attn_a_bias_tiles(t5_bias, t):
    assert t >= T5_MAX_DIST
    q = np.arange(t)[:, None]
    k = np.arange(t)[None, :]
    far = _t5_far_bias(t5_bias)
    visible = (k // CHUNK) <= (q // CHUNK)
    diag = jnp.where(visible[..., None], t5_bias[_t5_bucket(k - q)] - far, NEG)
    prev = t5_bias[_t5_bucket(k - t - q)] - far
    return jnp.moveaxis(jnp.stack([diag, prev]), -1, 0).astype(F32)


def _attn_b_bias_tile(rel_bias_l, tq, win):
    q = np.arange(tq)[:, None]
    kw = np.arange(win)[None, :] - BAND_REACH
    rel = np.clip(kw - q, -REL_CLIP, REL_CLIP) + REL_CLIP
    dchunk = q // CHUNK - np.floor_divide(kw, CHUNK)
    visible = (dchunk >= 0) & (dchunk <= BAND_PREV_CHUNKS)
    bias = jnp.where(visible[..., None], rel_bias_l[rel], NEG)
    return jnp.moveaxis(bias, -1, 0).astype(F32)


def _diff_lambda(lam_ref, lam_init):
    lv = lam_ref[...]
    s1 = jnp.sum(lv[0:1] * lv[1:2], axis=-1, keepdims=True)
    s2 = jnp.sum(lv[2:3] * lv[3:4], axis=-1, keepdims=True)
    return jnp.exp(s1) - jnp.exp(s2) + lam_init


def _stack_diff_queries(q):
    lane = lax.broadcasted_iota(jnp.int32, q.shape, 1)
    zero = jnp.zeros_like(q)
    return jnp.concatenate([jnp.where(lane < DK_A, q, zero), jnp.where(lane >= DK_A, q, zero)], axis=0)


def _diff_finish(o_pair, lam, g, lam_init):
    t = o_pair.shape[0] // 2
    o = o_pair[:t] - lam * o_pair[t:]
    o = o * lax.rsqrt(jnp.mean(o * o, axis=-1, keepdims=True) + LN_EPS) * g
    return o * (1.0 - lam_init)


def _attn_a_kernel(q_ref, k_ref, v_ref, bias_ref, lam_ref, g_ref, o_ref,
                   k_scr, v_scr, m_scr, l_scr, acc_scr, *, t, lam_init):
    qi = pl.program_id(2)

    @pl.when(qi == 0)
    def _():
        k_scr[...] = k_ref[...].astype(BF16)
        v_scr[...] = v_ref[...].astype(BF16)

    q2 = _stack_diff_queries(q_ref[...])
    m_scr[...] = jnp.full(m_scr.shape, -jnp.inf, F32)
    l_scr[...] = jnp.zeros(l_scr.shape, F32)
    acc_scr[...] = jnp.zeros(acc_scr.shape, F32)

    def step(kb, bias):
        start = pl.multiple_of(kb * t, t)
        k = k_scr[pl.ds(start, t), :]
        v = v_scr[pl.ds(start, t), :]
        s = lax.dot_general(q2, k, (((1,), (1,)), ((), ())), preferred_element_type=F32)
        if bias is not None:
            s = (s.reshape(2, t, t) + bias[None]).reshape(2 * t, t)
        m_prev = m_scr[...]
        m_new = jnp.maximum(m_prev, jnp.max(s, axis=-1, keepdims=True))
        alpha = jnp.exp(m_prev - m_new)
        e = jnp.exp(s - m_new)
        l_scr[...] = alpha * l_scr[...] + jnp.sum(e, axis=-1, keepdims=True)
        acc_scr[...] = alpha * acc_scr[...] + jnp.dot(e.astype(BF16), v, preferred_element_type=F32)
        m_scr[...] = m_new

    def far_body(kb, carry):
        step(kb, None)
        return carry

    lax.fori_loop(0, jnp.maximum(qi - 1, 0), far_body, 0)

    @pl.when(qi >= 1)
    def _():
        step(qi - 1, bias_ref[1])

    step(qi, bias_ref[0])

    o_pair = acc_scr[...] / l_scr[...]
    lam = _diff_lambda(lam_ref, lam_init)
    o_ref[...] = _diff_finish(o_pair, lam, g_ref[...], lam_init).astype(BF16)


def _attn_a_call(qa, ka, va, bias_tiles, lam_vecs, g, *, batch, seq, lam_init):
    t = TQ_A
    nq = seq // t
    return pl.pallas_call(
        functools.partial(_attn_a_kernel, t=t, lam_init=lam_init),
        out_shape=jax.ShapeDtypeStruct((batch * seq, MIX_W), BF16),
        grid=(batch, N_HEADS, nq),
        in_specs=[
            pl.BlockSpec((t, HEAD_W), lambda b, h, qi: (b * nq + qi, h)),
            pl.BlockSpec((seq, HEAD_W), lambda b, h, qi: (b, h)),
            pl.BlockSpec((seq, HEAD_W), lambda b, h, qi: (b, h)),
            pl.BlockSpec((None, 2, t, t), lambda b, h, qi: (h, 0, 0, 0)),
            pl.BlockSpec((4, DK_A), lambda b, h, qi: (0, 0)),
            pl.BlockSpec((1, HEAD_W), lambda b, h, qi: (0, 0)),
        ],
        out_specs=pl.BlockSpec((t, HEAD_W), lambda b, h, qi: (b * nq + qi, h)),
        scratch_shapes=[
            pltpu.VMEM((seq, HEAD_W), BF16),
            pltpu.VMEM((seq, HEAD_W), BF16),
            pltpu.VMEM((2 * t, 1), F32),
            pltpu.VMEM((2 * t, 1), F32),
            pltpu.VMEM((2 * t, HEAD_W), F32),
        ],
        compiler_params=_cparams(("arbitrary", "arbitrary", "arbitrary")),
        name="attn_a",
    )(qa, ka, va, bias_tiles, lam_vecs, g)


def _attn_b_kernel(q_ref, k_ref, v_ref, bias_ref, o_ref, k_scr, v_scr, *, tq, win, seq):
    qi = pl.program_id(2)

    @pl.when(qi == 0)
    def _():
        k_scr[0:BAND_REACH, :] = jnp.zeros((BAND_REACH, HEAD_W), BF16)
        v_scr[0:BAND_REACH, :] = jnp.zeros((BAND_REACH, HEAD_W), BF16)
        k_scr[BAND_REACH:BAND_REACH + seq, :] = k_ref[...].astype(BF16)
        v_scr[BAND_REACH:BAND_REACH + seq, :] = v_ref[...].astype(BF16)

    start = pl.multiple_of(qi * tq, tq)
    k = k_scr[pl.ds(start, win), :]
    v = v_scr[pl.ds(start, win), :]
    s = lax.dot_general(q_ref[...], k, (((1,), (1,)), ((), ())), preferred_element_type=F32)
    s = s * (HEAD_W ** -0.5) + bias_ref[...]
    col = lax.broadcasted_iota(jnp.int32, s.shape, 1)
    s = jnp.where(col + qi * tq >= BAND_REACH, s, NEG)
    m = jnp.max(s, axis=-1, keepdims=True)
    e = jnp.exp(s - m)
    l = jnp.sum(e, axis=-1, keepdims=True)
    o = jnp.dot(e.astype(BF16), v, preferred_element_type=F32) / l
    o_ref[...] = o.astype(BF16)


def _attn_b_call(qb, kb, vb, bias_tile, *, batch, seq):
    tq, win = TQ_B, WIN_B
    nq = seq // tq
    return pl.pallas_call(
        functools.partial(_attn_b_kernel, tq=tq, win=win, seq=seq),
        out_shape=jax.ShapeDtypeStruct((batch * seq, MIX_W), BF16),
        grid=(batch, N_HEADS, nq),
        in_specs=[
            pl.BlockSpec((tq, HEAD_W), lambda b, h, qi: (b * nq + qi, h)),
            pl.BlockSpec((seq, HEAD_W), lambda b, h, qi: (b, h)),
            pl.BlockSpec((seq, HEAD_W), lambda b, h, qi: (b, h)),
            pl.BlockSpec((None, tq, win), lambda b, h, qi: (h, 0, 0)),
        ],
        out_specs=pl.BlockSpec((tq, HEAD_W), lambda b, h, qi: (b * nq + qi, h)),
        scratch_shapes=[
            pltpu.VMEM((BAND_REACH + seq, HEAD_W), BF16),
            pltpu.VMEM((BAND_REACH + seq, HEAD_W), BF16),
        ],
        compiler_params=_cparams(("arbitrary", "arbitrary", "arbitrary")),
        name="attn_b",
    )(qb, kb, vb, bias_tile)


def _softmax_pv(parts):
    m = None
    for s, _ in parts:
        pm = jnp.max(s, axis=-1, keepdims=True)
        m = pm if m is None else jnp.maximum(m, pm)
    l = None
    o = None
    for s, v in parts:
        e = jnp.exp(s - m)
        pl_ = jnp.sum(e, axis=-1, keepdims=True)
        po = jnp.dot(e.astype(BF16), v, preferred_element_type=F32)
        l = pl_ if l is None else l + pl_
        o = po if o is None else o + po
    return o / l


def _nt_dot(a, b):
    return lax.dot_general(a, b, (((1,), (1,)), ((), ())), preferred_element_type=F32)


def _sample_kernel(qa_ref, ka_ref, va_ref, cak_ref, cav_ref, qb_ref, kb_ref, vb_ref, cbk_ref, cbv_ref,
                   ba_c_ref, ba_n_ref, bb_c_ref, bb_n_ref, lam_ref, g_ref, oa_ref, ob_ref,
                   *, t_new, lam_init):
    lam = _diff_lambda(lam_ref, lam_init)
    g = g_ref[...]
    for h in range(N_HEADS):
        cols = slice(h * HEAD_W, (h + 1) * HEAD_W)
        q2 = _stack_diff_queries(qa_ref[:, cols])
        kc = cak_ref[:, cols].astype(BF16)
        vc = cav_ref[:, cols].astype(BF16)
        kn = ka_ref[:, cols].astype(BF16)
        vn = va_ref[:, cols].astype(BF16)
        s_c = _nt_dot(q2, kc)
        s_n = _nt_dot(q2, kn)
        s_c = (s_c.reshape(2, t_new, -1) + ba_c_ref[h][None]).reshape(2 * t_new, -1)
        s_n = (s_n.reshape(2, t_new, -1) + ba_n_ref[h][None]).reshape(2 * t_new, -1)
        o_pair = _softmax_pv([(s_c, vc), (s_n, vn)])
        oa_ref[:, cols] = _diff_finish(o_pair, lam, g, lam_init).astype(BF16)
        qh = qb_ref[:, cols]
        kc = cbk_ref[:, cols].astype(BF16)
        vc = cbv_ref[:, cols].astype(BF16)
        kn = kb_ref[:, cols].astype(BF16)
        vn = vb_ref[:, cols].astype(BF16)
        s_c = _nt_dot(qh, kc) * (HEAD_W ** -0.5) + bb_c_ref[h]
        s_n = _nt_dot(qh, kn) * (HEAD_W ** -0.5) + bb_n_ref[h]
        ob_ref[:, cols] = _softmax_pv([(s_c, vc), (s_n, vn)]).astype(BF16)


def _sample_call(l, qa, ka, va, cache_a_k, cache_a_v, qb, kb, vb, cache_b_k, cache_b_v,
                 ba_c, ba_n, bb_c, bb_n, lam_vecs, g, *, batch, t_new, lam_init):
    past = cache_a_k.shape[2]
    band_past = cache_b_k.shape[2]
    row = lambda b: (b, 0)
    cache = lambda b: (l, b, 0, 0)
    const3 = lambda b: (0, 0, 0)
    const2 = lambda b: (0, 0)
    return pl.pallas_call(
        functools.partial(_sample_kernel, t_new=t_new, lam_init=lam_init),
        out_shape=(jax.ShapeDtypeStruct((batch * t_new, MIX_W), BF16),
                   jax.ShapeDtypeStruct((batch * t_new, MIX_W), BF16)),
        grid=(batch,),
        in_specs=[
            pl.BlockSpec((t_new, MIX_W), row),
            pl.BlockSpec((t_new, MIX_W), row),
            pl.BlockSpec((t_new, MIX_W), row),
            pl.BlockSpec((None, None, past, MIX_W), cache),
            pl.BlockSpec((None, None, past, MIX_W), cache),
            pl.BlockSpec((t_new, MIX_W), row),
            pl.BlockSpec((t_new, MIX_W), row),
            pl.BlockSpec((t_new, MIX_W), row),
            pl.BlockSpec((None, None, band_past, MIX_W), cache),
            pl.BlockSpec((None, None, band_past, MIX_W), cache),
            pl.BlockSpec(ba_c.shape, const3),
            pl.BlockSpec(ba_n.shape, const3),
            pl.BlockSpec(bb_c.shape, const3),
            pl.BlockSpec(bb_n.shape, const3),
            pl.BlockSpec((4, DK_A), const2),
            pl.BlockSpec((1, HEAD_W), const2),
        ],
        out_specs=(pl.BlockSpec((t_new, MIX_W), row), pl.BlockSpec((t_new, MIX_W), row)),
        compiler_params=_cparams(("arbitrary",)),
        name="sample_mix",
    )(qa, ka, va, cache_a_k, cache_a_v, qb, kb, vb, cache_b_k, cache_b_v,
      ba_c, ba_n, bb_c, bb_n, lam_vecs, g)


def _mixout_kernel(oa_ref, ob_ref, sga_ref, sgb_ref, x_ref, gm_ref, woa_ref, wob_ref, wout_ref,
                   lg_ref, lb_ref, o_ref, *, gpt, alpha):
    a = jnp.dot(oa_ref[...], woa_ref[...], preferred_element_type=F32)
    b = jnp.dot(ob_ref[...], wob_ref[...], preferred_element_type=F32)
    merged = sga_ref[...].astype(F32) * a + sgb_ref[...].astype(F32) * b
    y = jnp.dot(merged.astype(BF16), wout_ref[...], preferred_element_type=F32)
    z = alpha * x_ref[...] + _gate(y, gm_ref[...], gpt)
    o_ref[...] = _ln(z) * lg_ref[...] + lb_ref[...]


def _mixout_call(oa, ob, sga, sgb, x, mod_l, w_oa, w_ob, w_out, ln_g, ln_b,
                 *, rows_per_group, group0, bm, alpha):
    m, d = x.shape
    gpt = max(1, bm // rows_per_group)
    row = lambda i: (i, 0)
    const = lambda i: (0, 0)
    resident = pl.Buffered(1)
    return pl.pallas_call(
        functools.partial(_mixout_kernel, gpt=gpt, alpha=alpha),
        out_shape=jax.ShapeDtypeStruct((m, d), F32),
        grid=(m // bm,),
        in_specs=[
            pl.BlockSpec((bm, MIX_W), row),
            pl.BlockSpec((bm, MIX_W), row),
            pl.BlockSpec((bm, d), row),
            pl.BlockSpec((bm, d), row),
            pl.BlockSpec((bm, d), row),
            _mod_spec(gpt, rows_per_group, bm, group0, 2, d),
            pl.BlockSpec((MIX_W, d), const, pipeline_mode=resident),
            pl.BlockSpec((MIX_W, d), const, pipeline_mode=resident),
            pl.BlockSpec((d, d), const, pipeline_mode=resident),
            pl.BlockSpec((1, d), const),
            pl.BlockSpec((1, d), const),
        ],
        out_specs=pl.BlockSpec((bm, d), row),
        compiler_params=_cparams(("arbitrary",)),
        name="mixout",
    )(oa, ob, sga, sgb, x, mod_l, w_oa, w_ob, w_out, ln_g, ln_b)


def _ffn_kernel(x_ref, sc_ref, sh_ref, gf_ref, w1_ref, w3_ref, w2_ref, lg_ref, lb_ref, o_ref,
                h_scr, acc_scr, *, gpt, alpha):
    kf = pl.program_id(1)

    @pl.when(kf == 0)
    def _():
        h = _modulate(_ln(x_ref[...]), sc_ref[...], sh_ref[...], gpt)
        h_scr[...] = h.astype(BF16)
        acc_scr[...] = jnp.zeros(acc_scr.shape, F32)

    h = h_scr[...]
    a = jnp.dot(h, w1_ref[...], preferred_element_type=F32)
    b = jnp.dot(h, w3_ref[...], preferred_element_type=F32)
    u = (a * _sigmoid(a)) * b
    acc_scr[...] += jnp.dot(u.astype(BF16), w2_ref[...], preferred_element_type=F32)

    @pl.when(kf == pl.num_programs(1) - 1)
    def _():
        z = alpha * x_ref[...] + _gate(acc_scr[...], gf_ref[...], gpt)
        o_ref[...] = _ln(z) * lg_ref[...] + lb_ref[...]


def _ffn_call(x, mod_l, w1, w3, w2, ln_g, ln_b, *, rows_per_group, group0, bm, tf, alpha):
    m, d = x.shape
    dff = w1.shape[1]
    gpt = max(1, bm // rows_per_group)
    return pl.pallas_call(
        functools.partial(_ffn_kernel, gpt=gpt, alpha=alpha),
        out_shape=jax.ShapeDtypeStruct((m, d), F32),
        grid=(m // bm, dff // tf),
        in_specs=[
            pl.BlockSpec((bm, d), lambda i, k: (i, 0)),
            _mod_spec(gpt, rows_per_group, bm, group0, 4, d),
            _mod_spec(gpt, rows_per_group, bm, group0, 3, d),
            _mod_spec(gpt, rows_per_group, bm, group0, 5, d),
            pl.BlockSpec((d, tf), lambda i, k: (0, k)),
            pl.BlockSpec((d, tf), lambda i, k: (0, k)),
            pl.BlockSpec((tf, d), lambda i, k: (k, 0)),
            pl.BlockSpec((1, d), lambda i, k: (0, 0)),
            pl.BlockSpec((1, d), lambda i, k: (0, 0)),
        ],
        out_specs=pl.BlockSpec((bm, d), lambda i, k: (i, 0)),
        scratch_shapes=[pltpu.VMEM((bm, d), BF16), pltpu.VMEM((bm, d), F32)],
        compiler_params=_cparams(("arbitrary", "arbitrary")),
        name="ffn",
    )(x, mod_l, mod_l, mod_l, w1, w3, w2, ln_g, ln_b)


def _ff_tile(dff):
    for tf in (512, 256, 128):
        if dff % tf == 0:
            return tf
    raise ValueError(f"unsupported FFN width {dff}")


def kernel(x_prompt, x_sample, cache_a_k, cache_a_v, cache_b_k, cache_b_v, c_prompt, c_sample,
           w_mod, b_mod, w_in, lambda_q1, lambda_k1, lambda_q2, lambda_k2, subln_g, t5_bias,
           rel_bias, w_oa, w_ob, w_out, ln1_g, ln1_b, w1, w3, w2, ln2_g, ln2_b):
    depth = w_mod.shape[0]
    batch, seq, d = x_prompt.shape
    dec_batch, t_new, _ = x_sample.shape
    past = cache_a_k.shape[2]
    band_past = cache_b_k.shape[2]
    assert past % CHUNK == 0 and t_new <= CHUNK and band_past == BAND_REACH and past >= BAND_REACH
    assert seq % TQ_A == 0 and seq >= BAND_REACH and d % 128 == 0
    alpha = (2 * depth) ** 0.25
    prompt_band = min(BAND_REACH, seq)

    bm_p = min(512, seq)
    bm_s = min(512, dec_batch * t_new)
    tn = math.gcd(MIX_W, d)
    tf = _ff_tile(w1.shape[2])

    c_all = jnp.concatenate([c_sample, c_prompt], axis=0)
    mod = _mod_call(c_all, w_mod, b_mod).reshape(depth, dec_batch + batch, 1, 6 * d)

    ta_tiles = _attn_a_bias_tiles(t5_bias, TQ_A)
    far = _t5_far_bias(t5_bias)
    q_pos_s = past + np.arange(t_new)
    rel_a = np.arange(past + t_new)[None, :] - q_pos_s[:, None]
    ba_s = jnp.moveaxis(t5_bias[_t5_bucket(rel_a)] - far, -1, 0).astype(F32)
    kb_pos_s = past - band_past + np.arange(band_past + t_new)
    rel_b = np.clip(kb_pos_s[None, :] - q_pos_s[:, None], -REL_CLIP, REL_CLIP) + REL_CLIP

    cak = cache_a_k.reshape(depth, dec_batch, past, MIX_W)
    cav = cache_a_v.reshape(depth, dec_batch, past, MIX_W)
    cbk = cache_b_k.reshape(depth, dec_batch, band_past, MIX_W)
    cbv = cache_b_v.reshape(depth, dec_batch, band_past, MIX_W)

    xp = x_prompt.reshape(batch * seq, d)
    xs = x_sample.reshape(dec_batch * t_new, d)
    outs = [[] for _ in range(8)]
    for l in range(depth):
        lam_init = 0.8 - 0.6 * math.exp(-0.3 * l)
        lam_vecs = jnp.stack([lambda_q1[l], lambda_k1[l], lambda_q2[l], lambda_k2[l]]).astype(F32)
        g = subln_g[l].reshape(1, HEAD_W).astype(F32)
        mod_l = mod[l:l + 1]
        w_in_l = w_in[l].astype(BF16)
        w_oa_l, w_ob_l, w_out_l = w_oa[l].astype(BF16), w_ob[l].astype(BF16), w_out[l].astype(BF16)
        w1_l, w3_l, w2_l = w1[l].astype(BF16), w3[l].astype(BF16), w2[l].astype(BF16)
        ln1 = (ln1_g[l].reshape(1, d), ln1_b[l].reshape(1, d))
        ln2 = (ln2_g[l].reshape(1, d), ln2_b[l].reshape(1, d))
        bb_tile = _attn_b_bias_tile(rel_bias[l], TQ_B, WIN_B)
        bb_s = jnp.moveaxis(rel_bias[l][rel_b], -1, 0).astype(F32)

        grp = dict(rows_per_group=seq, group0=dec_batch, bm=bm_p)
        qa, ka, va, qb, kb, vb, sga, sgb = _inproj_call(xp, mod_l, w_in_l, tn=tn, **grp)
        oa = _attn_a_call(qa, ka, va, ta_tiles, lam_vecs, g, batch=batch, seq=seq, lam_init=lam_init)
        ob = _attn_b_call(qb, kb, vb, bb_tile, batch=batch, seq=seq)
        xp = _mixout_call(oa, ob, sga, sgb, xp, mod_l, w_oa_l, w_ob_l, w_out_l, *ln1, alpha=alpha, **grp)
        xp = _ffn_call(xp, mod_l, w1_l, w3_l, w2_l, *ln2, tf=tf, alpha=alpha, **grp)
        outs[0].append(ka.reshape(batch, seq, N_HEADS, 2, DK_A))
        outs[1].append(va.reshape(batch, seq, N_HEADS, HEAD_W))
        outs[2].append(kb.reshape(batch, seq, N_HEADS, HEAD_W)[:, seq - prompt_band:])
        outs[3].append(vb.reshape(batch, seq, N_HEADS, HEAD_W)[:, seq - prompt_band:])

        grp = dict(rows_per_group=t_new, group0=0, bm=bm_s)
        qa, ka, va, qb, kb, vb, sga, sgb = _inproj_call(xs, mod_l, w_in_l, tn=tn, **grp)
        oa, ob = _sample_call(l, qa, ka, va, cak, cav, qb, kb, vb, cbk, cbv,
                              ba_s[:, :, :past], ba_s[:, :, past:], bb_s[:, :, :band_past],
                              bb_s[:, :, band_past:], lam_vecs, g,
                              batch=dec_batch, t_new=t_new, lam_init=lam_init)
        xs = _mixout_call(oa, ob, sga, sgb, xs, mod_l, w_oa_l, w_ob_l, w_out_l, *ln1, alpha=alpha, **grp)
        xs = _ffn_call(xs, mod_l, w1_l, w3_l, w2_l, *ln2, tf=tf, alpha=alpha, **grp)
        outs[4].append(ka.reshape(dec_batch, t_new, N_HEADS, 2, DK_A))
        outs[5].append(va.reshape(dec_batch, t_new, N_HEADS, HEAD_W))
        outs[6].append(kb.reshape(dec_batch, t_new, N_HEADS, HEAD_W))
        outs[7].append(vb.reshape(dec_batch, t_new, N_HEADS, HEAD_W))

    return (xp.reshape(batch, seq, d), xs.reshape(dec_batch, t_new, d)) + tuple(jnp.stack(o) for o in outs)
```

```python
import functools
import math

import jax
import jax.numpy as jnp
import numpy as np
from jax import lax
from jax.experimental import pallas as pl
from jax.experimental.pallas import tpu as pltpu

F32 = jnp.float32
BF16 = jnp.bfloat16

CHUNK = 64
N_HEADS = 8
HEAD_W = 128
DK_A = 64
BAND_PREV_CHUNKS = 8
BAND_REACH = BAND_PREV_CHUNKS * CHUNK
REL_CLIP = 128
T5_BUCKETS = 32
T5_MAX_DIST = 128
NEG = -1e30
LN_EPS = 1e-5
MIX_W = N_HEADS * HEAD_W
LOG2E = math.log2(math.e)
QA_SCALE = DK_A ** -0.5 * LOG2E
QB_SCALE = HEAD_W ** -0.5 * LOG2E

V7X_VMEM_BYTES = 64 * 1024 * 1024
VMEM_LIMIT = V7X_VMEM_BYTES - 8 * 1024 * 1024

TQ_A = 256
HEADS_PER_STEP_A = 4
TQ_B = 128
WIN_B = BAND_REACH + TQ_B
UNITS_B = 4


def _cparams(sem):
    return pltpu.CompilerParams(dimension_semantics=sem, vmem_limit_bytes=VMEM_LIMIT)


def _sigmoid(x):
    return 1.0 / (1.0 + jnp.exp(-x))


def _ln(x):
    mu = jnp.mean(x, axis=-1, keepdims=True)
    xc = x - mu
    var = jnp.mean(xc * xc, axis=-1, keepdims=True)
    return xc * lax.rsqrt(var + LN_EPS)


def _modulate(y, scale, shift, gpt):
    if gpt == 1:
        return y * (1.0 + scale[0]) + shift[0]
    bm, d = y.shape
    y3 = y.reshape(gpt, bm // gpt, d)
    return (y3 * (1.0 + scale) + shift).reshape(bm, d)


def _gate(y, gate, gpt):
    if gpt == 1:
        return y * gate[0]
    bm, d = y.shape
    return (y.reshape(gpt, bm // gpt, d) * gate).reshape(bm, d)


def _nt_dot(a, b):
    return lax.dot_general(a, b, (((1,), (1,)), ((), ())), preferred_element_type=F32)


def _mod_kernel(c_ref, w_ref, b_ref, o_ref):
    c = c_ref[...]
    s = (c * _sigmoid(c)).astype(BF16)
    o_ref[...] = jnp.dot(s, w_ref[...].astype(BF16), preferred_element_type=F32) + b_ref[...]


def _mod_call(c_all, w_mod, b_mod):
    depth, d, n = w_mod.shape
    g = c_all.shape[0]
    tn = math.gcd(n, 1024)
    return pl.pallas_call(
        _mod_kernel,
        out_shape=jax.ShapeDtypeStruct((depth, g, n), F32),
        grid=(depth, n // tn),
        in_specs=[
            pl.BlockSpec((g, d), lambda l, j: (0, 0)),
            pl.BlockSpec((None, d, tn), lambda l, j: (l, 0, j)),
            pl.BlockSpec((None, 1, tn), lambda l, j: (l, 0, j)),
        ],
        out_specs=pl.BlockSpec((None, g, tn), lambda l, j: (l, 0, j)),
        compiler_params=_cparams(("arbitrary", "arbitrary")),
        name="mod",
    )(c_all, w_mod, b_mod.reshape(depth, 1, n))


def _inproj_segments(d, tn):
    widths = (MIX_W,) * 6 + (d, d)
    segs, lo = [], 0
    for w in widths:
        assert w % tn == 0
        segs.append((lo, w // tn))
        lo += w // tn
    return tuple(segs)


def _inproj_kernel(*refs, gpt, segs, n_carry):
    x_ref, sc_ref, sh_ref, w_ref = refs[:4]
    outs_refs = refs[4 + n_carry:12 + n_carry]
    h_scr = refs[12 + n_carry]
    qa_ref, ka_ref, va_ref, qb_ref, kb_ref, vb_ref, sga_ref, sgb_ref = outs_refs
    j = pl.program_id(1)

    @pl.when(j == 0)
    def _():
        h = _modulate(_ln(x_ref[...]), sc_ref[...], sh_ref[...], gpt)
        h_scr[...] = h.astype(BF16)

    acc = jnp.dot(h_scr[...], w_ref[...], preferred_element_type=F32)

    def plain(v):
        return v

    def scaled_bf16(scale):
        return lambda v: (v * scale).astype(BF16)

    def gate_bf16(v):
        return _sigmoid(v).astype(BF16)

    outs = ((qa_ref, scaled_bf16(QA_SCALE)), (ka_ref, plain), (va_ref, plain),
            (qb_ref, scaled_bf16(QB_SCALE)), (kb_ref, plain), (vb_ref, plain),
            (sga_ref, gate_bf16), (sgb_ref, gate_bf16))
    for (lo, cnt), (ref, fn) in zip(segs, outs):
        @pl.when((j >= lo) & (j < lo + cnt))
        def _(ref=ref, fn=fn):
            ref[...] = fn(acc)


def _mod_spec(gpt, rows_per_group, bm, group0, which, d):
    if gpt == 1:
        tiles_per_group = rows_per_group // bm
        return pl.BlockSpec((None, 1, 1, d), lambda i, *_: (0, group0 + i // tiles_per_group, 0, which))
    return pl.BlockSpec((None, gpt, 1, d), lambda i, *_: (0, group0 // gpt + i, 0, which))


def _inproj_call(x, mod_l, w_in, kv_carry, *, layer, depth, rows_per_group, group0, bm, tn):
    m, d = x.shape
    n = w_in.shape[1]
    gpt = max(1, bm // rows_per_group)
    segs = _inproj_segments(d, tn)
    assert n == tn * (segs[-1][0] + segs[-1][1])
    n_carry = len(kv_carry)

    def col(seg):
        lo, cnt = seg
        return lambda j: jnp.clip(j - lo, 0, cnt - 1)

    def out_spec(seg):
        c = col(seg)
        return pl.BlockSpec((bm, tn), lambda i, j: (i, c(j)))

    def slab_spec(seg):
        c = col(seg)
        return pl.BlockSpec((None, bm, tn), lambda i, j: (layer, i, c(j)))

    out_shape = (
        jax.ShapeDtypeStruct((m, MIX_W), BF16),
        jax.ShapeDtypeStruct((depth, m, MIX_W), F32),
        jax.ShapeDtypeStruct((depth, m, MIX_W), F32),
        jax.ShapeDtypeStruct((m, MIX_W), BF16),
        jax.ShapeDtypeStruct((m, MIX_W), F32),
        jax.ShapeDtypeStruct((m, MIX_W), F32),
        jax.ShapeDtypeStruct((m, d), BF16),
        jax.ShapeDtypeStruct((m, d), BF16),
    )
    out_specs = (out_spec(segs[0]), slab_spec(segs[1]), slab_spec(segs[2]), out_spec(segs[3]),
                 out_spec(segs[4]), out_spec(segs[5]), out_spec(segs[6]), out_spec(segs[7]))
    return pl.pallas_call(
        functools.partial(_inproj_kernel, gpt=gpt, segs=segs, n_carry=n_carry),
        out_shape=out_shape,
        grid=(m // bm, n // tn),
        in_specs=[
            pl.BlockSpec((bm, d), lambda i, j: (i, 0)),
            _mod_spec(gpt, rows_per_group, bm, group0, 1, d),
            _mod_spec(gpt, rows_per_group, bm, group0, 0, d),
            pl.BlockSpec((d, tn), lambda i, j: (0, j)),
        ] + [pl.BlockSpec(memory_space=pl.ANY)] * n_carry,
        out_specs=out_specs,
        scratch_shapes=[pltpu.VMEM((bm, d), BF16)],
        input_output_aliases={4 + c: 1 + c for c in range(n_carry)},
        compiler_params=_cparams(("arbitrary", "arbitrary")),
        name="inproj",
    )(x, mod_l, mod_l, w_in, *kv_carry)


def _t5_bucket(rel):
    nb = T5_BUCKETS // 2
    max_exact = nb // 2
    rel = np.asarray(rel, np.int32)
    bucket = (rel > 0).astype(np.int32) * nb
    n = np.abs(rel)
    nf = np.maximum(n, 1).astype(np.float32)
    ratio = np.log(nf / np.float32(max_exact)) / np.float32(math.log(T5_MAX_DIST / max_exact))
    large = max_exact + (ratio * np.float32(nb - max_exact)).astype(np.int32)
    large = np.minimum(large, nb - 1)
    return bucket + np.where(n < max_exact, n, large)


def _lookup(table, idx):
    n = table.shape[0]
    flat = jnp.asarray(np.asarray(idx, np.int32).reshape(-1))
    onehot = (flat[None, :] == jnp.arange(n, dtype=jnp.int32)[:, None]).astype(F32)
    out = lax.dot_general(table.astype(F32), onehot, (((0,), (0,)), ((), ())),
                          precision=lax.Precision.HIGHEST, preferred_element_type=F32)
    return out.reshape((table.shape[1],) + tuple(np.shape(idx)))


def _t5_table(t5_bias):
    far = t5_bias[int(_t5_bucket(-T5_MAX_DIST))]
    return (t5_bias - far[None, :]) * LOG2E


def _attn_a_bias_tiles(t5_tab, t):
    assert t >= T5_MAX_DIST
    k = np.arange(t)[:, None]
    q = np.arange(t)[None, :]
    visible = (k // CHUNK) <= (q // CHUNK)
    idx = np.stack([_t5_bucket(k - q), _t5_bucket(k - t - q)])
    vis = np.stack([visible, np.ones_like(visible)])
    return jnp.where(vis[None], _lookup(t5_tab, idx), NEG)


def _attn_b_bias_tile(rel_tab, tq, win):
    kw = np.arange(win)[:, None] - BAND_REACH
    q = np.arange(tq)[None, :]
    rel = np.clip(kw - q, -REL_CLIP, REL_CLIP) + REL_CLIP
    dchunk = q // CHUNK - np.floor_divide(kw, CHUNK)
    visible = (dchunk >= 0) & (dchunk <= BAND_PREV_CHUNKS)
    return jnp.where(visible[None], _lookup(rel_tab, rel), NEG)


def _diff_lambda(lam_ref, lam_init):
    lv = lam_ref[...]
    s1 = jnp.sum(lv[0:1] * lv[1:2], axis=-1, keepdims=True)
    s2 = jnp.sum(lv[2:3] * lv[3:4], axis=-1, keepdims=True)
    return jnp.exp(s1) - jnp.exp(s2) + lam_init


def _stack_diff_queries(q):
    lane = lax.broadcasted_iota(jnp.int32, q.shape, 1)
    zero = jnp.zeros_like(q)
    return jnp.concatenate([jnp.where(lane < DK_A, q, zero), jnp.where(lane >= DK_A, q, zero)], axis=0)


def _attn_a_kernel(q_ref, k_ref, v_ref, bias_ref, lam_ref, g_ref, o_ref,
                   k_scr, vt_scr, s_scr, acc_scr, m_scr, l_scr, *, t, nh, lam_init):
    qi = pl.program_id(2)
    nkb = vt_scr.shape[1]
    heads = range(nh)

    @pl.when(qi == 0)
    def _():
        for hh in heads:
            cols = slice(hh * HEAD_W, (hh + 1) * HEAD_W)
            k_scr[hh] = k_ref[:, cols].astype(BF16)
            for kb in range(nkb):
                vt_scr[hh, kb] = v_ref[kb * t:(kb + 1) * t, cols].T.astype(BF16)

    q2 = [_stack_diff_queries(q_ref[:, hh * HEAD_W:(hh + 1) * HEAD_W]) for hh in heads]
    acc_scr[...] = jnp.zeros(acc_scr.shape, F32)

    def scores(hh, kb):
        start = pl.multiple_of(kb * t, t)
        s_scr[hh] = _nt_dot(k_scr[hh, pl.ds(start, t), :], q2[hh])

    def step(kb, ml, bias_idx, prefetch):
        out = []
        for hh in heads:
            m, l = ml[hh]
            s = s_scr[hh]
            if bias_idx is not None:
                bias = bias_ref[hh, bias_idx]
                s = s + jnp.concatenate([bias, bias], axis=1)
            m_new = jnp.maximum(m, jnp.max(s, axis=0, keepdims=True))
            alpha = jnp.exp2(m - m_new)
            e = jnp.exp2(s - m_new)
            l_new = alpha * l + jnp.sum(e, axis=0, keepdims=True)
            pv = jnp.dot(vt_scr[hh, kb], e.astype(BF16), preferred_element_type=F32)
            if prefetch:
                scores(hh, kb + 1)
            acc_scr[hh] = alpha * acc_scr[hh] + pv
            out.append((m_new, l_new))
        return tuple(out)

    def load_ml():
        return tuple((m_scr[hh], l_scr[hh]) for hh in heads)

    def store_ml(ml):
        for hh in heads:
            m_scr[hh] = ml[hh][0]
            l_scr[hh] = ml[hh][1]

    for hh in heads:
        scores(hh, 0)
    ml0 = tuple((jnp.full((1, 2 * t), -jnp.inf, F32), jnp.zeros((1, 2 * t), F32)) for _ in heads)
    store_ml(lax.fori_loop(0, jnp.maximum(qi - 1, 0), lambda kb, ml: step(kb, ml, None, True), ml0))

    @pl.when(qi >= 1)
    def _():
        store_ml(step(qi - 1, load_ml(), 1, True))

    ml = step(qi, load_ml(), 0, False)

    lam = _diff_lambda(lam_ref, lam_init)
    for hh in heads:
        o2 = acc_scr[hh] / ml[hh][1]
        o = o2[:, :t] - lam * o2[:, t:]
        o = o * lax.rsqrt(jnp.mean(o * o, axis=0, keepdims=True) + LN_EPS)
        o_ref[:, hh * HEAD_W:(hh + 1) * HEAD_W] = (o.T * g_ref[...] * (1.0 - lam_init)).astype(BF16)


def _attn_a_call(qa, ka_all, va_all, bias_tiles, lam_vecs, g, *, layer, batch, seq, lam_init):
    t, nh = TQ_A, HEADS_PER_STEP_A
    nq = seq // t
    w = nh * HEAD_W
    return pl.pallas_call(
        functools.partial(_attn_a_kernel, t=t, nh=nh, lam_init=lam_init),
        out_shape=jax.ShapeDtypeStruct((batch * seq, MIX_W), BF16),
        grid=(batch, N_HEADS // nh, nq),
        in_specs=[
            pl.BlockSpec((t, w), lambda b, h, qi: (b * nq + qi, h)),
            pl.BlockSpec((None, seq, w), lambda b, h, qi: (layer, b, h)),
            pl.BlockSpec((None, seq, w), lambda b, h, qi: (layer, b, h)),
            pl.BlockSpec((nh, 2, t, t), lambda b, h, qi: (h, 0, 0, 0)),
            pl.BlockSpec((4, DK_A), lambda b, h, qi: (0, 0)),
            pl.BlockSpec((1, HEAD_W), lambda b, h, qi: (0, 0)),
        ],
        out_specs=pl.BlockSpec((t, w), lambda b, h, qi: (b * nq + qi, h)),
        scratch_shapes=[
            pltpu.VMEM((nh, seq, HEAD_W), BF16),
            pltpu.VMEM((nh, seq // t, HEAD_W, t), BF16),
            pltpu.VMEM((nh, t, 2 * t), F32),
            pltpu.VMEM((nh, HEAD_W, 2 * t), F32),
            pltpu.VMEM((nh, 1, 2 * t), F32),
            pltpu.VMEM((nh, 1, 2 * t), F32),
        ],
        compiler_params=_cparams(("arbitrary", "arbitrary", "arbitrary")),
        name="attn_a",
    )(qa, ka_all, va_all, bias_tiles, lam_vecs, g)


def _attn_b_kernel(q_ref, k_ref, v_ref, bias_ref, o_ref, k_scr, vt_scr, *, tq, win, seq, units):
    qi = pl.program_id(2)
    pad_blocks = BAND_REACH // tq
    win_blocks = win // tq

    @pl.when(qi == 0)
    def _():
        k_scr[0:BAND_REACH, :] = jnp.zeros((BAND_REACH, HEAD_W), BF16)
        k_scr[BAND_REACH:BAND_REACH + seq, :] = k_ref[...].astype(BF16)
        for blk in range(pad_blocks):
            vt_scr[blk] = jnp.zeros((HEAD_W, tq), BF16)
        for blk in range(seq // tq):
            vt_scr[pad_blocks + blk] = v_ref[blk * tq:(blk + 1) * tq, :].T.astype(BF16)

    bias = bias_ref[...]
    row = lax.broadcasted_iota(jnp.int32, (win, tq), 0)
    raw = []
    for u in range(units):
        start = pl.multiple_of((qi * units + u) * tq, tq)
        raw.append(_nt_dot(k_scr[pl.ds(start, win), :], q_ref[u * tq:(u + 1) * tq, :]))
    for u in range(units):
        blk0 = qi * units + u
        s = jnp.where(row + blk0 * tq >= BAND_REACH, raw[u] + bias, NEG)
        m = jnp.max(s, axis=0, keepdims=True)
        e = jnp.exp2(s - m)
        l = jnp.sum(e, axis=0, keepdims=True)
        e = e.astype(BF16)
        acc = None
        for w in range(win_blocks):
            pv = jnp.dot(vt_scr[blk0 + w], e[w * tq:(w + 1) * tq, :], preferred_element_type=F32)
            acc = pv if acc is None else acc + pv
        o_ref[u * tq:(u + 1) * tq, :] = (acc / l).T.astype(BF16)


def _attn_b_call(qb, kb, vb, bias_tile, *, batch, seq):
    tq, win, units = TQ_B, WIN_B, UNITS_B
    rows = tq * units
    nq = seq // rows
    return pl.pallas_call(
        functools.partial(_attn_b_kernel, tq=tq, win=win, seq=seq, units=units),
        out_shape=jax.ShapeDtypeStruct((batch * seq, MIX_W), BF16),
        grid=(batch, N_HEADS, nq),
        in_specs=[
            pl.BlockSpec((rows, HEAD_W), lambda b, h, qi: (b * nq + qi, h)),
            pl.BlockSpec((seq, HEAD_W), lambda b, h, qi: (b, h)),
            pl.BlockSpec((seq, HEAD_W), lambda b, h, qi: (b, h)),
            pl.BlockSpec((None, win, tq), lambda b, h, qi: (h, 0, 0)),
        ],
        out_specs=pl.BlockSpec((rows, HEAD_W), lambda b, h, qi: (b * nq + qi, h)),
        scratch_shapes=[
            pltpu.VMEM((BAND_REACH + seq, HEAD_W), BF16),
            pltpu.VMEM(((BAND_REACH + seq) // tq, HEAD_W, tq), BF16),
        ],
        compiler_params=_cparams(("arbitrary", "arbitrary", "arbitrary")),
        name="attn_b",
    )(qb, kb, vb, bias_tile)


def _softmax_pv(parts):
    m = None
    for s, _ in parts:
        pm = jnp.max(s, axis=-1, keepdims=True)
        m = pm if m is None else jnp.maximum(m, pm)
    l = None
    o = None
    for s, v in parts:
        e = jnp.exp2(s - m)
        pl_ = jnp.sum(e, axis=-1, keepdims=True)
        po = jnp.dot(e.astype(BF16), v, preferred_element_type=F32)
        l = pl_ if l is None else l + pl_
        o = po if o is None else o + po
    return o / l


def _sample_kernel(qa_ref, ka_ref, va_ref, cakt_ref, cav_ref, qb_ref, kb_ref, vb_ref, cbk_ref, cbv_ref,
                   ba_c_ref, ba_n_ref, bb_c_ref, bb_n_ref, lam_ref, g_ref, oa_ref, ob_ref,
                   *, t_new, lam_init):
    lam = _diff_lambda(lam_ref, lam_init)
    g = g_ref[...]
    past = cakt_ref.shape[-1]
    for h in range(N_HEADS):
        cols = slice(h * HEAD_W, (h + 1) * HEAD_W)
        q2 = _stack_diff_queries(qa_ref[:, cols])
        kct = cakt_ref[h].reshape(2 * DK_A, past).astype(BF16)
        s_c = jnp.dot(q2, kct, preferred_element_type=F32)
        s_n = _nt_dot(q2, ka_ref[:, cols].astype(BF16))
        s_c = (s_c.reshape(2, t_new, -1) + ba_c_ref[h][None]).reshape(2 * t_new, -1)
        s_n = (s_n.reshape(2, t_new, -1) + ba_n_ref[h][None]).reshape(2 * t_new, -1)
        o2 = _softmax_pv([(s_c, cav_ref[:, h, :].astype(BF16)), (s_n, va_ref[:, cols].astype(BF16))])
        o = o2[:t_new] - lam * o2[t_new:]
        o = o * lax.rsqrt(jnp.mean(o * o, axis=-1, keepdims=True) + LN_EPS) * g
        oa_ref[:, cols] = (o * (1.0 - lam_init)).astype(BF16)
        qh = qb_ref[:, cols]
        s_c = _nt_dot(qh, cbk_ref[:, h, :].astype(BF16)) + bb_c_ref[h]
        s_n = _nt_dot(qh, kb_ref[:, cols].astype(BF16)) + bb_n_ref[h]
        ob_ref[:, cols] = _softmax_pv([(s_c, cbv_ref[:, h, :].astype(BF16)),
                                       (s_n, vb_ref[:, cols].astype(BF16))]).astype(BF16)


def _sample_call(layer, qa, ka_all, va_all, cache_a_kt, cache_a_v, qb, kb, vb, cache_b_k, cache_b_v,
                 ba_c, ba_n, bb_c, bb_n, lam_vecs, g, *, batch, t_new, lam_init):
    past = cache_a_v.shape[2]
    band_past = cache_b_k.shape[2]
    row = lambda b: (b, 0)
    slab = lambda b: (layer, b, 0)
    cache = lambda b: (layer, b, 0, 0, 0)
    cache_t = lambda b: (layer, b, 0, 0, 0, 0)
    const3 = lambda b: (0, 0, 0)
    const2 = lambda b: (0, 0)
    return pl.pallas_call(
        functools.partial(_sample_kernel, t_new=t_new, lam_init=lam_init),
        out_shape=(jax.ShapeDtypeStruct((batch * t_new, MIX_W), BF16),
                   jax.ShapeDtypeStruct((batch * t_new, MIX_W), BF16)),
        grid=(batch,),
        in_specs=[
            pl.BlockSpec((t_new, MIX_W), row),
            pl.BlockSpec((None, t_new, MIX_W), slab),
            pl.BlockSpec((None, t_new, MIX_W), slab),
            pl.BlockSpec((None, None, N_HEADS, 2, DK_A, past), cache_t),
            pl.BlockSpec((None, None, past, N_HEADS, HEAD_W), cache),
            pl.BlockSpec((t_new, MIX_W), row),
            pl.BlockSpec((t_new, MIX_W), row),
            pl.BlockSpec((t_new, MIX_W), row),
            pl.BlockSpec((None, None, band_past, N_HEADS, HEAD_W), cache),
            pl.BlockSpec((None, None, band_past, N_HEADS, HEAD_W), cache),
            pl.BlockSpec(ba_c.shape, const3),
            pl.BlockSpec(ba_n.shape, const3),
            pl.BlockSpec(bb_c.shape, const3),
            pl.BlockSpec(bb_n.shape, const3),
            pl.BlockSpec((4, DK_A), const2),
            pl.BlockSpec((1, HEAD_W), const2),
        ],
        out_specs=(pl.BlockSpec((t_new, MIX_W), row), pl.BlockSpec((t_new, MIX_W), row)),
        compiler_params=_cparams(("arbitrary",)),
        name="sample_mix",
    )(qa, ka_all, va_all, cache_a_kt, cache_a_v, qb, kb, vb, cache_b_k, cache_b_v,
      ba_c, ba_n, bb_c, bb_n, lam_vecs, g)


def _mixout_kernel(oa_ref, ob_ref, sga_ref, sgb_ref, x_ref, gm_ref, woa_ref, wob_ref, wout_ref,
                   lg_ref, lb_ref, o_ref, *, gpt, alpha):
    a = jnp.dot(oa_ref[...], woa_ref[...], preferred_element_type=F32)
    b = jnp.dot(ob_ref[...], wob_ref[...], preferred_element_type=F32)
    merged = sga_ref[...].astype(F32) * a + sgb_ref[...].astype(F32) * b
    y = jnp.dot(merged.astype(BF16), wout_ref[...], preferred_element_type=F32)
    z = alpha * x_ref[...] + _gate(y, gm_ref[...], gpt)
    o_ref[...] = _ln(z) * lg_ref[...] + lb_ref[...]


def _mixout_call(oa, ob, sga, sgb, x, mod_l, w_oa, w_ob, w_out, ln_g, ln_b,
                 *, rows_per_group, group0, bm, alpha):
    m, d = x.shape
    gpt = max(1, bm // rows_per_group)
    row = lambda i: (i, 0)
    const = lambda i: (0, 0)
    resident = pl.Buffered(1)
    return pl.pallas_call(
        functools.partial(_mixout_kernel, gpt=gpt, alpha=alpha),
        out_shape=jax.ShapeDtypeStruct((m, d), F32),
        grid=(m // bm,),
        in_specs=[
            pl.BlockSpec((bm, MIX_W), row),
            pl.BlockSpec((bm, MIX_W), row),
            pl.BlockSpec((bm, d), row),
            pl.BlockSpec((bm, d), row),
            pl.BlockSpec((bm, d), row),
            _mod_spec(gpt, rows_per_group, bm, group0, 2, d),
            pl.BlockSpec((MIX_W, d), const, pipeline_mode=resident),
            pl.BlockSpec((MIX_W, d), const, pipeline_mode=resident),
            pl.BlockSpec((d, d), const, pipeline_mode=resident),
            pl.BlockSpec((1, d), const),
            pl.BlockSpec((1, d), const),
        ],
        out_specs=pl.BlockSpec((bm, d), row),
        compiler_params=_cparams(("arbitrary",)),
        name="mixout",
    )(oa, ob, sga, sgb, x, mod_l, w_oa, w_ob, w_out, ln_g, ln_b)


def _ffn_kernel(x_ref, sc_ref, sh_ref, gf_ref, w1_ref, w3_ref, w2_ref, lg_ref, lb_ref, o_ref,
                h_scr, acc_scr, *, gpt, alpha):
    kf = pl.program_id(1)

    @pl.when(kf == 0)
    def _():
        h = _modulate(_ln(x_ref[...]), sc_ref[...], sh_ref[...], gpt)
        h_scr[...] = h.astype(BF16)
        acc_scr[...] = jnp.zeros(acc_scr.shape, F32)

    h = h_scr[...]
    a = jnp.dot(h, w1_ref[...], preferred_element_type=F32)
    b = jnp.dot(h, w3_ref[...], preferred_element_type=F32)
    u = (a * _sigmoid(a)) * b
    acc_scr[...] += jnp.dot(u.astype(BF16), w2_ref[...], preferred_element_type=F32)

    @pl.when(kf == pl.num_programs(1) - 1)
    def _():
        z = alpha * x_ref[...] + _gate(acc_scr[...], gf_ref[...], gpt)
        o_ref[...] = _ln(z) * lg_ref[...] + lb_ref[...]


def _ffn_call(x, mod_l, w1, w3, w2, ln_g, ln_b, *, rows_per_group, group0, bm, tf, alpha):
    m, d = x.shape
    dff = w1.shape[1]
    gpt = max(1, bm // rows_per_group)
    return pl.pallas_call(
        functools.partial(_ffn_kernel, gpt=gpt, alpha=alpha),
        out_shape=jax.ShapeDtypeStruct((m, d), F32),
        grid=(m // bm, dff // tf),
        in_specs=[
            pl.BlockSpec((bm, d), lambda i, k: (i, 0)),
            _mod_spec(gpt, rows_per_group, bm, group0, 4, d),
            _mod_spec(gpt, rows_per_group, bm, group0, 3, d),
            _mod_spec(gpt, rows_per_group, bm, group0, 5, d),
            pl.BlockSpec((d, tf), lambda i, k: (0, k)),
            pl.BlockSpec((d, tf), lambda i, k: (0, k)),
            pl.BlockSpec((tf, d), lambda i, k: (k, 0)),
            pl.BlockSpec((1, d), lambda i, k: (0, 0)),
            pl.BlockSpec((1, d), lambda i, k: (0, 0)),
        ],
        out_specs=pl.BlockSpec((bm, d), lambda i, k: (i, 0)),
        scratch_shapes=[pltpu.VMEM((bm, d), BF16), pltpu.VMEM((bm, d), F32)],
        compiler_params=_cparams(("arbitrary", "arbitrary")),
        name="ffn",
    )(x, mod_l, mod_l, mod_l, w1, w3, w2, ln_g, ln_b)


def _ff_tile(dff):
    for tf in (512, 256, 128):
        if dff % tf == 0:
            return tf
    raise ValueError(f"unsupported FFN width {dff}")


def kernel(x_prompt, x_sample, cache_a_k, cache_a_v, cache_b_k, cache_b_v, c_prompt, c_sample,
           w_mod, b_mod, w_in, lambda_q1, lambda_k1, lambda_q2, lambda_k2, subln_g, t5_bias,
           rel_bias, w_oa, w_ob, w_out, ln1_g, ln1_b, w1, w3, w2, ln2_g, ln2_b):
    depth = w_mod.shape[0]
    batch, seq, d = x_prompt.shape
    dec_batch, t_new, _ = x_sample.shape
    past = cache_a_k.shape[2]
    band_past = cache_b_k.shape[2]
    assert past % CHUNK == 0 and t_new <= CHUNK and band_past == BAND_REACH and past >= BAND_REACH
    assert seq % TQ_A == 0 and seq % (TQ_B * UNITS_B) == 0 and seq >= BAND_REACH and d % 128 == 0
    alpha = (2 * depth) ** 0.25
    prompt_band = min(BAND_REACH, seq)

    bm_p = min(512, seq)
    bm_s = min(512, dec_batch * t_new)
    tn = math.gcd(MIX_W, d)
    tf = _ff_tile(w1.shape[2])

    c_all = jnp.concatenate([c_sample, c_prompt], axis=0)
    mod = _mod_call(c_all, w_mod, b_mod).reshape(depth, dec_batch + batch, 1, 6 * d)

    t5_tab = _t5_table(t5_bias)
    ta_tiles = _attn_a_bias_tiles(t5_tab, TQ_A)
    q_pos_s = past + np.arange(t_new)
    rel_a = np.arange(past + t_new)[None, :] - q_pos_s[:, None]
    ba_s = _lookup(t5_tab, _t5_bucket(rel_a))
    kb_pos_s = past - band_past + np.arange(band_past + t_new)
    rel_b = np.clip(kb_pos_s[None, :] - q_pos_s[:, None], -REL_CLIP, REL_CLIP) + REL_CLIP

    cakt = jnp.transpose(cache_a_k, (0, 1, 3, 4, 5, 2))

    xp = x_prompt.reshape(batch * seq, d)
    xs = x_sample.reshape(dec_batch * t_new, d)
    kv_p, kv_s = (), ()
    bk_p, bv_p, bk_s, bv_s = [], [], [], []
    for l in range(depth):
        lam_init = 0.8 - 0.6 * math.exp(-0.3 * l)
        lam_vecs = jnp.stack([lambda_q1[l], lambda_k1[l], lambda_q2[l], lambda_k2[l]]).astype(F32)
        g = subln_g[l].reshape(1, HEAD_W).astype(F32)
        mod_l = mod[l:l + 1]
        w_in_l = w_in[l].astype(BF16)
        w_oa_l, w_ob_l, w_out_l = w_oa[l].astype(BF16), w_ob[l].astype(BF16), w_out[l].astype(BF16)
        w1_l, w3_l, w2_l = w1[l].astype(BF16), w3[l].astype(BF16), w2[l].astype(BF16)
        ln1 = (ln1_g[l].reshape(1, d), ln1_b[l].reshape(1, d))
        ln2 = (ln2_g[l].reshape(1, d), ln2_b[l].reshape(1, d))
        rel_tab = rel_bias[l] * LOG2E
        bb_tile = _attn_b_bias_tile(rel_tab, TQ_B, WIN_B)
        bb_s = _lookup(rel_tab, rel_b)

        grp = dict(rows_per_group=seq, group0=dec_batch, bm=bm_p)
        qa, ka_all, va_all, qb, kb, vb, sga, sgb = _inproj_call(
            xp, mod_l, w_in_l, kv_p, layer=l, depth=depth, tn=tn, **grp)
        kv_p = (ka_all, va_all)
        oa = _attn_a_call(qa, ka_all, va_all, ta_tiles, lam_vecs, g,
                          layer=l, batch=batch, seq=seq, lam_init=lam_init)
        ob = _attn_b_call(qb, kb, vb, bb_tile, batch=batch, seq=seq)
        xp = _mixout_call(oa, ob, sga, sgb, xp, mod_l, w_oa_l, w_ob_l, w_out_l, *ln1, alpha=alpha, **grp)
        xp = _ffn_call(xp, mod_l, w1_l, w3_l, w2_l, *ln2, tf=tf, alpha=alpha, **grp)
        bk_p.append(kb.reshape(batch, seq, N_HEADS, HEAD_W)[:, seq - prompt_band:])
        bv_p.append(vb.reshape(batch, seq, N_HEADS, HEAD_W)[:, seq - prompt_band:])

        grp = dict(rows_per_group=t_new, group0=0, bm=bm_s)
        qa, ka_all, va_all, qb, kb, vb, sga, sgb = _inproj_call(
            xs, mod_l, w_in_l, kv_s, layer=l, depth=depth, tn=tn, **grp)
        kv_s = (ka_all, va_all)
        oa, ob = _sample_call(l, qa, ka_all, va_all, cakt, cache_a_v, qb, kb, vb, cache_b_k, cache_b_v,
                              ba_s[:, :, :past], ba_s[:, :, past:], bb_s[:, :, :band_past],
                              bb_s[:, :, band_past:], lam_vecs, g,
                              batch=dec_batch, t_new=t_new, lam_init=lam_init)
        xs = _mixout_call(oa, ob, sga, sgb, xs, mod_l, w_oa_l, w_ob_l, w_out_l, *ln1, alpha=alpha, **grp)
        xs = _ffn_call(xs, mod_l, w1_l, w3_l, w2_l, *ln2, tf=tf, alpha=alpha, **grp)
        bk_s.append(kb.reshape(dec_batch, t_new, N_HEADS, HEAD_W))
        bv_s.append(vb.reshape(dec_batch, t_new, N_HEADS, HEAD_W))

    return (xp.reshape(batch, seq, d), xs.reshape(dec_batch, t_new, d),
            kv_p[0].reshape(depth, batch, seq, N_HEADS, 2, DK_A),
            kv_p[1].reshape(depth, batch, seq, N_HEADS, HEAD_W),
            jnp.stack(bk_p), jnp.stack(bv_p),
            kv_s[0].reshape(depth, dec_batch, t_new, N_HEADS, 2, DK_A),
            kv_s[1].reshape(depth, dec_batch, t_new, N_HEADS, HEAD_W),
            jnp.stack(bk_s), jnp.stack(bv_s))
```

```python
import functools
import math

import jax
import jax.numpy as jnp
import numpy as np
from jax import lax
from jax.experimental import pallas as pl
from jax.experimental.pallas import tpu as pltpu

F32 = jnp.float32
BF16 = jnp.bfloat16

CHUNK = 64
N_HEADS = 8
HEAD_W = 128
DK_A = 64
BAND_PREV_CHUNKS = 8
BAND_REACH = BAND_PREV_CHUNKS * CHUNK
REL_CLIP = 128
T5_BUCKETS = 32
T5_MAX_DIST = 128
NEG = -1e30
LN_EPS = 1e-5
MIX_W = N_HEADS * HEAD_W
LOG2E = math.log2(math.e)
QA_SCALE = DK_A ** -0.5 * LOG2E
QB_SCALE = HEAD_W ** -0.5 * LOG2E

V7X_VMEM_BYTES = 64 * 1024 * 1024
VMEM_LIMIT = V7X_VMEM_BYTES - 8 * 1024 * 1024

TQ_A = 256
HEADS_PER_STEP_A = 4
TQ_B = 128
WIN_B = BAND_REACH + TQ_B
UNITS_B = 4


def _cparams(sem):
    return pltpu.CompilerParams(dimension_semantics=sem, vmem_limit_bytes=VMEM_LIMIT)


def _sigmoid(x):
    return 1.0 / (1.0 + jnp.exp(-x))


def _ln(x):
    mu = jnp.mean(x, axis=-1, keepdims=True)
    xc = x - mu
    var = jnp.mean(xc * xc, axis=-1, keepdims=True)
    return xc * lax.rsqrt(var + LN_EPS)


def _modulate(y, scale, shift, gpt):
    if gpt == 1:
        return y * (1.0 + scale[0]) + shift[0]
    bm, d = y.shape
    y3 = y.reshape(gpt, bm // gpt, d)
    return (y3 * (1.0 + scale) + shift).reshape(bm, d)


def _gate(y, gate, gpt):
    if gpt == 1:
        return y * gate[0]
    bm, d = y.shape
    return (y.reshape(gpt, bm // gpt, d) * gate).reshape(bm, d)


def _nt_dot(a, b):
    return lax.dot_general(a, b, (((1,), (1,)), ((), ())), preferred_element_type=F32)


def _mod_kernel(c_ref, w_ref, b_ref, o_ref):
    c = c_ref[...]
    s = (c * _sigmoid(c)).astype(BF16)
    o_ref[...] = jnp.dot(s, w_ref[...].astype(BF16), preferred_element_type=F32) + b_ref[...]


def _mod_call(c_all, w_mod, b_mod):
    depth, d, n = w_mod.shape
    g = c_all.shape[0]
    tn = math.gcd(n, 1024)
    return pl.pallas_call(
        _mod_kernel,
        out_shape=jax.ShapeDtypeStruct((depth, g, n), F32),
        grid=(depth, n // tn),
        in_specs=[
            pl.BlockSpec((g, d), lambda l, j: (0, 0)),
            pl.BlockSpec((None, d, tn), lambda l, j: (l, 0, j)),
            pl.BlockSpec((None, 1, tn), lambda l, j: (l, 0, j)),
        ],
        out_specs=pl.BlockSpec((None, g, tn), lambda l, j: (l, 0, j)),
        compiler_params=_cparams(("arbitrary", "arbitrary")),
        name="mod",
    )(c_all, w_mod, b_mod.reshape(depth, 1, n))


def _inproj_segments(d, tn):
    widths = (MIX_W,) * 6 + (d, d)
    segs, lo = [], 0
    for w in widths:
        assert w % tn == 0
        segs.append((lo, w // tn))
        lo += w // tn
    return tuple(segs)


def _inproj_write(ref, kind, acc):
    if kind == "f32":
        ref[...] = acc
    elif kind == "bf16":
        ref[...] = acc.astype(BF16)
    elif kind == "qa":
        ref[...] = (acc * QA_SCALE).astype(BF16)
    elif kind == "qb":
        ref[...] = (acc * QB_SCALE).astype(BF16)
    elif kind == "gate":
        ref[...] = _sigmoid(acc).astype(BF16)
    elif kind == "key_major":
        for h in range(ref.shape[0]):
            ref[h] = acc[:, h * HEAD_W:(h + 1) * HEAD_W].T
    else:
        raise ValueError(kind)


def _inproj_kernel(*refs, gpt, segs, plan, n_in, n_out):
    x_ref, sc_ref, sh_ref, w_ref = refs[:4]
    out_refs = refs[n_in:n_in + n_out]
    h_scr = refs[n_in + n_out]
    j = pl.program_id(1)

    @pl.when(j == 0)
    def _():
        h = _modulate(_ln(x_ref[...]), sc_ref[...], sh_ref[...], gpt)
        h_scr[...] = h.astype(BF16)

    for (lo, cnt), writers in zip(segs, plan):
        @pl.when((j >= lo) & (j < lo + cnt))
        def _(writers=writers):
            acc = jnp.dot(h_scr[...], w_ref[...], preferred_element_type=F32)
            for idx, kind in writers:
                _inproj_write(out_refs[idx], kind, acc)


def _mod_spec(gpt, rows_per_group, bm, group0, which, d):
    if gpt == 1:
        tiles_per_group = rows_per_group // bm
        return pl.BlockSpec((None, 1, 1, d), lambda i, *_: (0, group0 + i // tiles_per_group, 0, which))
    return pl.BlockSpec((None, gpt, 1, d), lambda i, *_: (0, group0 // gpt + i, 0, which))


def _inproj_call(x, mod_l, w_in, kv_carry, *, layer, depth, rows_per_group, group0, bm, tn, key_major):
    m, d = x.shape
    n = w_in.shape[1]
    gpt = max(1, bm // rows_per_group)
    segs = _inproj_segments(d, tn)
    assert n == tn * (segs[-1][0] + segs[-1][1])
    n_carry = len(kv_carry)

    def col(seg):
        lo, cnt = seg
        return lambda j: jnp.clip(j - lo, 0, cnt - 1)

    def rows(seg, width, dtype):
        c = col(seg)
        return (jax.ShapeDtypeStruct((m, width), dtype), pl.BlockSpec((bm, tn), lambda i, j: (i, c(j))))

    def slab(seg):
        c = col(seg)
        return (jax.ShapeDtypeStruct((depth, m, MIX_W), F32),
                pl.BlockSpec((None, bm, tn), lambda i, j: (layer, i, c(j))))

    def key_major_slab(seg):
        c = col(seg)
        tiles_per_seq = rows_per_group // bm
        return (jax.ShapeDtypeStruct((depth, m // rows_per_group, N_HEADS, HEAD_W, rows_per_group), F32),
                pl.BlockSpec((None, None, tn // HEAD_W, HEAD_W, bm),
                             lambda i, j: (layer, i // tiles_per_seq, c(j), 0, i % tiles_per_seq)))

    outs = {"qa": rows(segs[0], MIX_W, BF16)}
    if key_major:
        assert rows_per_group % bm == 0
        outs["ka"] = rows(segs[1], MIX_W, BF16)
        outs["ka_all"] = key_major_slab(segs[1])
        ka_writers = (("ka", "bf16"), ("ka_all", "key_major"))
    else:
        outs["ka_all"] = slab(segs[1])
        ka_writers = (("ka_all", "f32"),)
    outs["va_all"] = slab(segs[2])
    outs["qb"] = rows(segs[3], MIX_W, BF16)
    outs["kb"] = rows(segs[4], MIX_W, F32)
    outs["vb"] = rows(segs[5], MIX_W, F32)
    outs["sga"] = rows(segs[6], d, BF16)
    outs["sgb"] = rows(segs[7], d, BF16)
    names = list(outs)
    plan = tuple(tuple((names.index(nm), kind) for nm, kind in writers) for writers in (
        (("qa", "qa"),), ka_writers, (("va_all", "f32"),), (("qb", "qb"),),
        (("kb", "f32"),), (("vb", "f32"),), (("sga", "gate"),), (("sgb", "gate"),)))
    n_in = 4 + n_carry
    aliases = {4 + c: names.index(nm) for c, nm in enumerate(("ka_all", "va_all")[:n_carry])}
    res = pl.pallas_call(
        functools.partial(_inproj_kernel, gpt=gpt, segs=segs, plan=plan, n_in=n_in, n_out=len(names)),
        out_shape=tuple(outs[nm][0] for nm in names),
        grid=(m // bm, n // tn),
        in_specs=[
            pl.BlockSpec((bm, d), lambda i, j: (i, 0)),
            _mod_spec(gpt, rows_per_group, bm, group0, 1, d),
            _mod_spec(gpt, rows_per_group, bm, group0, 0, d),
            pl.BlockSpec((d, tn), lambda i, j: (0, j)),
        ] + [pl.BlockSpec(memory_space=pl.ANY)] * n_carry,
        out_specs=tuple(outs[nm][1] for nm in names),
        scratch_shapes=[pltpu.VMEM((bm, d), BF16)],
        input_output_aliases=aliases,
        compiler_params=_cparams(("arbitrary", "arbitrary")),
        name="inproj",
    )(x, mod_l, mod_l, w_in, *kv_carry)
    return dict(zip(names, res))


def _t5_bucket(rel):
    nb = T5_BUCKETS // 2
    max_exact = nb // 2
    rel = np.asarray(rel, np.int32)
    bucket = (rel > 0).astype(np.int32) * nb
    n = np.abs(rel)
    nf = np.maximum(n, 1).astype(np.float32)
    ratio = np.log(nf / np.float32(max_exact)) / np.float32(math.log(T5_MAX_DIST / max_exact))
    large = max_exact + (ratio * np.float32(nb - max_exact)).astype(np.int32)
    large = np.minimum(large, nb - 1)
    return bucket + np.where(n < max_exact, n, large)


def _lookup(table, idx):
    n = table.shape[0]
    flat = jnp.asarray(np.asarray(idx, np.int32).reshape(-1))
    onehot = (flat[None, :] == jnp.arange(n, dtype=jnp.int32)[:, None]).astype(F32)
    out = lax.dot_general(table.astype(F32), onehot, (((0,), (0,)), ((), ())),
                          precision=lax.Precision.HIGHEST, preferred_element_type=F32)
    return out.reshape((table.shape[1],) + tuple(np.shape(idx)))


def _t5_table(t5_bias):
    far = t5_bias[int(_t5_bucket(-T5_MAX_DIST))]
    return (t5_bias - far[None, :]) * LOG2E


def _attn_a_bias_tiles(t5_tab, t):
    assert t >= T5_MAX_DIST
    k = np.arange(t)[:, None]
    q = np.arange(t)[None, :]
    visible = (k // CHUNK) <= (q // CHUNK)
    idx = np.stack([_t5_bucket(k - q), _t5_bucket(k - t - q)])
    vis = np.stack([visible, np.ones_like(visible)])
    return jnp.where(vis[None], _lookup(t5_tab, idx), NEG)


def _attn_b_bias_tile(rel_tab, tq, win):
    kw = np.arange(win)[:, None] - BAND_REACH
    q = np.arange(tq)[None, :]
    rel = np.clip(kw - q, -REL_CLIP, REL_CLIP) + REL_CLIP
    dchunk = q // CHUNK - np.floor_divide(kw, CHUNK)
    visible = (dchunk >= 0) & (dchunk <= BAND_PREV_CHUNKS)
    return jnp.where(visible[None], _lookup(rel_tab, rel), NEG)


def _diff_lambda(lam_ref, lam_init):
    lv = lam_ref[...]
    s1 = jnp.sum(lv[0:1] * lv[1:2], axis=-1, keepdims=True)
    s2 = jnp.sum(lv[2:3] * lv[3:4], axis=-1, keepdims=True)
    return jnp.exp(s1) - jnp.exp(s2) + lam_init


def _stack_diff_queries(q):
    lane = lax.broadcasted_iota(jnp.int32, q.shape, 1)
    zero = jnp.zeros_like(q)
    return jnp.concatenate([jnp.where(lane < DK_A, q, zero), jnp.where(lane >= DK_A, q, zero)], axis=0)


def _attn_a_kernel(q_ref, k_ref, v_ref, bias_ref, lam_ref, g_ref, o_ref,
                   vt_scr, s_scr, acc_scr, m_scr, l_scr, *, t, nh, lam_init):
    qi = pl.program_id(2)
    nkb = vt_scr.shape[1]
    heads = range(nh)

    @pl.when(qi == 0)
    def _():
        for hh in heads:
            cols = slice(hh * HEAD_W, (hh + 1) * HEAD_W)
            for kb in range(nkb):
                vt_scr[hh, kb] = v_ref[kb * t:(kb + 1) * t, cols].T.astype(BF16)

    q2 = [_stack_diff_queries(q_ref[:, hh * HEAD_W:(hh + 1) * HEAD_W]) for hh in heads]
    acc_scr[...] = jnp.zeros(acc_scr.shape, F32)

    def scores(hh, kb):
        start = pl.multiple_of(kb * t, t)
        s_scr[hh] = _nt_dot(k_ref[pl.ds(start, t), hh * HEAD_W:(hh + 1) * HEAD_W], q2[hh])

    def step(kb, ml, bias_idx, prefetch):
        out = []
        for hh in heads:
            m, l = ml[hh]
            s = s_scr[hh]
            if bias_idx is not None:
                bias = bias_ref[hh, bias_idx]
                s = s + jnp.concatenate([bias, bias], axis=1)
            m_new = jnp.maximum(m, jnp.max(s, axis=0, keepdims=True))
            alpha = jnp.exp2(m - m_new)
            e = jnp.exp2(s - m_new)
            l_new = alpha * l + jnp.sum(e, axis=0, keepdims=True)
            pv = jnp.dot(vt_scr[hh, kb], e.astype(BF16), preferred_element_type=F32)
            if prefetch:
                scores(hh, kb + 1)
            acc_scr[hh] = alpha * acc_scr[hh] + pv
            out.append((m_new, l_new))
        return tuple(out)

    def load_ml():
        return tuple((m_scr[hh], l_scr[hh]) for hh in heads)

    def store_ml(ml):
        for hh in heads:
            m_scr[hh] = ml[hh][0]
            l_scr[hh] = ml[hh][1]

    for hh in heads:
        scores(hh, 0)
    ml0 = tuple((jnp.full((1, 2 * t), -jnp.inf, F32), jnp.zeros((1, 2 * t), F32)) for _ in heads)
    n_far = jnp.maximum(qi - 1, 0)
    store_ml(lax.fori_loop(
        0, lax.shift_right_logical(n_far, 1),
        lambda p, ml: step(2 * p + 1, step(2 * p, ml, None, True), None, True), ml0))

    @pl.when(lax.rem(n_far, 2) == 1)
    def _():
        store_ml(step(n_far - 1, load_ml(), None, True))

    @pl.when(qi >= 1)
    def _():
        store_ml(step(qi - 1, load_ml(), 1, True))

    ml = step(qi, load_ml(), 0, False)

    lam = _diff_lambda(lam_ref, lam_init)
    for hh in heads:
        o2 = acc_scr[hh] / ml[hh][1]
        o = o2[:, :t] - lam * o2[:, t:]
        o = o * lax.rsqrt(jnp.mean(o * o, axis=0, keepdims=True) + LN_EPS)
        o_ref[:, hh * HEAD_W:(hh + 1) * HEAD_W] = (o.T * g_ref[...] * (1.0 - lam_init)).astype(BF16)


def _attn_a_call(qa, ka, va_all, bias_tiles, lam_vecs, g, *, layer, batch, seq, lam_init):
    t, nh = TQ_A, HEADS_PER_STEP_A
    nq = seq // t
    w = nh * HEAD_W
    return pl.pallas_call(
        functools.partial(_attn_a_kernel, t=t, nh=nh, lam_init=lam_init),
        out_shape=jax.ShapeDtypeStruct((batch * seq, MIX_W), BF16),
        grid=(batch, N_HEADS // nh, nq),
        in_specs=[
            pl.BlockSpec((t, w), lambda b, h, qi: (b * nq + qi, h)),
            pl.BlockSpec((seq, w), lambda b, h, qi: (b, h)),
            pl.BlockSpec((None, seq, w), lambda b, h, qi: (layer, b, h)),
            pl.BlockSpec((nh, 2, t, t), lambda b, h, qi: (h, 0, 0, 0)),
            pl.BlockSpec((4, DK_A), lambda b, h, qi: (0, 0)),
            pl.BlockSpec((1, HEAD_W), lambda b, h, qi: (0, 0)),
        ],
        out_specs=pl.BlockSpec((t, w), lambda b, h, qi: (b * nq + qi, h)),
        scratch_shapes=[
            pltpu.VMEM((nh, seq // t, HEAD_W, t), BF16),
            pltpu.VMEM((nh, t, 2 * t), F32),
            pltpu.VMEM((nh, HEAD_W, 2 * t), F32),
            pltpu.VMEM((nh, 1, 2 * t), F32),
            pltpu.VMEM((nh, 1, 2 * t), F32),
        ],
        compiler_params=_cparams(("arbitrary", "arbitrary", "arbitrary")),
        name="attn_a",
    )(qa, ka, va_all, bias_tiles, lam_vecs, g)


def _attn_b_kernel(q_ref, k_ref, v_ref, bias_ref, o_ref, k_scr, vt_scr, *, tq, win, seq, units):
    qi = pl.program_id(2)
    pad_blocks = BAND_REACH // tq
    win_blocks = win // tq

    @pl.when(qi == 0)
    def _():
        k_scr[0:BAND_REACH, :] = jnp.zeros((BAND_REACH, HEAD_W), BF16)
        k_scr[BAND_REACH:BAND_REACH + seq, :] = k_ref[...].astype(BF16)
        for blk in range(pad_blocks):
            vt_scr[blk] = jnp.zeros((HEAD_W, tq), BF16)
        for blk in range(seq // tq):
            vt_scr[pad_blocks + blk] = v_ref[blk * tq:(blk + 1) * tq, :].T.astype(BF16)

    bias = bias_ref[...]
    row = lax.broadcasted_iota(jnp.int32, (win, tq), 0)
    raw = []
    for u in range(units):
        start = pl.multiple_of((qi * units + u) * tq, tq)
        raw.append(_nt_dot(k_scr[pl.ds(start, win), :], q_ref[u * tq:(u + 1) * tq, :]))
    for u in range(units):
        blk0 = qi * units + u
        s = jnp.where(row + blk0 * tq >= BAND_REACH, raw[u] + bias, NEG)
        m = jnp.max(s, axis=0, keepdims=True)
        e = jnp.exp2(s - m)
        l = jnp.sum(e, axis=0, keepdims=True)
        e = e.astype(BF16)
        acc = None
        for w in range(win_blocks):
            pv = jnp.dot(vt_scr[blk0 + w], e[w * tq:(w + 1) * tq, :], preferred_element_type=F32)
            acc = pv if acc is None else acc + pv
        o_ref[u * tq:(u + 1) * tq, :] = (acc / l).T.astype(BF16)


def _attn_b_call(qb, kb, vb, bias_tile, *, batch, seq):
    tq, win, units = TQ_B, WIN_B, UNITS_B
    rows = tq * units
    nq = seq // rows
    return pl.pallas_call(
        functools.partial(_attn_b_kernel, tq=tq, win=win, seq=seq, units=units),
        out_shape=jax.ShapeDtypeStruct((batch * seq, MIX_W), BF16),
        grid=(batch, N_HEADS, nq),
        in_specs=[
            pl.BlockSpec((rows, HEAD_W), lambda b, h, qi: (b * nq + qi, h)),
            pl.BlockSpec((seq, HEAD_W), lambda b, h, qi: (b, h)),
            pl.BlockSpec((seq, HEAD_W), lambda b, h, qi: (b, h)),
            pl.BlockSpec((None, win, tq), lambda b, h, qi: (h, 0, 0)),
        ],
        out_specs=pl.BlockSpec((rows, HEAD_W), lambda b, h, qi: (b * nq + qi, h)),
        scratch_shapes=[
            pltpu.VMEM((BAND_REACH + seq, HEAD_W), BF16),
            pltpu.VMEM(((BAND_REACH + seq) // tq, HEAD_W, tq), BF16),
        ],
        compiler_params=_cparams(("arbitrary", "arbitrary", "arbitrary")),
        name="attn_b",
    )(qb, kb, vb, bias_tile)


def _softmax_pv(parts):
    m = None
    for s, _ in parts:
        pm = jnp.max(s, axis=-1, keepdims=True)
        m = pm if m is None else jnp.maximum(m, pm)
    l = None
    o = None
    for s, v in parts:
        e = jnp.exp2(s - m)
        pl_ = jnp.sum(e, axis=-1, keepdims=True)
        po = jnp.dot(e.astype(BF16), v, preferred_element_type=F32)
        l = pl_ if l is None else l + pl_
        o = po if o is None else o + po
    return o / l


def _sample_kernel(qa_ref, ka_ref, va_ref, cakt_ref, cav_ref, qb_ref, kb_ref, vb_ref, cbk_ref, cbv_ref,
                   ba_c_ref, ba_n_ref, bb_c_ref, bb_n_ref, lam_ref, g_ref, oa_ref, ob_ref,
                   *, t_new, lam_init):
    lam = _diff_lambda(lam_ref, lam_init)
    g = g_ref[...]
    past = cakt_ref.shape[-1]
    for h in range(N_HEADS):
        cols = slice(h * HEAD_W, (h + 1) * HEAD_W)
        q2 = _stack_diff_queries(qa_ref[:, cols])
        kct = cakt_ref[h].reshape(2 * DK_A, past).astype(BF16)
        s_c = jnp.dot(q2, kct, preferred_element_type=F32)
        s_n = _nt_dot(q2, ka_ref[:, cols].astype(BF16))
        s_c = (s_c.reshape(2, t_new, -1) + ba_c_ref[h][None]).reshape(2 * t_new, -1)
        s_n = (s_n.reshape(2, t_new, -1) + ba_n_ref[h][None]).reshape(2 * t_new, -1)
        o2 = _softmax_pv([(s_c, cav_ref[:, h, :].astype(BF16)), (s_n, va_ref[:, cols].astype(BF16))])
        o = o2[:t_new] - lam * o2[t_new:]
        o = o * lax.rsqrt(jnp.mean(o * o, axis=-1, keepdims=True) + LN_EPS) * g
        oa_ref[:, cols] = (o * (1.0 - lam_init)).astype(BF16)
        qh = qb_ref[:, cols]
        s_c = _nt_dot(qh, cbk_ref[:, h, :].astype(BF16)) + bb_c_ref[h]
        s_n = _nt_dot(qh, kb_ref[:, cols].astype(BF16)) + bb_n_ref[h]
        ob_ref[:, cols] = _softmax_pv([(s_c, cbv_ref[:, h, :].astype(BF16)),
                                       (s_n, vb_ref[:, cols].astype(BF16))]).astype(BF16)


def _sample_call(layer, qa, ka_all, va_all, cache_a_kt, cache_a_v, qb, kb, vb, cache_b_k, cache_b_v,
                 ba_c, ba_n, bb_c, bb_n, lam_vecs, g, *, batch, t_new, lam_init):
    past = cache_a_v.shape[2]
    band_past = cache_b_k.shape[2]
    row = lambda b: (b, 0)
    slab = lambda b: (layer, b, 0)
    cache = lambda b: (layer, b, 0, 0, 0)
    cache_t = lambda b: (layer, b, 0, 0, 0, 0)
    const3 = lambda b: (0, 0, 0)
    const2 = lambda b: (0, 0)
    return pl.pallas_call(
        functools.partial(_sample_kernel, t_new=t_new, lam_init=lam_init),
        out_shape=(jax.ShapeDtypeStruct((batch * t_new, MIX_W), BF16),
                   jax.ShapeDtypeStruct((batch * t_new, MIX_W), BF16)),
        grid=(batch,),
        in_specs=[
            pl.BlockSpec((t_new, MIX_W), row),
            pl.BlockSpec((None, t_new, MIX_W), slab),
            pl.BlockSpec((None, t_new, MIX_W), slab),
            pl.BlockSpec((None, None, N_HEADS, 2, DK_A, past), cache_t),
            pl.BlockSpec((None, None, past, N_HEADS, HEAD_W), cache),
            pl.BlockSpec((t_new, MIX_W), row),
            pl.BlockSpec((t_new, MIX_W), row),
            pl.BlockSpec((t_new, MIX_W), row),
            pl.BlockSpec((None, None, band_past, N_HEADS, HEAD_W), cache),
            pl.BlockSpec((None, None, band_past, N_HEADS, HEAD_W), cache),
            pl.BlockSpec(ba_c.shape, const3),
            pl.BlockSpec(ba_n.shape, const3),
            pl.BlockSpec(bb_c.shape, const3),
            pl.BlockSpec(bb_n.shape, const3),
            pl.BlockSpec((4, DK_A), const2),
            pl.BlockSpec((1, HEAD_W), const2),
        ],
        out_specs=(pl.BlockSpec((t_new, MIX_W), row), pl.BlockSpec((t_new, MIX_W), row)),
        compiler_params=_cparams(("arbitrary",)),
        name="sample_mix",
    )(qa, ka_all, va_all, cache_a_kt, cache_a_v, qb, kb, vb, cache_b_k, cache_b_v,
      ba_c, ba_n, bb_c, bb_n, lam_vecs, g)


def _mixout_kernel(oa_ref, ob_ref, sga_ref, sgb_ref, x_ref, gm_ref, woa_ref, wob_ref, wout_ref,
                   lg_ref, lb_ref, o_ref, *, gpt, alpha):
    a = jnp.dot(oa_ref[...], woa_ref[...], preferred_element_type=F32)
    b = jnp.dot(ob_ref[...], wob_ref[...], preferred_element_type=F32)
    merged = sga_ref[...].astype(F32) * a + sgb_ref[...].astype(F32) * b
    y = jnp.dot(merged.astype(BF16), wout_ref[...], preferred_element_type=F32)
    z = alpha * x_ref[...] + _gate(y, gm_ref[...], gpt)
    o_ref[...] = _ln(z) * lg_ref[...] + lb_ref[...]


def _mixout_call(oa, ob, sga, sgb, x, mod_l, w_oa, w_ob, w_out, ln_g, ln_b,
                 *, rows_per_group, group0, bm, alpha):
    m, d = x.shape
    gpt = max(1, bm // rows_per_group)
    row = lambda i: (i, 0)
    const = lambda i: (0, 0)
    resident = pl.Buffered(1)
    return pl.pallas_call(
        functools.partial(_mixout_kernel, gpt=gpt, alpha=alpha),
        out_shape=jax.ShapeDtypeStruct((m, d), F32),
        grid=(m // bm,),
        in_specs=[
            pl.BlockSpec((bm, MIX_W), row),
            pl.BlockSpec((bm, MIX_W), row),
            pl.BlockSpec((bm, d), row),
            pl.BlockSpec((bm, d), row),
            pl.BlockSpec((bm, d), row),
            _mod_spec(gpt, rows_per_group, bm, group0, 2, d),
            pl.BlockSpec((MIX_W, d), const, pipeline_mode=resident),
            pl.BlockSpec((MIX_W, d), const, pipeline_mode=resident),
            pl.BlockSpec((d, d), const, pipeline_mode=resident),
            pl.BlockSpec((1, d), const),
            pl.BlockSpec((1, d), const),
        ],
        out_specs=pl.BlockSpec((bm, d), row),
        compiler_params=_cparams(("arbitrary",)),
        name="mixout",
    )(oa, ob, sga, sgb, x, mod_l, w_oa, w_ob, w_out, ln_g, ln_b)


def _ffn_kernel(x_ref, sc_ref, sh_ref, gf_ref, w1_ref, w3_ref, w2_ref, lg_ref, lb_ref, o_ref,
                h_scr, acc_scr, *, gpt, alpha):
    kf = pl.program_id(1)

    @pl.when(kf == 0)
    def _():
        h = _modulate(_ln(x_ref[...]), sc_ref[...], sh_ref[...], gpt)
        h_scr[...] = h.astype(BF16)
        acc_scr[...] = jnp.zeros(acc_scr.shape, F32)

    h = h_scr[...]
    a = jnp.dot(h, w1_ref[...], preferred_element_type=F32)
    b = jnp.dot(h, w3_ref[...], preferred_element_type=F32)
    u = (a * _sigmoid(a)) * b
    acc_scr[...] += jnp.dot(u.astype(BF16), w2_ref[...], preferred_element_type=F32)

    @pl.when(kf == pl.num_programs(1) - 1)
    def _():
        z = alpha * x_ref[...] + _gate(acc_scr[...], gf_ref[...], gpt)
        o_ref[...] = _ln(z) * lg_ref[...] + lb_ref[...]


def _ffn_call(x, mod_l, w1, w3, w2, ln_g, ln_b, *, rows_per_group, group0, bm, tf, alpha):
    m, d = x.shape
    dff = w1.shape[1]
    gpt = max(1, bm // rows_per_group)
    return pl.pallas_call(
        functools.partial(_ffn_kernel, gpt=gpt, alpha=alpha),
        out_shape=jax.ShapeDtypeStruct((m, d), F32),
        grid=(m // bm, dff // tf),
        in_specs=[
            pl.BlockSpec((bm, d), lambda i, k: (i, 0)),
            _mod_spec(gpt, rows_per_group, bm, group0, 4, d),
            _mod_spec(gpt, rows_per_group, bm, group0, 3, d),
            _mod_spec(gpt, rows_per_group, bm, group0, 5, d),
            pl.BlockSpec((d, tf), lambda i, k: (0, k)),
            pl.BlockSpec((d, tf), lambda i, k: (0, k)),
            pl.BlockSpec((tf, d), lambda i, k: (k, 0)),
            pl.BlockSpec((1, d), lambda i, k: (0, 0)),
            pl.BlockSpec((1, d), lambda i, k: (0, 0)),
        ],
        out_specs=pl.BlockSpec((bm, d), lambda i, k: (i, 0)),
        scratch_shapes=[pltpu.VMEM((bm, d), BF16), pltpu.VMEM((bm, d), F32)],
        compiler_params=_cparams(("arbitrary", "arbitrary")),
        name="ffn",
    )(x, mod_l, mod_l, mod_l, w1, w3, w2, ln_g, ln_b)


def _ff_tile(dff):
    for tf in (512, 256, 128):
        if dff % tf == 0:
            return tf
    raise ValueError(f"unsupported FFN width {dff}")


def kernel(x_prompt, x_sample, cache_a_k, cache_a_v, cache_b_k, cache_b_v, c_prompt, c_sample,
           w_mod, b_mod, w_in, lambda_q1, lambda_k1, lambda_q2, lambda_k2, subln_g, t5_bias,
           rel_bias, w_oa, w_ob, w_out, ln1_g, ln1_b, w1, w3, w2, ln2_g, ln2_b):
    depth = w_mod.shape[0]
    batch, seq, d = x_prompt.shape
    dec_batch, t_new, _ = x_sample.shape
    past = cache_a_k.shape[2]
    band_past = cache_b_k.shape[2]
    assert past % CHUNK == 0 and t_new <= CHUNK and band_past == BAND_REACH and past >= BAND_REACH
    assert seq % TQ_A == 0 and seq % (TQ_B * UNITS_B) == 0 and seq >= BAND_REACH and d % 128 == 0
    alpha = (2 * depth) ** 0.25
    prompt_band = min(BAND_REACH, seq)

    bm_p = min(512, seq)
    bm_s = min(512, dec_batch * t_new)
    tn = math.gcd(MIX_W, d)
    tf = _ff_tile(w1.shape[2])

    c_all = jnp.concatenate([c_sample, c_prompt], axis=0)
    mod = _mod_call(c_all, w_mod, b_mod).reshape(depth, dec_batch + batch, 1, 6 * d)

    t5_tab = _t5_table(t5_bias)
    ta_tiles = _attn_a_bias_tiles(t5_tab, TQ_A)
    q_pos_s = past + np.arange(t_new)
    rel_a = np.arange(past + t_new)[None, :] - q_pos_s[:, None]
    ba_s = _lookup(t5_tab, _t5_bucket(rel_a))
    kb_pos_s = past - band_past + np.arange(band_past + t_new)
    rel_b = np.clip(kb_pos_s[None, :] - q_pos_s[:, None], -REL_CLIP, REL_CLIP) + REL_CLIP

    cakt = jnp.transpose(cache_a_k, (0, 1, 3, 4, 5, 2))

    xp = x_prompt.reshape(batch * seq, d)
    xs = x_sample.reshape(dec_batch * t_new, d)
    kv_p, kv_s = (), ()
    bk_p, bv_p, bk_s, bv_s = [], [], [], []
    for l in range(depth):
        lam_init = 0.8 - 0.6 * math.exp(-0.3 * l)
        lam_vecs = jnp.stack([lambda_q1[l], lambda_k1[l], lambda_q2[l], lambda_k2[l]]).astype(F32)
        g = subln_g[l].reshape(1, HEAD_W).astype(F32)
        mod_l = mod[l:l + 1]
        w_in_l = w_in[l].astype(BF16)
        w_oa_l, w_ob_l, w_out_l = w_oa[l].astype(BF16), w_ob[l].astype(BF16), w_out[l].astype(BF16)
        w1_l, w3_l, w2_l = w1[l].astype(BF16), w3[l].astype(BF16), w2[l].astype(BF16)
        ln1 = (ln1_g[l].reshape(1, d), ln1_b[l].reshape(1, d))
        ln2 = (ln2_g[l].reshape(1, d), ln2_b[l].reshape(1, d))
        rel_tab = rel_bias[l] * LOG2E
        bb_tile = _attn_b_bias_tile(rel_tab, TQ_B, WIN_B)
        bb_s = _lookup(rel_tab, rel_b)

        grp = dict(rows_per_group=seq, group0=dec_batch, bm=bm_p)
        p = _inproj_call(xp, mod_l, w_in_l, kv_p, layer=l, depth=depth, tn=tn, key_major=True, **grp)
        kv_p = (p["ka_all"], p["va_all"])
        oa = _attn_a_call(p["qa"], p["ka"], p["va_all"], ta_tiles, lam_vecs, g,
                          layer=l, batch=batch, seq=seq, lam_init=lam_init)
        ob = _attn_b_call(p["qb"], p["kb"], p["vb"], bb_tile, batch=batch, seq=seq)
        xp = _mixout_call(oa, ob, p["sga"], p["sgb"], xp, mod_l, w_oa_l, w_ob_l, w_out_l, *ln1,
                          alpha=alpha, **grp)
        xp = _ffn_call(xp, mod_l, w1_l, w3_l, w2_l, *ln2, tf=tf, alpha=alpha, **grp)
        bk_p.append(p["kb"].reshape(batch, seq, N_HEADS, HEAD_W)[:, seq - prompt_band:])
        bv_p.append(p["vb"].reshape(batch, seq, N_HEADS, HEAD_W)[:, seq - prompt_band:])

        grp = dict(rows_per_group=t_new, group0=0, bm=bm_s)
        p = _inproj_call(xs, mod_l, w_in_l, kv_s, layer=l, depth=depth, tn=tn, key_major=False, **grp)
        kv_s = (p["ka_all"], p["va_all"])
        oa, ob = _sample_call(l, p["qa"], p["ka_all"], p["va_all"], cakt, cache_a_v, p["qb"], p["kb"],
                              p["vb"], cache_b_k, cache_b_v,
                              ba_s[:, :, :past], ba_s[:, :, past:], bb_s[:, :, :band_past],
                              bb_s[:, :, band_past:], lam_vecs, g,
                              batch=dec_batch, t_new=t_new, lam_init=lam_init)
        xs = _mixout_call(oa, ob, p["sga"], p["sgb"], xs, mod_l, w_oa_l, w_ob_l, w_out_l, *ln1,
                          alpha=alpha, **grp)
        xs = _ffn_call(xs, mod_l, w1_l, w3_l, w2_l, *ln2, tf=tf, alpha=alpha, **grp)
        bk_s.append(p["kb"].reshape(dec_batch, t_new, N_HEADS, HEAD_W))
        bv_s.append(p["vb"].reshape(dec_batch, t_new, N_HEADS, HEAD_W))

    a_k_prompt = jnp.transpose(kv_p[0].reshape(depth, batch, N_HEADS, 2, DK_A, seq), (0, 1, 5, 2, 3, 4))
    return (xp.reshape(batch, seq, d), xs.reshape(dec_batch, t_new, d),
            a_k_prompt,
            kv_p[1].reshape(depth, batch, seq, N_HEADS, HEAD_W),
            jnp.stack(bk_p), jnp.stack(bv_p),
            kv_s[0].reshape(depth, dec_batch, t_new, N_HEADS, 2, DK_A),
            kv_s[1].reshape(depth, dec_batch, t_new, N_HEADS, HEAD_W),
            jnp.stack(bk_s), jnp.stack(bv_s))
```

```python
import functools
import math

import jax
import jax.numpy as jnp
import numpy as np
from jax import lax
from jax.experimental import pallas as pl
from jax.experimental.pallas import tpu as pltpu

F32 = jnp.float32
BF16 = jnp.bfloat16

CHUNK = 64
N_HEADS = 8
HEAD_W = 128
DK_A = 64
BAND_PREV_CHUNKS = 8
BAND_REACH = BAND_PREV_CHUNKS * CHUNK
REL_CLIP = 128
T5_BUCKETS = 32
T5_MAX_DIST = 128
NEG = -1e30
LN_EPS = 1e-5
MIX_W = N_HEADS * HEAD_W
LOG2E = math.log2(math.e)
QA_SCALE = DK_A ** -0.5 * LOG2E
QB_SCALE = HEAD_W ** -0.5 * LOG2E

V7X_VMEM_BYTES = 64 * 1024 * 1024
VMEM_LIMIT = V7X_VMEM_BYTES - 8 * 1024 * 1024

TQ_A = 256
HEADS_PER_STEP_A = 4
ONES_ROWS = 16
TQ_B = 128
WIN_B = BAND_REACH + TQ_B
UNITS_B = 4


def _cparams(sem):
    return pltpu.CompilerParams(dimension_semantics=sem, vmem_limit_bytes=VMEM_LIMIT)


def _sigmoid(x):
    return 1.0 / (1.0 + jnp.exp(-x))


def _ln(x):
    mu = jnp.mean(x, axis=-1, keepdims=True)
    xc = x - mu
    var = jnp.mean(xc * xc, axis=-1, keepdims=True)
    return xc * lax.rsqrt(var + LN_EPS)


def _modulate(y, scale, shift, gpt):
    if gpt == 1:
        return y * (1.0 + scale[0]) + shift[0]
    bm, d = y.shape
    y3 = y.reshape(gpt, bm // gpt, d)
    return (y3 * (1.0 + scale) + shift).reshape(bm, d)


def _gate(y, gate, gpt):
    if gpt == 1:
        return y * gate[0]
    bm, d = y.shape
    return (y.reshape(gpt, bm // gpt, d) * gate).reshape(bm, d)


def _nt_dot(a, b):
    return lax.dot_general(a, b, (((1,), (1,)), ((), ())), preferred_element_type=F32)


def _mod_kernel(c_ref, w_ref, b_ref, o_ref):
    c = c_ref[...]
    s = (c * _sigmoid(c)).astype(BF16)
    o_ref[...] = jnp.dot(s, w_ref[...].astype(BF16), preferred_element_type=F32) + b_ref[...]


def _mod_call(c_all, w_mod, b_mod):
    depth, d, n = w_mod.shape
    g = c_all.shape[0]
    tn = math.gcd(n, 1024)
    return pl.pallas_call(
        _mod_kernel,
        out_shape=jax.ShapeDtypeStruct((depth, g, n), F32),
        grid=(depth, n // tn),
        in_specs=[
            pl.BlockSpec((g, d), lambda l, j: (0, 0)),
            pl.BlockSpec((None, d, tn), lambda l, j: (l, 0, j)),
            pl.BlockSpec((None, 1, tn), lambda l, j: (l, 0, j)),
        ],
        out_specs=pl.BlockSpec((None, g, tn), lambda l, j: (l, 0, j)),
        compiler_params=_cparams(("arbitrary", "arbitrary")),
        name="mod",
    )(c_all, w_mod, b_mod.reshape(depth, 1, n))


def _inproj_segments(d, tn):
    widths = (MIX_W,) * 6 + (d, d)
    segs, lo = [], 0
    for w in widths:
        assert w % tn == 0
        segs.append((lo, w // tn))
        lo += w // tn
    return tuple(segs)


def _inproj_write(ref, kind, acc):
    if kind == "f32":
        ref[...] = acc
    elif kind == "bf16":
        ref[...] = acc.astype(BF16)
    elif kind == "qa":
        ref[...] = (acc * QA_SCALE).astype(BF16)
    elif kind == "qb":
        ref[...] = (acc * QB_SCALE).astype(BF16)
    elif kind == "gate":
        ref[...] = _sigmoid(acc).astype(BF16)
    elif kind == "key_major":
        for h in range(ref.shape[0]):
            ref[h] = acc[:, h * HEAD_W:(h + 1) * HEAD_W].T
    else:
        raise ValueError(kind)


def _inproj_kernel(*refs, gpt, segs, plan, n_in, n_out):
    x_ref, sc_ref, sh_ref, w_ref = refs[:4]
    out_refs = refs[n_in:n_in + n_out]
    h_scr = refs[n_in + n_out]
    j = pl.program_id(1)

    @pl.when(j == 0)
    def _():
        h = _modulate(_ln(x_ref[...]), sc_ref[...], sh_ref[...], gpt)
        h_scr[...] = h.astype(BF16)

    for (lo, cnt), writers in zip(segs, plan):
        @pl.when((j >= lo) & (j < lo + cnt))
        def _(writers=writers):
            acc = jnp.dot(h_scr[...], w_ref[...], preferred_element_type=F32)
            for idx, kind in writers:
                _inproj_write(out_refs[idx], kind, acc)


def _mod_spec(gpt, rows_per_group, bm, group0, which, d):
    if gpt == 1:
        tiles_per_group = rows_per_group // bm
        return pl.BlockSpec((None, 1, 1, d), lambda i, *_: (0, group0 + i // tiles_per_group, 0, which))
    return pl.BlockSpec((None, gpt, 1, d), lambda i, *_: (0, group0 // gpt + i, 0, which))


def _inproj_call(x, mod_l, w_in, kv_carry, *, layer, depth, rows_per_group, group0, bm, tn, key_major):
    m, d = x.shape
    n = w_in.shape[1]
    gpt = max(1, bm // rows_per_group)
    segs = _inproj_segments(d, tn)
    assert n == tn * (segs[-1][0] + segs[-1][1])
    n_carry = len(kv_carry)

    def col(seg):
        lo, cnt = seg
        return lambda j: jnp.clip(j - lo, 0, cnt - 1)

    def rows(seg, width, dtype):
        c = col(seg)
        return (jax.ShapeDtypeStruct((m, width), dtype), pl.BlockSpec((bm, tn), lambda i, j: (i, c(j))))

    def slab(seg):
        c = col(seg)
        return (jax.ShapeDtypeStruct((depth, m, MIX_W), F32),
                pl.BlockSpec((None, bm, tn), lambda i, j: (layer, i, c(j))))

    def key_major_slab(seg):
        c = col(seg)
        tiles_per_seq = rows_per_group // bm
        return (jax.ShapeDtypeStruct((depth, m // rows_per_group, N_HEADS, HEAD_W, rows_per_group), F32),
                pl.BlockSpec((None, None, tn // HEAD_W, HEAD_W, bm),
                             lambda i, j: (layer, i // tiles_per_seq, c(j), 0, i % tiles_per_seq)))

    outs = {"qa": rows(segs[0], MIX_W, BF16)}
    if key_major:
        assert rows_per_group % bm == 0
        outs["ka"] = rows(segs[1], MIX_W, BF16)
        outs["ka_all"] = key_major_slab(segs[1])
        ka_writers = (("ka", "bf16"), ("ka_all", "key_major"))
    else:
        outs["ka_all"] = slab(segs[1])
        ka_writers = (("ka_all", "f32"),)
    outs["va_all"] = slab(segs[2])
    outs["qb"] = rows(segs[3], MIX_W, BF16)
    outs["kb"] = rows(segs[4], MIX_W, F32)
    outs["vb"] = rows(segs[5], MIX_W, F32)
    outs["sga"] = rows(segs[6], d, BF16)
    outs["sgb"] = rows(segs[7], d, BF16)
    names = list(outs)
    plan = tuple(tuple((names.index(nm), kind) for nm, kind in writers) for writers in (
        (("qa", "qa"),), ka_writers, (("va_all", "f32"),), (("qb", "qb"),),
        (("kb", "f32"),), (("vb", "f32"),), (("sga", "gate"),), (("sgb", "gate"),)))
    n_in = 4 + n_carry
    aliases = {4 + c: names.index(nm) for c, nm in enumerate(("ka_all", "va_all")[:n_carry])}
    res = pl.pallas_call(
        functools.partial(_inproj_kernel, gpt=gpt, segs=segs, plan=plan, n_in=n_in, n_out=len(names)),
        out_shape=tuple(outs[nm][0] for nm in names),
        grid=(m // bm, n // tn),
        in_specs=[
            pl.BlockSpec((bm, d), lambda i, j: (i, 0)),
            _mod_spec(gpt, rows_per_group, bm, group0, 1, d),
            _mod_spec(gpt, rows_per_group, bm, group0, 0, d),
            pl.BlockSpec((d, tn), lambda i, j: (0, j)),
        ] + [pl.BlockSpec(memory_space=pl.ANY)] * n_carry,
        out_specs=tuple(outs[nm][1] for nm in names),
        scratch_shapes=[pltpu.VMEM((bm, d), BF16)],
        input_output_aliases=aliases,
        compiler_params=_cparams(("arbitrary", "arbitrary")),
        name="inproj",
    )(x, mod_l, mod_l, w_in, *kv_carry)
    return dict(zip(names, res))


def _t5_bucket(rel):
    nb = T5_BUCKETS // 2
    max_exact = nb // 2
    rel = np.asarray(rel, np.int32)
    bucket = (rel > 0).astype(np.int32) * nb
    n = np.abs(rel)
    nf = np.maximum(n, 1).astype(np.float32)
    ratio = np.log(nf / np.float32(max_exact)) / np.float32(math.log(T5_MAX_DIST / max_exact))
    large = max_exact + (ratio * np.float32(nb - max_exact)).astype(np.int32)
    large = np.minimum(large, nb - 1)
    return bucket + np.where(n < max_exact, n, large)


def _lookup(table, idx):
    n = table.shape[0]
    flat = jnp.asarray(np.asarray(idx, np.int32).reshape(-1))
    onehot = (flat[None, :] == jnp.arange(n, dtype=jnp.int32)[:, None]).astype(F32)
    out = lax.dot_general(table.astype(F32), onehot, (((0,), (0,)), ((), ())),
                          precision=lax.Precision.HIGHEST, preferred_element_type=F32)
    return out.reshape((table.shape[1],) + tuple(np.shape(idx)))


def _t5_table(t5_bias):
    far = t5_bias[int(_t5_bucket(-T5_MAX_DIST))]
    return (t5_bias - far[None, :]) * LOG2E


def _attn_a_bias_tiles(t5_tab, t):
    assert t >= T5_MAX_DIST
    k = np.arange(t)[:, None]
    q = np.arange(t)[None, :]
    visible = (k // CHUNK) <= (q // CHUNK)
    idx = np.stack([_t5_bucket(k - q), _t5_bucket(k - t - q)])
    vis = np.stack([visible, np.ones_like(visible)])
    return jnp.where(vis[None], _lookup(t5_tab, idx), NEG)


def _attn_b_bias_tile(rel_tab, tq, win):
    kw = np.arange(win)[:, None] - BAND_REACH
    q = np.arange(tq)[None, :]
    rel = np.clip(kw - q, -REL_CLIP, REL_CLIP) + REL_CLIP
    dchunk = q // CHUNK - np.floor_divide(kw, CHUNK)
    visible = (dchunk >= 0) & (dchunk <= BAND_PREV_CHUNKS)
    return jnp.where(visible[None], _lookup(rel_tab, rel), NEG)


def _diff_lambda(lam_ref, lam_init):
    lv = lam_ref[...]
    s1 = jnp.sum(lv[0:1] * lv[1:2], axis=-1, keepdims=True)
    s2 = jnp.sum(lv[2:3] * lv[3:4], axis=-1, keepdims=True)
    return jnp.exp(s1) - jnp.exp(s2) + lam_init


def _stack_diff_queries(q):
    lane = lax.broadcasted_iota(jnp.int32, q.shape, 1)
    zero = jnp.zeros_like(q)
    return jnp.concatenate([jnp.where(lane < DK_A, q, zero), jnp.where(lane >= DK_A, q, zero)], axis=0)


def _attn_a_kernel(q_ref, k_ref, v_ref, bias_ref, lam_ref, g_ref, o_ref,
                   vt_scr, s_scr, acc_scr, m_scr, *, t, nh, lam_init):
    qi = pl.program_id(2)
    nkb = vt_scr.shape[1]
    heads = range(nh)

    @pl.when(qi == 0)
    def _():
        sub = lax.broadcasted_iota(jnp.int32, (ONES_ROWS, t), 0)
        ones_rows = jnp.where(sub == 0, 1.0, 0.0).astype(BF16)
        for hh in heads:
            cols = slice(hh * HEAD_W, (hh + 1) * HEAD_W)
            for kb in range(nkb):
                vt_scr[hh, kb, 0:HEAD_W, :] = v_ref[kb * t:(kb + 1) * t, cols].T.astype(BF16)
                vt_scr[hh, kb, HEAD_W:HEAD_W + ONES_ROWS, :] = ones_rows

    q2 = [_stack_diff_queries(q_ref[:, hh * HEAD_W:(hh + 1) * HEAD_W]) for hh in heads]
    acc_scr[...] = jnp.zeros(acc_scr.shape, F32)

    def scores(hh, kb):
        start = pl.multiple_of(kb * t, t)
        s_scr[hh] = _nt_dot(k_ref[pl.ds(start, t), hh * HEAD_W:(hh + 1) * HEAD_W], q2[hh])

    def step(kb, ms, bias_idx, prefetch):
        out = []
        for hh in heads:
            s = s_scr[hh]
            if bias_idx is not None:
                bias = bias_ref[hh, bias_idx]
                s = s + jnp.concatenate([bias, bias], axis=1)
            m_new = jnp.maximum(ms[hh], jnp.max(s, axis=0, keepdims=True))
            alpha = jnp.exp2(ms[hh] - m_new)
            e = jnp.exp2(s - m_new).astype(BF16)
            pv = jnp.dot(vt_scr[hh, kb], e, preferred_element_type=F32)
            if prefetch:
                scores(hh, kb + 1)
            acc_scr[hh] = alpha * acc_scr[hh] + pv
            out.append(m_new)
        return tuple(out)

    def load_m():
        return tuple(m_scr[hh] for hh in heads)

    def store_m(ms):
        for hh in heads:
            m_scr[hh] = ms[hh]

    for hh in heads:
        scores(hh, 0)
    m0 = tuple(jnp.full((1, 2 * t), -jnp.inf, F32) for _ in heads)
    n_far = jnp.maximum(qi - 1, 0)
    store_m(lax.fori_loop(
        0, lax.shift_right_logical(n_far, 1),
        lambda p, ms: step(2 * p + 1, step(2 * p, ms, None, True), None, True), m0))

    @pl.when(lax.rem(n_far, 2) == 1)
    def _():
        store_m(step(n_far - 1, load_m(), None, True))

    @pl.when(qi >= 1)
    def _():
        store_m(step(qi - 1, load_m(), 1, True))

    step(qi, load_m(), 0, False)

    lam = _diff_lambda(lam_ref, lam_init)
    for hh in heads:
        o2 = acc_scr[hh, 0:HEAD_W, :] / acc_scr[hh, HEAD_W:HEAD_W + 1, :]
        o = o2[:, :t] - lam * o2[:, t:]
        o = o * lax.rsqrt(jnp.mean(o * o, axis=0, keepdims=True) + LN_EPS)
        o_ref[:, hh * HEAD_W:(hh + 1) * HEAD_W] = (o.T * g_ref[...] * (1.0 - lam_init)).astype(BF16)


def _attn_a_call(qa, ka, va_all, bias_tiles, lam_vecs, g, *, layer, batch, seq, lam_init):
    t, nh = TQ_A, HEADS_PER_STEP_A
    nq = seq // t
    w = nh * HEAD_W
    return pl.pallas_call(
        functools.partial(_attn_a_kernel, t=t, nh=nh, lam_init=lam_init),
        out_shape=jax.ShapeDtypeStruct((batch * seq, MIX_W), BF16),
        grid=(batch, N_HEADS // nh, nq),
        in_specs=[
            pl.BlockSpec((t, w), lambda b, h, qi: (b * nq + qi, h)),
            pl.BlockSpec((seq, w), lambda b, h, qi: (b, h)),
            pl.BlockSpec((None, seq, w), lambda b, h, qi: (layer, b, h)),
            pl.BlockSpec((nh, 2, t, t), lambda b, h, qi: (h, 0, 0, 0)),
            pl.BlockSpec((4, DK_A), lambda b, h, qi: (0, 0)),
            pl.BlockSpec((1, HEAD_W), lambda b, h, qi: (0, 0)),
        ],
        out_specs=pl.BlockSpec((t, w), lambda b, h, qi: (b * nq + qi, h)),
        scratch_shapes=[
            pltpu.VMEM((nh, seq // t, HEAD_W + ONES_ROWS, t), BF16),
            pltpu.VMEM((nh, t, 2 * t), F32),
            pltpu.VMEM((nh, HEAD_W + ONES_ROWS, 2 * t), F32),
            pltpu.VMEM((nh, 1, 2 * t), F32),
        ],
        compiler_params=_cparams(("arbitrary", "arbitrary", "arbitrary")),
        name="attn_a",
    )(qa, ka, va_all, bias_tiles, lam_vecs, g)


def _attn_b_kernel(q_ref, k_ref, v_ref, bias_ref, o_ref, k_scr, vt_scr, *, tq, win, seq, units):
    qi = pl.program_id(2)
    pad_blocks = BAND_REACH // tq
    win_blocks = win // tq

    @pl.when(qi == 0)
    def _():
        k_scr[0:BAND_REACH, :] = jnp.zeros((BAND_REACH, HEAD_W), BF16)
        k_scr[BAND_REACH:BAND_REACH + seq, :] = k_ref[...].astype(BF16)
        for blk in range(pad_blocks):
            vt_scr[blk] = jnp.zeros((HEAD_W, tq), BF16)
        for blk in range(seq // tq):
            vt_scr[pad_blocks + blk] = v_ref[blk * tq:(blk + 1) * tq, :].T.astype(BF16)

    def body(mask_early_keys):
        bias = bias_ref[...]
        raw = []
        for u in range(units):
            start = pl.multiple_of((qi * units + u) * tq, tq)
            raw.append(_nt_dot(k_scr[pl.ds(start, win), :], q_ref[u * tq:(u + 1) * tq, :]))
        for u in range(units):
            blk0 = qi * units + u
            s = raw[u] + bias
            if mask_early_keys:
                row = lax.broadcasted_iota(jnp.int32, (win, tq), 0)
                s = jnp.where(row + blk0 * tq >= BAND_REACH, s, NEG)
            m = jnp.max(s, axis=0, keepdims=True)
            e = jnp.exp2(s - m)
            l = jnp.sum(e, axis=0, keepdims=True)
            e = e.astype(BF16)
            acc = None
            for w in range(win_blocks):
                pv = jnp.dot(vt_scr[blk0 + w], e[w * tq:(w + 1) * tq, :], preferred_element_type=F32)
                acc = pv if acc is None else acc + pv
            o_ref[u * tq:(u + 1) * tq, :] = (acc / l).T.astype(BF16)

    early_steps = -(-BAND_REACH // (units * tq))

    @pl.when(qi < early_steps)
    def _():
        body(True)

    @pl.when(qi >= early_steps)
    def _():
        body(False)


def _attn_b_call(qb, kb, vb, bias_tile, *, batch, seq):
    tq, win, units = TQ_B, WIN_B, UNITS_B
    rows = tq * units
    nq = seq // rows
    return pl.pallas_call(
        functools.partial(_attn_b_kernel, tq=tq, win=win, seq=seq, units=units),
        out_shape=jax.ShapeDtypeStruct((batch * seq, MIX_W), BF16),
        grid=(batch, N_HEADS, nq),
        in_specs=[
            pl.BlockSpec((rows, HEAD_W), lambda b, h, qi: (b * nq + qi, h)),
            pl.BlockSpec((seq, HEAD_W), lambda b, h, qi: (b, h)),
            pl.BlockSpec((seq, HEAD_W), lambda b, h, qi: (b, h)),
            pl.BlockSpec((None, win, tq), lambda b, h, qi: (h, 0, 0)),
        ],
        out_specs=pl.BlockSpec((rows, HEAD_W), lambda b, h, qi: (b * nq + qi, h)),
        scratch_shapes=[
            pltpu.VMEM((BAND_REACH + seq, HEAD_W), BF16),
            pltpu.VMEM(((BAND_REACH + seq) // tq, HEAD_W, tq), BF16),
        ],
        compiler_params=_cparams(("arbitrary", "arbitrary", "arbitrary")),
        name="attn_b",
    )(qb, kb, vb, bias_tile)


def _softmax_pv(parts):
    m = None
    for s, _ in parts:
        pm = jnp.max(s, axis=-1, keepdims=True)
        m = pm if m is None else jnp.maximum(m, pm)
    l = None
    o = None
    for s, v in parts:
        e = jnp.exp2(s - m)
        pl_ = jnp.sum(e, axis=-1, keepdims=True)
        po = jnp.dot(e.astype(BF16), v, preferred_element_type=F32)
        l = pl_ if l is None else l + pl_
        o = po if o is None else o + po
    return o / l


def _sample_kernel(qa_ref, ka_ref, va_ref, cakt_ref, cav_ref, qb_ref, kb_ref, vb_ref, cbk_ref, cbv_ref,
                   ba_c_ref, ba_n_ref, bb_c_ref, bb_n_ref, lam_ref, g_ref, oa_ref, ob_ref,
                   *, t_new, lam_init):
    lam = _diff_lambda(lam_ref, lam_init)
    g = g_ref[...]
    past = cakt_ref.shape[-1]

    def head_rows(ref, h):
        return ref[pl.ds(h, ref.shape[0] // N_HEADS, stride=N_HEADS), :].astype(BF16)

    for h in range(N_HEADS):
        cols = slice(h * HEAD_W, (h + 1) * HEAD_W)
        q2 = _stack_diff_queries(qa_ref[:, cols])
        kct = cakt_ref[h].reshape(2 * DK_A, past).astype(BF16)
        s_c = jnp.dot(q2, kct, preferred_element_type=F32)
        s_n = _nt_dot(q2, ka_ref[:, cols].astype(BF16))
        s_c = (s_c.reshape(2, t_new, -1) + ba_c_ref[h][None]).reshape(2 * t_new, -1)
        s_n = (s_n.reshape(2, t_new, -1) + ba_n_ref[h][None]).reshape(2 * t_new, -1)
        o2 = _softmax_pv([(s_c, head_rows(cav_ref, h)), (s_n, va_ref[:, cols].astype(BF16))])
        o = o2[:t_new] - lam * o2[t_new:]
        o = o * lax.rsqrt(jnp.mean(o * o, axis=-1, keepdims=True) + LN_EPS) * g
        oa_ref[:, cols] = (o * (1.0 - lam_init)).astype(BF16)
        qh = qb_ref[:, cols]
        s_c = _nt_dot(qh, head_rows(cbk_ref, h)) + bb_c_ref[h]
        s_n = _nt_dot(qh, kb_ref[:, cols].astype(BF16)) + bb_n_ref[h]
        ob_ref[:, cols] = _softmax_pv([(s_c, head_rows(cbv_ref, h)),
                                       (s_n, vb_ref[:, cols].astype(BF16))]).astype(BF16)


def _sample_call(layer, qa, ka_all, va_all, cache_a_kt, cache_a_v, qb, kb, vb, cache_b_k, cache_b_v,
                 ba_c, ba_n, bb_c, bb_n, lam_vecs, g, *, batch, t_new, lam_init):
    past = cache_a_kt.shape[-1]
    band_past = cache_b_k.shape[2] // N_HEADS
    row = lambda b: (b, 0)
    slab = lambda b: (layer, b, 0)
    cache = lambda b: (layer, b, 0, 0)
    cache_t = lambda b: (layer, b, 0, 0, 0, 0)
    const3 = lambda b: (0, 0, 0)
    const2 = lambda b: (0, 0)
    return pl.pallas_call(
        functools.partial(_sample_kernel, t_new=t_new, lam_init=lam_init),
        out_shape=(jax.ShapeDtypeStruct((batch * t_new, MIX_W), BF16),
                   jax.ShapeDtypeStruct((batch * t_new, MIX_W), BF16)),
        grid=(batch,),
        in_specs=[
            pl.BlockSpec((t_new, MIX_W), row),
            pl.BlockSpec((None, t_new, MIX_W), slab),
            pl.BlockSpec((None, t_new, MIX_W), slab),
            pl.BlockSpec((None, None, N_HEADS, 2, DK_A, past), cache_t),
            pl.BlockSpec((None, None, past * N_HEADS, HEAD_W), cache),
            pl.BlockSpec((t_new, MIX_W), row),
            pl.BlockSpec((t_new, MIX_W), row),
            pl.BlockSpec((t_new, MIX_W), row),
            pl.BlockSpec((None, None, band_past * N_HEADS, HEAD_W), cache),
            pl.BlockSpec((None, None, band_past * N_HEADS, HEAD_W), cache),
            pl.BlockSpec(ba_c.shape, const3),
            pl.BlockSpec(ba_n.shape, const3),
            pl.BlockSpec(bb_c.shape, const3),
            pl.BlockSpec(bb_n.shape, const3),
            pl.BlockSpec((4, DK_A), const2),
            pl.BlockSpec((1, HEAD_W), const2),
        ],
        out_specs=(pl.BlockSpec((t_new, MIX_W), row), pl.BlockSpec((t_new, MIX_W), row)),
        compiler_params=_cparams(("arbitrary",)),
        name="sample_mix",
    )(qa, ka_all, va_all, cache_a_kt, cache_a_v, qb, kb, vb, cache_b_k, cache_b_v,
      ba_c, ba_n, bb_c, bb_n, lam_vecs, g)


def _mixout_kernel(oa_ref, ob_ref, sga_ref, sgb_ref, x_ref, gm_ref, woa_ref, wob_ref, wout_ref,
                   lg_ref, lb_ref, o_ref, *, gpt, alpha):
    a = jnp.dot(oa_ref[...], woa_ref[...], preferred_element_type=F32)
    b = jnp.dot(ob_ref[...], wob_ref[...], preferred_element_type=F32)
    merged = sga_ref[...].astype(F32) * a + sgb_ref[...].astype(F32) * b
    y = jnp.dot(merged.astype(BF16), wout_ref[...], preferred_element_type=F32)
    z = alpha * x_ref[...] + _gate(y, gm_ref[...], gpt)
    o_ref[...] = _ln(z) * lg_ref[...] + lb_ref[...]


def _mixout_call(oa, ob, sga, sgb, x, mod_l, w_oa, w_ob, w_out, ln_g, ln_b,
                 *, rows_per_group, group0, bm, alpha):
    m, d = x.shape
    gpt = max(1, bm // rows_per_group)
    row = lambda i: (i, 0)
    const = lambda i: (0, 0)
    resident = pl.Buffered(1)
    return pl.pallas_call(
        functools.partial(_mixout_kernel, gpt=gpt, alpha=alpha),
        out_shape=jax.ShapeDtypeStruct((m, d), F32),
        grid=(m // bm,),
        in_specs=[
            pl.BlockSpec((bm, MIX_W), row),
            pl.BlockSpec((bm, MIX_W), row),
            pl.BlockSpec((bm, d), row),
            pl.BlockSpec((bm, d), row),
            pl.BlockSpec((bm, d), row),
            _mod_spec(gpt, rows_per_group, bm, group0, 2, d),
            pl.BlockSpec((MIX_W, d), const, pipeline_mode=resident),
            pl.BlockSpec((MIX_W, d), const, pipeline_mode=resident),
            pl.BlockSpec((d, d), const, pipeline_mode=resident),
            pl.BlockSpec((1, d), const),
            pl.BlockSpec((1, d), const),
        ],
        out_specs=pl.BlockSpec((bm, d), row),
        compiler_params=_cparams(("arbitrary",)),
        name="mixout",
    )(oa, ob, sga, sgb, x, mod_l, w_oa, w_ob, w_out, ln_g, ln_b)


def _ffn_kernel(x_ref, sc_ref, sh_ref, gf_ref, w1_ref, w3_ref, w2_ref, lg_ref, lb_ref, o_ref,
                h_scr, acc_scr, *, gpt, alpha):
    kf = pl.program_id(1)

    @pl.when(kf == 0)
    def _():
        h = _modulate(_ln(x_ref[...]), sc_ref[...], sh_ref[...], gpt)
        h_scr[...] = h.astype(BF16)
        acc_scr[...] = jnp.zeros(acc_scr.shape, F32)

    h = h_scr[...]
    a = jnp.dot(h, w1_ref[...], preferred_element_type=F32)
    b = jnp.dot(h, w3_ref[...], preferred_element_type=F32)
    u = (a * _sigmoid(a)) * b
    acc_scr[...] += jnp.dot(u.astype(BF16), w2_ref[...], preferred_element_type=F32)

    @pl.when(kf == pl.num_programs(1) - 1)
    def _():
        z = alpha * x_ref[...] + _gate(acc_scr[...], gf_ref[...], gpt)
        o_ref[...] = _ln(z) * lg_ref[...] + lb_ref[...]


def _ffn_call(x, mod_l, w1, w3, w2, ln_g, ln_b, *, rows_per_group, group0, bm, tf, alpha):
    m, d = x.shape
    dff = w1.shape[1]
    gpt = max(1, bm // rows_per_group)
    return pl.pallas_call(
        functools.partial(_ffn_kernel, gpt=gpt, alpha=alpha),
        out_shape=jax.ShapeDtypeStruct((m, d), F32),
        grid=(m // bm, dff // tf),
        in_specs=[
            pl.BlockSpec((bm, d), lambda i, k: (i, 0)),
            _mod_spec(gpt, rows_per_group, bm, group0, 4, d),
            _mod_spec(gpt, rows_per_group, bm, group0, 3, d),
            _mod_spec(gpt, rows_per_group, bm, group0, 5, d),
            pl.BlockSpec((d, tf), lambda i, k: (0, k)),
            pl.BlockSpec((d, tf), lambda i, k: (0, k)),
            pl.BlockSpec((tf, d), lambda i, k: (k, 0)),
            pl.BlockSpec((1, d), lambda i, k: (0, 0)),
            pl.BlockSpec((1, d), lambda i, k: (0, 0)),
        ],
        out_specs=pl.BlockSpec((bm, d), lambda i, k: (i, 0)),
        scratch_shapes=[pltpu.VMEM((bm, d), BF16), pltpu.VMEM((bm, d), F32)],
        compiler_params=_cparams(("arbitrary", "arbitrary")),
        name="ffn",
    )(x, mod_l, mod_l, mod_l, w1, w3, w2, ln_g, ln_b)


def _ff_tile(dff):
    for tf in (512, 256, 128):
        if dff % tf == 0:
            return tf
    raise ValueError(f"unsupported FFN width {dff}")


def kernel(x_prompt, x_sample, cache_a_k, cache_a_v, cache_b_k, cache_b_v, c_prompt, c_sample,
           w_mod, b_mod, w_in, lambda_q1, lambda_k1, lambda_q2, lambda_k2, subln_g, t5_bias,
           rel_bias, w_oa, w_ob, w_out, ln1_g, ln1_b, w1, w3, w2, ln2_g, ln2_b):
    depth = w_mod.shape[0]
    batch, seq, d = x_prompt.shape
    dec_batch, t_new, _ = x_sample.shape
    past = cache_a_k.shape[2]
    band_past = cache_b_k.shape[2]
    assert past % CHUNK == 0 and t_new <= CHUNK and band_past == BAND_REACH and past >= BAND_REACH
    assert seq % TQ_A == 0 and seq % (TQ_B * UNITS_B) == 0 and seq >= BAND_REACH and d % 128 == 0
    alpha = (2 * depth) ** 0.25
    prompt_band = min(BAND_REACH, seq)

    bm_p = min(512, seq)
    bm_s = min(512, dec_batch * t_new)
    bm_in = min(1024, seq)
    tn = math.gcd(MIX_W // 2, d)
    tf = _ff_tile(w1.shape[2])

    c_all = jnp.concatenate([c_sample, c_prompt], axis=0)
    mod = _mod_call(c_all, w_mod, b_mod).reshape(depth, dec_batch + batch, 1, 6 * d)

    t5_tab = _t5_table(t5_bias)
    ta_tiles = _attn_a_bias_tiles(t5_tab, TQ_A)
    q_pos_s = past + np.arange(t_new)
    rel_a = np.arange(past + t_new)[None, :] - q_pos_s[:, None]
    ba_s = _lookup(t5_tab, _t5_bucket(rel_a))
    kb_pos_s = past - band_past + np.arange(band_past + t_new)
    rel_b = np.clip(kb_pos_s[None, :] - q_pos_s[:, None], -REL_CLIP, REL_CLIP) + REL_CLIP

    cakt = jnp.transpose(cache_a_k, (0, 1, 3, 4, 5, 2))
    cav = cache_a_v.reshape(depth, dec_batch, past * N_HEADS, HEAD_W)
    cbk = cache_b_k.reshape(depth, dec_batch, band_past * N_HEADS, HEAD_W)
    cbv = cache_b_v.reshape(depth, dec_batch, band_past * N_HEADS, HEAD_W)

    xp = x_prompt.reshape(batch * seq, d)
    xs = x_sample.reshape(dec_batch * t_new, d)
    kv_p, kv_s = (), ()
    bk_p, bv_p, bk_s, bv_s = [], [], [], []
    for l in range(depth):
        lam_init = 0.8 - 0.6 * math.exp(-0.3 * l)
        lam_vecs = jnp.stack([lambda_q1[l], lambda_k1[l], lambda_q2[l], lambda_k2[l]]).astype(F32)
        g = subln_g[l].reshape(1, HEAD_W).astype(F32)
        mod_l = mod[l:l + 1]
        w_in_l = w_in[l].astype(BF16)
        w_oa_l, w_ob_l, w_out_l = w_oa[l].astype(BF16), w_ob[l].astype(BF16), w_out[l].astype(BF16)
        w1_l, w3_l, w2_l = w1[l].astype(BF16), w3[l].astype(BF16), w2[l].astype(BF16)
        ln1 = (ln1_g[l].reshape(1, d), ln1_b[l].reshape(1, d))
        ln2 = (ln2_g[l].reshape(1, d), ln2_b[l].reshape(1, d))
        rel_tab = rel_bias[l] * LOG2E
        bb_tile = _attn_b_bias_tile(rel_tab, TQ_B, WIN_B)
        bb_s = _lookup(rel_tab, rel_b)

        grp = dict(rows_per_group=seq, group0=dec_batch, bm=bm_p)
        p = _inproj_call(xp, mod_l, w_in_l, kv_p, layer=l, depth=depth, tn=tn, key_major=True,
                         **dict(grp, bm=bm_in))
        kv_p = (p["ka_all"], p["va_all"])
        oa = _attn_a_call(p["qa"], p["ka"], p["va_all"], ta_tiles, lam_vecs, g,
                          layer=l, batch=batch, seq=seq, lam_init=lam_init)
        ob = _attn_b_call(p["qb"], p["kb"], p["vb"], bb_tile, batch=batch, seq=seq)
        xp = _mixout_call(oa, ob, p["sga"], p["sgb"], xp, mod_l, w_oa_l, w_ob_l, w_out_l, *ln1,
                          alpha=alpha, **grp)
        xp = _ffn_call(xp, mod_l, w1_l, w3_l, w2_l, *ln2, tf=tf, alpha=alpha, **grp)
        bk_p.append(p["kb"].reshape(batch, seq, N_HEADS, HEAD_W)[:, seq - prompt_band:])
        bv_p.append(p["vb"].reshape(batch, seq, N_HEADS, HEAD_W)[:, seq - prompt_band:])

        grp = dict(rows_per_group=t_new, group0=0, bm=bm_s)
        p = _inproj_call(xs, mod_l, w_in_l, kv_s, layer=l, depth=depth, tn=tn, key_major=False, **grp)
        kv_s = (p["ka_all"], p["va_all"])
        oa, ob = _sample_call(l, p["qa"], p["ka_all"], p["va_all"], cakt, cav, p["qb"], p["kb"],
                              p["vb"], cbk, cbv,
                              ba_s[:, :, :past], ba_s[:, :, past:], bb_s[:, :, :band_past],
                              bb_s[:, :, band_past:], lam_vecs, g,
                              batch=dec_batch, t_new=t_new, lam_init=lam_init)
        xs = _mixout_call(oa, ob, p["sga"], p["sgb"], xs, mod_l, w_oa_l, w_ob_l, w_out_l, *ln1,
                          alpha=alpha, **grp)
        xs = _ffn_call(xs, mod_l, w1_l, w3_l, w2_l, *ln2, tf=tf, alpha=alpha, **grp)
        bk_s.append(p["kb"].reshape(dec_batch, t_new, N_HEADS, HEAD_W))
        bv_s.append(p["vb"].reshape(dec_batch, t_new, N_HEADS, HEAD_W))

    a_k_prompt = jnp.transpose(kv_p[0].reshape(depth, batch, N_HEADS, 2, DK_A, seq), (0, 1, 5, 2, 3, 4))
    return (xp.reshape(batch, seq, d), xs.reshape(dec_batch, t_new, d),
            a_k_prompt,
            kv_p[1].reshape(depth, batch, seq, N_HEADS, HEAD_W),
            jnp.stack(bk_p), jnp.stack(bv_p),
            kv_s[0].reshape(depth, dec_batch, t_new, N_HEADS, 2, DK_A),
            kv_s[1].reshape(depth, dec_batch, t_new, N_HEADS, HEAD_W),
            jnp.stack(bk_s), jnp.stack(bv_s))
```

```python
import functools
import math

import jax
import jax.numpy as jnp
import numpy as np
from jax import lax
from jax.experimental import pallas as pl
from jax.experimental.pallas import tpu as pltpu

F32 = jnp.float32
BF16 = jnp.bfloat16

CHUNK = 64
N_HEADS = 8
HEAD_W = 128
DK_A = 64
BAND_PREV_CHUNKS = 8
BAND_REACH = BAND_PREV_CHUNKS * CHUNK
REL_CLIP = 128
T5_BUCKETS = 32
T5_MAX_DIST = 128
NEG = -1e30
LN_EPS = 1e-5
MIX_W = N_HEADS * HEAD_W
LOG2E = math.log2(math.e)
QA_SCALE = DK_A ** -0.5 * LOG2E
QB_SCALE = HEAD_W ** -0.5 * LOG2E

V7X_VMEM_BYTES = 64 * 1024 * 1024
VMEM_LIMIT = V7X_VMEM_BYTES - 3 * 1024 * 1024

TQ_A = 256
HEADS_PER_STEP_A = 4
FFN_SUB_ROWS = 512
LN_CHUNK_ROWS = 256
ONES_ROWS = 16
TQ_B = 128
WIN_B = BAND_REACH + TQ_B
UNITS_B = 4


def _cparams(sem):
    return pltpu.CompilerParams(dimension_semantics=sem, vmem_limit_bytes=VMEM_LIMIT)


def _sigmoid(x):
    return 1.0 / (1.0 + jnp.exp(-x))


def _ln(x):
    mu = jnp.mean(x, axis=-1, keepdims=True)
    xc = x - mu
    var = jnp.mean(xc * xc, axis=-1, keepdims=True)
    return xc * lax.rsqrt(var + LN_EPS)


def _modulate(y, scale, shift, gpt):
    if gpt == 1:
        return y * (1.0 + scale[0]) + shift[0]
    bm, d = y.shape
    y3 = y.reshape(gpt, bm // gpt, d)
    return (y3 * (1.0 + scale) + shift).reshape(bm, d)


def _gate(y, gate, gpt):
    if gpt == 1:
        return y * gate[0]
    bm, d = y.shape
    return (y.reshape(gpt, bm // gpt, d) * gate).reshape(bm, d)


def _nt_dot(a, b):
    return lax.dot_general(a, b, (((1,), (1,)), ((), ())), preferred_element_type=F32)


def _mod_kernel(c_ref, w_ref, b_ref, o_ref):
    c = c_ref[...]
    s = (c * _sigmoid(c)).astype(BF16)
    o_ref[...] = jnp.dot(s, w_ref[...].astype(BF16), preferred_element_type=F32) + b_ref[...]


def _mod_call(c_all, w_mod, b_mod):
    depth, d, n = w_mod.shape
    g = c_all.shape[0]
    tn = math.gcd(n, 1024)
    return pl.pallas_call(
        _mod_kernel,
        out_shape=jax.ShapeDtypeStruct((depth, g, n), F32),
        grid=(depth, n // tn),
        in_specs=[
            pl.BlockSpec((g, d), lambda l, j: (0, 0)),
            pl.BlockSpec((None, d, tn), lambda l, j: (l, 0, j)),
            pl.BlockSpec((None, 1, tn), lambda l, j: (l, 0, j)),
        ],
        out_specs=pl.BlockSpec((None, g, tn), lambda l, j: (l, 0, j)),
        compiler_params=_cparams(("arbitrary", "arbitrary")),
        name="mod",
    )(c_all, w_mod, b_mod.reshape(depth, 1, n))


def _inproj_segments(d, tn):
    widths = (MIX_W,) * 6 + (d, d)
    segs, lo = [], 0
    for w in widths:
        assert w % tn == 0
        segs.append((lo, w // tn))
        lo += w // tn
    return tuple(segs)


def _inproj_write(ref, kind, acc):
    if kind == "f32":
        ref[...] = acc
    elif kind == "bf16":
        ref[...] = acc.astype(BF16)
    elif kind == "qa":
        ref[...] = (acc * QA_SCALE).astype(BF16)
    elif kind == "qb":
        ref[...] = (acc * QB_SCALE).astype(BF16)
    elif kind == "gate":
        ref[...] = _sigmoid(acc).astype(BF16)
    elif kind == "key_major":
        for h in range(ref.shape[0]):
            ref[h] = acc[:, h * HEAD_W:(h + 1) * HEAD_W].T
    else:
        raise ValueError(kind)


def _inproj_kernel(*refs, gpt, segs, plan, n_in, n_out):
    x_ref, sc_ref, sh_ref, w_ref = refs[:4]
    out_refs = refs[n_in:n_in + n_out]
    h_scr = refs[n_in + n_out]
    j = pl.program_id(1)

    @pl.when(j == 0)
    def _():
        h = _modulate(_ln(x_ref[...]), sc_ref[...], sh_ref[...], gpt)
        h_scr[...] = h.astype(BF16)

    for (lo, cnt), writers in zip(segs, plan):
        @pl.when((j >= lo) & (j < lo + cnt))
        def _(writers=writers):
            acc = jnp.dot(h_scr[...], w_ref[...], preferred_element_type=F32)
            for idx, kind in writers:
                _inproj_write(out_refs[idx], kind, acc)


def _mod_spec(gpt, rows_per_group, bm, group0, which, d):
    if gpt == 1:
        tiles_per_group = rows_per_group // bm
        return pl.BlockSpec((None, 1, 1, d), lambda i, *_: (0, group0 + i // tiles_per_group, 0, which))
    return pl.BlockSpec((None, gpt, 1, d), lambda i, *_: (0, group0 // gpt + i, 0, which))


def _inproj_call(x, mod_l, w_in, kv_carry, *, layer, depth, rows_per_group, group0, bm, tn, key_major):
    m, d = x.shape
    n = w_in.shape[1]
    gpt = max(1, bm // rows_per_group)
    segs = _inproj_segments(d, tn)
    assert n == tn * (segs[-1][0] + segs[-1][1])
    n_carry = len(kv_carry)

    def col(seg):
        lo, cnt = seg
        return lambda j: jnp.clip(j - lo, 0, cnt - 1)

    def rows(seg, width, dtype):
        c = col(seg)
        return (jax.ShapeDtypeStruct((m, width), dtype), pl.BlockSpec((bm, tn), lambda i, j: (i, c(j))))

    def slab(seg):
        c = col(seg)
        return (jax.ShapeDtypeStruct((depth, m, MIX_W), F32),
                pl.BlockSpec((None, bm, tn), lambda i, j: (layer, i, c(j))))

    def key_major_slab(seg):
        c = col(seg)
        tiles_per_seq = rows_per_group // bm
        return (jax.ShapeDtypeStruct((depth, m // rows_per_group, N_HEADS, HEAD_W, rows_per_group), F32),
                pl.BlockSpec((None, None, tn // HEAD_W, HEAD_W, bm),
                             lambda i, j: (layer, i // tiles_per_seq, c(j), 0, i % tiles_per_seq)))

    outs = {"qa": rows(segs[0], MIX_W, BF16)}
    if key_major:
        assert rows_per_group % bm == 0
        outs["ka"] = rows(segs[1], MIX_W, BF16)
        outs["ka_all"] = key_major_slab(segs[1])
        ka_writers = (("ka", "bf16"), ("ka_all", "key_major"))
    else:
        outs["ka_all"] = slab(segs[1])
        ka_writers = (("ka_all", "f32"),)
    outs["va_all"] = slab(segs[2])
    outs["qb"] = rows(segs[3], MIX_W, BF16)
    outs["kb"] = rows(segs[4], MIX_W, F32)
    outs["vb"] = rows(segs[5], MIX_W, F32)
    outs["sga"] = rows(segs[6], d, BF16)
    outs["sgb"] = rows(segs[7], d, BF16)
    names = list(outs)
    plan = tuple(tuple((names.index(nm), kind) for nm, kind in writers) for writers in (
        (("qa", "qa"),), ka_writers, (("va_all", "f32"),), (("qb", "qb"),),
        (("kb", "f32"),), (("vb", "f32"),), (("sga", "gate"),), (("sgb", "gate"),)))
    n_in = 4 + n_carry
    aliases = {4 + c: names.index(nm) for c, nm in enumerate(("ka_all", "va_all")[:n_carry])}
    res = pl.pallas_call(
        functools.partial(_inproj_kernel, gpt=gpt, segs=segs, plan=plan, n_in=n_in, n_out=len(names)),
        out_shape=tuple(outs[nm][0] for nm in names),
        grid=(m // bm, n // tn),
        in_specs=[
            pl.BlockSpec((bm, d), lambda i, j: (i, 0)),
            _mod_spec(gpt, rows_per_group, bm, group0, 1, d),
            _mod_spec(gpt, rows_per_group, bm, group0, 0, d),
            pl.BlockSpec((d, tn), lambda i, j: (0, j)),
        ] + [pl.BlockSpec(memory_space=pl.ANY)] * n_carry,
        out_specs=tuple(outs[nm][1] for nm in names),
        scratch_shapes=[pltpu.VMEM((bm, d), BF16)],
        input_output_aliases=aliases,
        compiler_params=_cparams(("arbitrary", "arbitrary")),
        name="inproj",
    )(x, mod_l, mod_l, w_in, *kv_carry)
    return dict(zip(names, res))


def _t5_bucket(rel):
    nb = T5_BUCKETS // 2
    max_exact = nb // 2
    rel = np.asarray(rel, np.int32)
    bucket = (rel > 0).astype(np.int32) * nb
    n = np.abs(rel)
    nf = np.maximum(n, 1).astype(np.float32)
    ratio = np.log(nf / np.float32(max_exact)) / np.float32(math.log(T5_MAX_DIST / max_exact))
    large = max_exact + (ratio * np.float32(nb - max_exact)).astype(np.int32)
    large = np.minimum(large, nb - 1)
    return bucket + np.where(n < max_exact, n, large)


def _lookup(table, idx):
    n = table.shape[0]
    flat = jnp.asarray(np.asarray(idx, np.int32).reshape(-1))
    onehot = (flat[None, :] == jnp.arange(n, dtype=jnp.int32)[:, None]).astype(F32)
    out = lax.dot_general(table.astype(F32), onehot, (((0,), (0,)), ((), ())),
                          precision=lax.Precision.HIGHEST, preferred_element_type=F32)
    return out.reshape((table.shape[1],) + tuple(np.shape(idx)))


def _t5_table(t5_bias):
    far = t5_bias[int(_t5_bucket(-T5_MAX_DIST))]
    return (t5_bias - far[None, :]) * LOG2E


def _attn_a_bias_tiles(t5_tab, t):
    assert t >= T5_MAX_DIST
    k = np.arange(t)[:, None]
    q = np.arange(t)[None, :]
    visible = (k // CHUNK) <= (q // CHUNK)
    idx = np.stack([_t5_bucket(k - q), _t5_bucket(k - t - q)])
    vis = np.stack([visible, np.ones_like(visible)])
    return jnp.where(vis[None], _lookup(t5_tab, idx), NEG)


def _attn_b_bias_tile(rel_tab, tq, win):
    kw = np.arange(win)[:, None] - BAND_REACH
    q = np.arange(tq)[None, :]
    rel = np.clip(kw - q, -REL_CLIP, REL_CLIP) + REL_CLIP
    dchunk = q // CHUNK - np.floor_divide(kw, CHUNK)
    visible = (dchunk >= 0) & (dchunk <= BAND_PREV_CHUNKS)
    return jnp.where(visible[None], _lookup(rel_tab, rel), NEG)


def _diff_lambda(lam_ref, lam_init):
    lv = lam_ref[...]
    s1 = jnp.sum(lv[0:1] * lv[1:2], axis=-1, keepdims=True)
    s2 = jnp.sum(lv[2:3] * lv[3:4], axis=-1, keepdims=True)
    return jnp.exp(s1) - jnp.exp(s2) + lam_init


def _stack_diff_queries(q):
    lane = lax.broadcasted_iota(jnp.int32, q.shape, 1)
    zero = jnp.zeros_like(q)
    return jnp.concatenate([jnp.where(lane < DK_A, q, zero), jnp.where(lane >= DK_A, q, zero)], axis=0)


def _stack_diff_queries_t(q):
    qt = q.astype(F32).T.astype(BF16)
    row = lax.broadcasted_iota(jnp.int32, qt.shape, 0)
    zero = jnp.zeros_like(qt)
    return jnp.concatenate([jnp.where(row < DK_A, qt, zero), jnp.where(row >= DK_A, qt, zero)], axis=1)


def _attn_a_kernel(q_ref, k_ref, v_ref, bias_ref, lam_ref, g_ref, o_ref,
                   vt_scr, s_scr, acc_scr, m_scr, *, t, nh, lam_init):
    qi = pl.program_id(2)
    nkb = vt_scr.shape[1]
    heads = range(nh)

    @pl.when(qi == 0)
    def _():
        sub = lax.broadcasted_iota(jnp.int32, (ONES_ROWS, t), 0)
        ones_rows = jnp.where(sub == 0, 1.0, 0.0).astype(BF16)
        for hh in heads:
            cols = slice(hh * HEAD_W, (hh + 1) * HEAD_W)
            for kb in range(nkb):
                vt_scr[hh, kb, 0:HEAD_W, :] = v_ref[kb * t:(kb + 1) * t, cols].T.astype(BF16)
                vt_scr[hh, kb, HEAD_W:HEAD_W + ONES_ROWS, :] = ones_rows

    q2t = [_stack_diff_queries_t(q_ref[:, hh * HEAD_W:(hh + 1) * HEAD_W]) for hh in heads]
    acc_scr[...] = jnp.zeros(acc_scr.shape, F32)

    def scores(hh, kb):
        start = pl.multiple_of(kb * t, t)
        s_scr[hh] = jnp.dot(k_ref[pl.ds(start, t), hh * HEAD_W:(hh + 1) * HEAD_W], q2t[hh],
                            preferred_element_type=F32)

    def step(kb, ms, bias_idx, prefetch):
        out = []
        for hh in heads:
            s = s_scr[hh]
            if bias_idx is not None:
                bias = bias_ref[hh, bias_idx]
                s = s + jnp.concatenate([bias, bias], axis=1)
            m_new = jnp.maximum(ms[hh], jnp.max(s, axis=0, keepdims=True))
            alpha = jnp.exp2(ms[hh] - m_new)
            e = jnp.exp2(s - m_new).astype(BF16)
            pv = jnp.dot(vt_scr[hh, kb], e, preferred_element_type=F32)
            if prefetch:
                scores(hh, kb + 1)
            acc_scr[hh] = alpha * acc_scr[hh] + pv
            out.append(m_new)
        return tuple(out)

    def load_m():
        return tuple(m_scr[hh] for hh in heads)

    def store_m(ms):
        for hh in heads:
            m_scr[hh] = ms[hh]

    for hh in heads:
        scores(hh, 0)
    m0 = tuple(jnp.full((1, 2 * t), -jnp.inf, F32) for _ in heads)
    n_far = jnp.maximum(qi - 1, 0)
    store_m(lax.fori_loop(
        0, lax.shift_right_logical(n_far, 1),
        lambda p, ms: step(2 * p + 1, step(2 * p, ms, None, True), None, True), m0))

    def finish(steps):
        ms = load_m()
        for kb, bias_idx, prefetch in steps:
            ms = step(kb, ms, bias_idx, prefetch)
        lam = _diff_lambda(lam_ref, lam_init)
        for hh in heads:
            o2 = acc_scr[hh, 0:HEAD_W, :] / acc_scr[hh, HEAD_W:HEAD_W + 1, :]
            o = o2[:, :t] - lam * o2[:, t:]
            o = o * lax.rsqrt(jnp.mean(o * o, axis=0, keepdims=True) + LN_EPS)
            o_ref[:, hh * HEAD_W:(hh + 1) * HEAD_W] = (o.T * g_ref[...] * (1.0 - lam_init)).astype(BF16)

    previous = (qi - 1, 1, True)
    diagonal = (qi, 0, False)
    odd_far = lax.rem(n_far, 2) == 1

    @pl.when(qi == 0)
    def _():
        finish([diagonal])

    @pl.when((qi >= 1) & jnp.logical_not(odd_far))
    def _():
        finish([previous, diagonal])

    @pl.when((qi >= 1) & odd_far)
    def _():
        finish([(n_far - 1, None, True), previous, diagonal])


def _attn_a_call(qa, ka, va_all, bias_tiles, lam_vecs, g, *, layer, batch, seq, lam_init):
    t, nh = TQ_A, HEADS_PER_STEP_A
    nq = seq // t
    w = nh * HEAD_W
    return pl.pallas_call(
        functools.partial(_attn_a_kernel, t=t, nh=nh, lam_init=lam_init),
        out_shape=jax.ShapeDtypeStruct((batch * seq, MIX_W), BF16),
        grid=(batch, N_HEADS // nh, nq),
        in_specs=[
            pl.BlockSpec((t, w), lambda b, h, qi: (b * nq + qi, h)),
            pl.BlockSpec((seq, w), lambda b, h, qi: (b, h)),
            pl.BlockSpec((None, seq, w), lambda b, h, qi: (layer, b, h)),
            pl.BlockSpec((nh, 2, t, t), lambda b, h, qi: (h, 0, 0, 0)),
            pl.BlockSpec((4, DK_A), lambda b, h, qi: (0, 0)),
            pl.BlockSpec((1, HEAD_W), lambda b, h, qi: (0, 0)),
        ],
        out_specs=pl.BlockSpec((t, w), lambda b, h, qi: (b * nq + qi, h)),
        scratch_shapes=[
            pltpu.VMEM((nh, seq // t, HEAD_W + ONES_ROWS, t), BF16),
            pltpu.VMEM((nh, t, 2 * t), F32),
            pltpu.VMEM((nh, HEAD_W + ONES_ROWS, 2 * t), F32),
            pltpu.VMEM((nh, 1, 2 * t), F32),
        ],
        compiler_params=_cparams(("arbitrary", "arbitrary", "arbitrary")),
        name="attn_a",
    )(qa, ka, va_all, bias_tiles, lam_vecs, g)


def _attn_b_kernel(q_ref, k_ref, v_ref, bias_ref, o_ref, k_scr, vt_scr, *, tq, win, seq, units):
    qi = pl.program_id(2)
    pad_blocks = BAND_REACH // tq
    win_blocks = win // tq

    @pl.when(qi == 0)
    def _():
        k_scr[0:BAND_REACH, :] = jnp.zeros((BAND_REACH, HEAD_W), BF16)
        k_scr[BAND_REACH:BAND_REACH + seq, :] = k_ref[...].astype(BF16)
        for blk in range(pad_blocks):
            vt_scr[blk] = jnp.zeros((HEAD_W, tq), BF16)
        for blk in range(seq // tq):
            vt_scr[pad_blocks + blk] = v_ref[blk * tq:(blk + 1) * tq, :].T.astype(BF16)

    def body(mask_early_keys):
        bias = bias_ref[...]
        raw = []
        for u in range(units):
            start = pl.multiple_of((qi * units + u) * tq, tq)
            raw.append(_nt_dot(k_scr[pl.ds(start, win), :], q_ref[u * tq:(u + 1) * tq, :]))
        for u in range(units):
            blk0 = qi * units + u
            s = raw[u] + bias
            if mask_early_keys:
                row = lax.broadcasted_iota(jnp.int32, (win, tq), 0)
                s = jnp.where(row + blk0 * tq >= BAND_REACH, s, NEG)
            m = jnp.max(s, axis=0, keepdims=True)
            e = jnp.exp2(s - m)
            l = jnp.sum(e, axis=0, keepdims=True)
            vt_win = jnp.concatenate([vt_scr[blk0 + w] for w in range(win_blocks)], axis=1)
            acc = jnp.dot(vt_win, e.astype(BF16), preferred_element_type=F32)
            o_ref[u * tq:(u + 1) * tq, :] = (acc / l).T.astype(BF16)

    early_steps = -(-BAND_REACH // (units * tq))

    @pl.when(qi < early_steps)
    def _():
        body(True)

    @pl.when(qi >= early_steps)
    def _():
        body(False)


def _attn_b_call(qb, kb, vb, bias_tile, *, batch, seq):
    tq, win, units = TQ_B, WIN_B, UNITS_B
    rows = tq * units
    nq = seq // rows
    return pl.pallas_call(
        functools.partial(_attn_b_kernel, tq=tq, win=win, seq=seq, units=units),
        out_shape=jax.ShapeDtypeStruct((batch * seq, MIX_W), BF16),
        grid=(batch, N_HEADS, nq),
        in_specs=[
            pl.BlockSpec((rows, HEAD_W), lambda b, h, qi: (b * nq + qi, h)),
            pl.BlockSpec((seq, HEAD_W), lambda b, h, qi: (b, h)),
            pl.BlockSpec((seq, HEAD_W), lambda b, h, qi: (b, h)),
            pl.BlockSpec((None, win, tq), lambda b, h, qi: (h, 0, 0)),
        ],
        out_specs=pl.BlockSpec((rows, HEAD_W), lambda b, h, qi: (b * nq + qi, h)),
        scratch_shapes=[
            pltpu.VMEM((BAND_REACH + seq, HEAD_W), BF16),
            pltpu.VMEM(((BAND_REACH + seq) // tq, HEAD_W, tq), BF16),
        ],
        compiler_params=_cparams(("arbitrary", "arbitrary", "arbitrary")),
        name="attn_b",
    )(qb, kb, vb, bias_tile)


def _softmax_pv(parts):
    m = None
    for s, _ in parts:
        pm = jnp.max(s, axis=-1, keepdims=True)
        m = pm if m is None else jnp.maximum(m, pm)
    l = None
    o = None
    for s, v in parts:
        e = jnp.exp2(s - m)
        pl_ = jnp.sum(e, axis=-1, keepdims=True)
        po = jnp.dot(e.astype(BF16), v, preferred_element_type=F32)
        l = pl_ if l is None else l + pl_
        o = po if o is None else o + po
    return o / l


def _sample_kernel(qa_ref, ka_ref, va_ref, cakt_ref, cav_ref, qb_ref, kb_ref, vb_ref, cbk_ref, cbv_ref,
                   ba_c_ref, ba_n_ref, bb_c_ref, bb_n_ref, lam_ref, g_ref, oa_ref, ob_ref,
                   *, t_new, lam_init):
    lam = _diff_lambda(lam_ref, lam_init)
    g = g_ref[...]
    past = cakt_ref.shape[-1]

    def head_rows(ref, h):
        return ref[pl.ds(h, ref.shape[0] // N_HEADS, stride=N_HEADS), :].astype(BF16)

    for h in range(N_HEADS):
        cols = slice(h * HEAD_W, (h + 1) * HEAD_W)
        q2 = _stack_diff_queries(qa_ref[:, cols])
        kct = cakt_ref[h].reshape(2 * DK_A, past).astype(BF16)
        s_c = jnp.dot(q2, kct, preferred_element_type=F32)
        s_n = _nt_dot(q2, ka_ref[:, cols].astype(BF16))
        s_c = (s_c.reshape(2, t_new, -1) + ba_c_ref[h][None]).reshape(2 * t_new, -1)
        s_n = (s_n.reshape(2, t_new, -1) + ba_n_ref[h][None]).reshape(2 * t_new, -1)
        o2 = _softmax_pv([(s_c, head_rows(cav_ref, h)), (s_n, va_ref[:, cols].astype(BF16))])
        o = o2[:t_new] - lam * o2[t_new:]
        o = o * lax.rsqrt(jnp.mean(o * o, axis=-1, keepdims=True) + LN_EPS) * g
        oa_ref[:, cols] = (o * (1.0 - lam_init)).astype(BF16)
        qh = qb_ref[:, cols]
        s_c = _nt_dot(qh, head_rows(cbk_ref, h)) + bb_c_ref[h]
        s_n = _nt_dot(qh, kb_ref[:, cols].astype(BF16)) + bb_n_ref[h]
        ob_ref[:, cols] = _softmax_pv([(s_c, head_rows(cbv_ref, h)),
                                       (s_n, vb_ref[:, cols].astype(BF16))]).astype(BF16)


def _sample_call(layer, qa, ka_all, va_all, cache_a_kt, cache_a_v, qb, kb, vb, cache_b_k, cache_b_v,
                 ba_c, ba_n, bb_c, bb_n, lam_vecs, g, *, batch, t_new, lam_init):
    past = cache_a_kt.shape[-1]
    band_past = cache_b_k.shape[2] // N_HEADS
    row = lambda b: (b, 0)
    slab = lambda b: (layer, b, 0)
    cache = lambda b: (layer, b, 0, 0)
    cache_t = lambda b: (layer, b, 0, 0, 0, 0)
    const3 = lambda b: (0, 0, 0)
    const2 = lambda b: (0, 0)
    return pl.pallas_call(
        functools.partial(_sample_kernel, t_new=t_new, lam_init=lam_init),
        out_shape=(jax.ShapeDtypeStruct((batch * t_new, MIX_W), BF16),
                   jax.ShapeDtypeStruct((batch * t_new, MIX_W), BF16)),
        grid=(batch,),
        in_specs=[
            pl.BlockSpec((t_new, MIX_W), row),
            pl.BlockSpec((None, t_new, MIX_W), slab),
            pl.BlockSpec((None, t_new, MIX_W), slab),
            pl.BlockSpec((None, None, N_HEADS, 2, DK_A, past), cache_t),
            pl.BlockSpec((None, None, past * N_HEADS, HEAD_W), cache),
            pl.BlockSpec((t_new, MIX_W), row),
            pl.BlockSpec((t_new, MIX_W), row),
            pl.BlockSpec((t_new, MIX_W), row),
            pl.BlockSpec((None, None, band_past * N_HEADS, HEAD_W), cache),
            pl.BlockSpec((None, None, band_past * N_HEADS, HEAD_W), cache),
            pl.BlockSpec(ba_c.shape, const3),
            pl.BlockSpec(ba_n.shape, const3),
            pl.BlockSpec(bb_c.shape, const3),
            pl.BlockSpec(bb_n.shape, const3),
            pl.BlockSpec((4, DK_A), const2),
            pl.BlockSpec((1, HEAD_W), const2),
        ],
        out_specs=(pl.BlockSpec((t_new, MIX_W), row), pl.BlockSpec((t_new, MIX_W), row)),
        compiler_params=_cparams(("arbitrary",)),
        name="sample_mix",
    )(qa, ka_all, va_all, cache_a_kt, cache_a_v, qb, kb, vb, cache_b_k, cache_b_v,
      ba_c, ba_n, bb_c, bb_n, lam_vecs, g)


def _mixout_kernel(oa_ref, ob_ref, sga_ref, sgb_ref, x_ref, gm_ref, woa_ref, wob_ref, wout_ref,
                   lg_ref, lb_ref, o_ref, *, gpt, alpha):
    a = jnp.dot(oa_ref[...], woa_ref[...], preferred_element_type=F32)
    b = jnp.dot(ob_ref[...], wob_ref[...], preferred_element_type=F32)
    merged = sga_ref[...].astype(F32) * a + sgb_ref[...].astype(F32) * b
    y = jnp.dot(merged.astype(BF16), wout_ref[...], preferred_element_type=F32)
    z = alpha * x_ref[...] + _gate(y, gm_ref[...], gpt)
    o_ref[...] = _ln(z) * lg_ref[...] + lb_ref[...]


def _mixout_call(oa, ob, sga, sgb, x, mod_l, w_oa, w_ob, w_out, ln_g, ln_b,
                 *, rows_per_group, group0, bm, alpha):
    m, d = x.shape
    gpt = max(1, bm // rows_per_group)
    row = lambda i: (i, 0)
    const = lambda i: (0, 0)
    resident = pl.Buffered(1)
    return pl.pallas_call(
        functools.partial(_mixout_kernel, gpt=gpt, alpha=alpha),
        out_shape=jax.ShapeDtypeStruct((m, d), F32),
        grid=(m // bm,),
        in_specs=[
            pl.BlockSpec((bm, MIX_W), row),
            pl.BlockSpec((bm, MIX_W), row),
            pl.BlockSpec((bm, d), row),
            pl.BlockSpec((bm, d), row),
            pl.BlockSpec((bm, d), row),
            _mod_spec(gpt, rows_per_group, bm, group0, 2, d),
            pl.BlockSpec((MIX_W, d), const, pipeline_mode=resident),
            pl.BlockSpec((MIX_W, d), const, pipeline_mode=resident),
            pl.BlockSpec((d, d), const, pipeline_mode=resident),
            pl.BlockSpec((1, d), const),
            pl.BlockSpec((1, d), const),
        ],
        out_specs=pl.BlockSpec((bm, d), row),
        compiler_params=_cparams(("arbitrary",)),
        name="mixout",
    )(oa, ob, sga, sgb, x, mod_l, w_oa, w_ob, w_out, ln_g, ln_b)


def _ffn_kernel(x_ref, sc_ref, sh_ref, gf_ref, w1_ref, w3_ref, w2_ref, lg_ref, lb_ref, o_ref,
                h_scr, *, gpt, alpha):
    kf = pl.program_id(1)
    bm = h_scr.shape[0]
    chunk = min(bm, LN_CHUNK_ROWS)
    gpc = max(1, gpt * chunk // bm)

    def chunks():
        for r in range(bm // chunk):
            g0 = r * gpc if gpt > 1 else 0
            yield slice(r * chunk, (r + 1) * chunk), slice(g0, g0 + gpc)

    @pl.when(kf == 0)
    def _():
        for rows, grp in chunks():
            h = _modulate(_ln(x_ref[rows, :]), sc_ref[grp], sh_ref[grp], gpc)
            h_scr[rows, :] = h.astype(BF16)
            o_ref[rows, :] = jnp.zeros((chunk, o_ref.shape[1]), F32)

    sub = min(bm, FFN_SUB_ROWS)
    ab = []
    for r in range(bm // sub):
        h = h_scr[r * sub:(r + 1) * sub, :]
        ab.append((jnp.dot(h, w1_ref[...], preferred_element_type=F32),
                   jnp.dot(h, w3_ref[...], preferred_element_type=F32)))
    for r, (a, b) in enumerate(ab):
        u = (a * _sigmoid(a)) * b
        o_ref[r * sub:(r + 1) * sub, :] += jnp.dot(u.astype(BF16), w2_ref[...], preferred_element_type=F32)

    @pl.when(kf == pl.num_programs(1) - 1)
    def _():
        for rows, grp in chunks():
            z = alpha * x_ref[rows, :] + _gate(o_ref[rows, :], gf_ref[grp], gpc)
            o_ref[rows, :] = _ln(z) * lg_ref[...] + lb_ref[...]


def _ffn_call(x, mod_l, w1, w3, w2, ln_g, ln_b, *, rows_per_group, group0, bm, tf, alpha):
    m, d = x.shape
    dff = w1.shape[1]
    gpt = max(1, bm // rows_per_group)
    return pl.pallas_call(
        functools.partial(_ffn_kernel, gpt=gpt, alpha=alpha),
        out_shape=jax.ShapeDtypeStruct((m, d), F32),
        grid=(m // bm, dff // tf),
        in_specs=[
            pl.BlockSpec((bm, d), lambda i, k: (i, 0)),
            _mod_spec(gpt, rows_per_group, bm, group0, 4, d),
            _mod_spec(gpt, rows_per_group, bm, group0, 3, d),
            _mod_spec(gpt, rows_per_group, bm, group0, 5, d),
            pl.BlockSpec((d, tf), lambda i, k: (0, k)),
            pl.BlockSpec((d, tf), lambda i, k: (0, k)),
            pl.BlockSpec((tf, d), lambda i, k: (k, 0)),
            pl.BlockSpec((1, d), lambda i, k: (0, 0)),
            pl.BlockSpec((1, d), lambda i, k: (0, 0)),
        ],
        out_specs=pl.BlockSpec((bm, d), lambda i, k: (i, 0)),
        scratch_shapes=[pltpu.VMEM((bm, d), BF16)],
        compiler_params=_cparams(("arbitrary", "arbitrary")),
        name="ffn",
    )(x, mod_l, mod_l, mod_l, w1, w3, w2, ln_g, ln_b)


def _ff_tile(dff):
    for tf in (512, 256, 128):
        if dff % tf == 0:
            return tf
    raise ValueError(f"unsupported FFN width {dff}")


def kernel(x_prompt, x_sample, cache_a_k, cache_a_v, cache_b_k, cache_b_v, c_prompt, c_sample,
           w_mod, b_mod, w_in, lambda_q1, lambda_k1, lambda_q2, lambda_k2, subln_g, t5_bias,
           rel_bias, w_oa, w_ob, w_out, ln1_g, ln1_b, w1, w3, w2, ln2_g, ln2_b):
    depth = w_mod.shape[0]
    batch, seq, d = x_prompt.shape
    dec_batch, t_new, _ = x_sample.shape
    past = cache_a_k.shape[2]
    band_past = cache_b_k.shape[2]
    assert past % CHUNK == 0 and t_new <= CHUNK and band_past == BAND_REACH and past >= BAND_REACH
    assert seq % TQ_A == 0 and seq % (TQ_B * UNITS_B) == 0 and seq >= BAND_REACH and d % 128 == 0
    alpha = (2 * depth) ** 0.25
    prompt_band = min(BAND_REACH, seq)

    bm_p = min(512, seq)
    bm_s = min(512, dec_batch * t_new)
    bm_in = min(1024, seq)
    tn = math.gcd(MIX_W // 2, d)
    tf = _ff_tile(w1.shape[2])

    c_all = jnp.concatenate([c_sample, c_prompt], axis=0)
    mod = _mod_call(c_all, w_mod, b_mod).reshape(depth, dec_batch + batch, 1, 6 * d)

    t5_tab = _t5_table(t5_bias)
    ta_tiles = _attn_a_bias_tiles(t5_tab, TQ_A)
    q_pos_s = past + np.arange(t_new)
    rel_a = np.arange(past + t_new)[None, :] - q_pos_s[:, None]
    ba_s = _lookup(t5_tab, _t5_bucket(rel_a))
    kb_pos_s = past - band_past + np.arange(band_past + t_new)
    rel_b = np.clip(kb_pos_s[None, :] - q_pos_s[:, None], -REL_CLIP, REL_CLIP) + REL_CLIP

    cakt = jnp.transpose(cache_a_k, (0, 1, 3, 4, 5, 2))
    cav = cache_a_v.reshape(depth, dec_batch, past * N_HEADS, HEAD_W)
    cbk = cache_b_k.reshape(depth, dec_batch, band_past * N_HEADS, HEAD_W)
    cbv = cache_b_v.reshape(depth, dec_batch, band_past * N_HEADS, HEAD_W)

    xp = x_prompt.reshape(batch * seq, d)
    xs = x_sample.reshape(dec_batch * t_new, d)
    kv_p, kv_s = (), ()
    bk_p, bv_p, bk_s, bv_s = [], [], [], []
    for l in range(depth):
        lam_init = 0.8 - 0.6 * math.exp(-0.3 * l)
        lam_vecs = jnp.stack([lambda_q1[l], lambda_k1[l], lambda_q2[l], lambda_k2[l]]).astype(F32)
        g = subln_g[l].reshape(1, HEAD_W).astype(F32)
        mod_l = mod[l:l + 1]
        w_in_l = w_in[l].astype(BF16)
        w_oa_l, w_ob_l, w_out_l = w_oa[l].astype(BF16), w_ob[l].astype(BF16), w_out[l].astype(BF16)
        w1_l, w3_l, w2_l = w1[l].astype(BF16), w3[l].astype(BF16), w2[l].astype(BF16)
        ln1 = (ln1_g[l].reshape(1, d), ln1_b[l].reshape(1, d))
        ln2 = (ln2_g[l].reshape(1, d), ln2_b[l].reshape(1, d))
        rel_tab = rel_bias[l] * LOG2E
        bb_tile = _attn_b_bias_tile(rel_tab, TQ_B, WIN_B)
        bb_s = _lookup(rel_tab, rel_b)

        grp = dict(rows_per_group=seq, group0=dec_batch, bm=bm_p)
        p = _inproj_call(xp, mod_l, w_in_l, kv_p, layer=l, depth=depth, tn=tn, key_major=True,
                         **dict(grp, bm=bm_in))
        kv_p = (p["ka_all"], p["va_all"])
        oa = _attn_a_call(p["qa"], p["ka"], p["va_all"], ta_tiles, lam_vecs, g,
                          layer=l, batch=batch, seq=seq, lam_init=lam_init)
        ob = _attn_b_call(p["qb"], p["kb"], p["vb"], bb_tile, batch=batch, seq=seq)
        xp = _mixout_call(oa, ob, p["sga"], p["sgb"], xp, mod_l, w_oa_l, w_ob_l, w_out_l, *ln1,
                          alpha=alpha, **grp)
        xp = _ffn_call(xp, mod_l, w1_l, w3_l, w2_l, *ln2, tf=tf, alpha=alpha, **dict(grp, bm=bm_in))
        bk_p.append(p["kb"].reshape(batch, seq, N_HEADS, HEAD_W)[:, seq - prompt_band:])
        bv_p.append(p["vb"].reshape(batch, seq, N_HEADS, HEAD_W)[:, seq - prompt_band:])

        grp = dict(rows_per_group=t_new, group0=0, bm=bm_s)
        p = _inproj_call(xs, mod_l, w_in_l, kv_s, layer=l, depth=depth, tn=tn, key_major=False, **grp)
        kv_s = (p["ka_all"], p["va_all"])
        oa, ob = _sample_call(l, p["qa"], p["ka_all"], p["va_all"], cakt, cav, p["qb"], p["kb"],
                              p["vb"], cbk, cbv,
                              ba_s[:, :, :past], ba_s[:, :, past:], bb_s[:, :, :band_past],
                              bb_s[:, :, band_past:], lam_vecs, g,
                              batch=dec_batch, t_new=t_new, lam_init=lam_init)
        xs = _mixout_call(oa, ob, p["sga"], p["sgb"], xs, mod_l, w_oa_l, w_ob_l, w_out_l, *ln1,
                          alpha=alpha, **grp)
        xs = _ffn_call(xs, mod_l, w1_l, w3_l, w2_l, *ln2, tf=tf, alpha=alpha, **grp)
        bk_s.append(p["kb"].reshape(dec_batch, t_new, N_HEADS, HEAD_W))
        bv_s.append(p["vb"].reshape(dec_batch, t_new, N_HEADS, HEAD_W))

    a_k_prompt = jnp.transpose(kv_p[0].reshape(depth, batch, N_HEADS, 2, DK_A, seq), (0, 1, 5, 2, 3, 4))
    return (xp.reshape(batch, seq, d), xs.reshape(dec_batch, t_new, d),
            a_k_prompt,
            kv_p[1].reshape(depth, batch, seq, N_HEADS, HEAD_W),
            jnp.stack(bk_p), jnp.stack(bv_p),
            kv_s[0].reshape(depth, dec_batch, t_new, N_HEADS, 2, DK_A),
            kv_s[1].reshape(depth, dec_batch, t_new, N_HEADS, HEAD_W),
            jnp.stack(bk_s), jnp.stack(bv_s))
```

```python
import functools
import math

import jax
import jax.numpy as jnp
import numpy as np
from jax import lax
from jax.experimental import pallas as pl
from jax.experimental.pallas import tpu as pltpu

F32 = jnp.float32
BF16 = jnp.bfloat16

CHUNK = 64
N_HEADS = 8
HEAD_W = 128
DK_A = 64
BAND_PREV_CHUNKS = 8
BAND_REACH = BAND_PREV_CHUNKS * CHUNK
REL_CLIP = 128
T5_BUCKETS = 32
T5_MAX_DIST = 128
NEG = -1e30
LN_EPS = 1e-5
MIX_W = N_HEADS * HEAD_W
LOG2E = math.log2(math.e)
QA_SCALE = DK_A ** -0.5 * LOG2E
QB_SCALE = HEAD_W ** -0.5 * LOG2E

V7X_VMEM_BYTES = 64 * 1024 * 1024
VMEM_LIMIT = V7X_VMEM_BYTES - 3 * 1024 * 1024

TQ_A = 256
HEADS_PER_STEP_A = 4
FFN_SUB_ROWS = 512
LN_CHUNK_ROWS = 256
CAST_BLOCK_ELEMS = 1 << 20
ONES_ROWS = 16
TQ_B = 128
WIN_B = BAND_REACH + TQ_B
UNITS_B = 8


def _cparams(sem):
    return pltpu.CompilerParams(dimension_semantics=sem, vmem_limit_bytes=VMEM_LIMIT)


def _sigmoid(x):
    return 1.0 / (1.0 + jnp.exp(-x))


def _ln(x):
    mu = jnp.mean(x, axis=-1, keepdims=True)
    xc = x - mu
    var = jnp.mean(xc * xc, axis=-1, keepdims=True)
    return xc * lax.rsqrt(var + LN_EPS)


def _modulate(y, scale, shift, gpt):
    if gpt == 1:
        return y * (1.0 + scale[0]) + shift[0]
    bm, d = y.shape
    y3 = y.reshape(gpt, bm // gpt, d)
    return (y3 * (1.0 + scale) + shift).reshape(bm, d)


def _gate(y, gate, gpt):
    if gpt == 1:
        return y * gate[0]
    bm, d = y.shape
    return (y.reshape(gpt, bm // gpt, d) * gate).reshape(bm, d)


def _nt_dot(a, b):
    return lax.dot_general(a, b, (((1,), (1,)), ((), ())), preferred_element_type=F32)


def _mod_kernel(c_ref, w_ref, b_ref, o_ref):
    c = c_ref[...]
    s = (c * _sigmoid(c)).astype(BF16)
    o_ref[...] = jnp.dot(s, w_ref[...].astype(BF16), preferred_element_type=F32) + b_ref[...]


def _mod_call(c_all, w_mod, b_mod):
    depth, d, n = w_mod.shape
    g = c_all.shape[0]
    tn = math.gcd(n, 1024)
    return pl.pallas_call(
        _mod_kernel,
        out_shape=jax.ShapeDtypeStruct((depth, g, n), F32),
        grid=(depth, n // tn),
        in_specs=[
            pl.BlockSpec((g, d), lambda l, j: (0, 0)),
            pl.BlockSpec((None, d, tn), lambda l, j: (l, 0, j)),
            pl.BlockSpec((None, 1, tn), lambda l, j: (l, 0, j)),
        ],
        out_specs=pl.BlockSpec((None, g, tn), lambda l, j: (l, 0, j)),
        compiler_params=_cparams(("arbitrary", "arbitrary")),
        name="mod",
    )(c_all, w_mod, b_mod.reshape(depth, 1, n))


def _cast_kernel(w_ref, o_ref):
    o_ref[...] = w_ref[...].astype(BF16)


def _to_bf16(w):
    depth, r, c = w.shape
    br = r
    while br % 2 == 0 and br > 16 and br * c > CAST_BLOCK_ELEMS:
        br //= 2
    return pl.pallas_call(
        _cast_kernel,
        out_shape=jax.ShapeDtypeStruct(w.shape, BF16),
        grid=(depth, r // br),
        in_specs=[pl.BlockSpec((None, br, c), lambda l, i: (l, i, 0))],
        out_specs=pl.BlockSpec((None, br, c), lambda l, i: (l, i, 0)),
        compiler_params=_cparams(("arbitrary", "arbitrary")),
        name="to_bf16",
    )(w)


def _inproj_segments(d, tn):
    widths = (MIX_W,) * 6 + (d, d)
    segs, lo = [], 0
    for w in widths:
        assert w % tn == 0
        segs.append((lo, w // tn))
        lo += w // tn
    return tuple(segs)


def _inproj_write(ref, kind, acc):
    if kind == "f32":
        ref[...] = acc
    elif kind == "bf16":
        ref[...] = acc.astype(BF16)
    elif kind == "qa":
        ref[...] = (acc * QA_SCALE).astype(BF16)
    elif kind == "qb":
        ref[...] = (acc * QB_SCALE).astype(BF16)
    elif kind == "gate":
        ref[...] = _sigmoid(acc).astype(BF16)
    elif kind == "key_major":
        for h in range(ref.shape[0]):
            ref[h] = acc[:, h * HEAD_W:(h + 1) * HEAD_W].T
    else:
        raise ValueError(kind)


def _inproj_kernel(*refs, gpt, segs, plan, n_in, n_out):
    x_ref, sc_ref, sh_ref, w_ref = refs[:4]
    out_refs = refs[n_in:n_in + n_out]
    h_scr = refs[n_in + n_out]
    j = pl.program_id(1)

    @pl.when(j == 0)
    def _():
        h = _modulate(_ln(x_ref[...]), sc_ref[...], sh_ref[...], gpt)
        h_scr[...] = h.astype(BF16)

    for (lo, cnt), writers in zip(segs, plan):
        @pl.when((j >= lo) & (j < lo + cnt))
        def _(writers=writers):
            acc = jnp.dot(h_scr[...], w_ref[...], preferred_element_type=F32)
            for idx, kind in writers:
                _inproj_write(out_refs[idx], kind, acc)


def _mod_spec(gpt, rows_per_group, bm, group0, which, d):
    if gpt == 1:
        tiles_per_group = rows_per_group // bm
        return pl.BlockSpec((None, 1, 1, d), lambda i, *_: (0, group0 + i // tiles_per_group, 0, which))
    return pl.BlockSpec((None, gpt, 1, d), lambda i, *_: (0, group0 // gpt + i, 0, which))


def _inproj_call(x, mod_l, w_in, kv_carry, *, layer, depth, rows_per_group, group0, bm, tn, key_major):
    m, d = x.shape
    n = w_in.shape[2]
    gpt = max(1, bm // rows_per_group)
    segs = _inproj_segments(d, tn)
    assert n == tn * (segs[-1][0] + segs[-1][1])
    n_carry = len(kv_carry)

    def col(seg):
        lo, cnt = seg
        return lambda j: jnp.clip(j - lo, 0, cnt - 1)

    def rows(seg, width, dtype):
        c = col(seg)
        return (jax.ShapeDtypeStruct((m, width), dtype), pl.BlockSpec((bm, tn), lambda i, j: (i, c(j))))

    def slab(seg):
        c = col(seg)
        return (jax.ShapeDtypeStruct((depth, m, MIX_W), F32),
                pl.BlockSpec((None, bm, tn), lambda i, j: (layer, i, c(j))))

    def key_major_slab(seg):
        c = col(seg)
        tiles_per_seq = rows_per_group // bm
        return (jax.ShapeDtypeStruct((depth, m // rows_per_group, N_HEADS, HEAD_W, rows_per_group), F32),
                pl.BlockSpec((None, None, tn // HEAD_W, HEAD_W, bm),
                             lambda i, j: (layer, i // tiles_per_seq, c(j), 0, i % tiles_per_seq)))

    outs = {"qa": rows(segs[0], MIX_W, BF16)}
    if key_major:
        assert rows_per_group % bm == 0
        outs["ka"] = rows(segs[1], MIX_W, BF16)
        outs["ka_all"] = key_major_slab(segs[1])
        ka_writers = (("ka", "bf16"), ("ka_all", "key_major"))
    else:
        outs["ka_all"] = slab(segs[1])
        ka_writers = (("ka_all", "f32"),)
    outs["va_all"] = slab(segs[2])
    outs["qb"] = rows(segs[3], MIX_W, BF16)
    outs["kb"] = rows(segs[4], MIX_W, F32)
    outs["vb"] = rows(segs[5], MIX_W, F32)
    outs["sga"] = rows(segs[6], d, BF16)
    outs["sgb"] = rows(segs[7], d, BF16)
    names = list(outs)
    plan = tuple(tuple((names.index(nm), kind) for nm, kind in writers) for writers in (
        (("qa", "qa"),), ka_writers, (("va_all", "f32"),), (("qb", "qb"),),
        (("kb", "f32"),), (("vb", "f32"),), (("sga", "gate"),), (("sgb", "gate"),)))
    n_in = 4 + n_carry
    aliases = {4 + c: names.index(nm) for c, nm in enumerate(("ka_all", "va_all")[:n_carry])}
    res = pl.pallas_call(
        functools.partial(_inproj_kernel, gpt=gpt, segs=segs, plan=plan, n_in=n_in, n_out=len(names)),
        out_shape=tuple(outs[nm][0] for nm in names),
        grid=(m // bm, n // tn),
        in_specs=[
            pl.BlockSpec((bm, d), lambda i, j: (i, 0)),
            _mod_spec(gpt, rows_per_group, bm, group0, 1, d),
            _mod_spec(gpt, rows_per_group, bm, group0, 0, d),
            pl.BlockSpec((None, d, tn), lambda i, j: (layer, 0, j)),
        ] + [pl.BlockSpec(memory_space=pl.ANY)] * n_carry,
        out_specs=tuple(outs[nm][1] for nm in names),
        scratch_shapes=[pltpu.VMEM((bm, d), BF16)],
        input_output_aliases=aliases,
        compiler_params=_cparams(("arbitrary", "arbitrary")),
        name="inproj",
    )(x, mod_l, mod_l, w_in, *kv_carry)
    return dict(zip(names, res))


def _t5_bucket(rel):
    nb = T5_BUCKETS // 2
    max_exact = nb // 2
    rel = np.asarray(rel, np.int32)
    bucket = (rel > 0).astype(np.int32) * nb
    n = np.abs(rel)
    nf = np.maximum(n, 1).astype(np.float32)
    ratio = np.log(nf / np.float32(max_exact)) / np.float32(math.log(T5_MAX_DIST / max_exact))
    large = max_exact + (ratio * np.float32(nb - max_exact)).astype(np.int32)
    large = np.minimum(large, nb - 1)
    return bucket + np.where(n < max_exact, n, large)


def _lookup(table, idx):
    n = table.shape[0]
    flat = jnp.asarray(np.asarray(idx, np.int32).reshape(-1))
    onehot = (flat[None, :] == jnp.arange(n, dtype=jnp.int32)[:, None]).astype(F32)
    out = lax.dot_general(table.astype(F32), onehot, (((0,), (0,)), ((), ())),
                          precision=lax.Precision.HIGHEST, preferred_element_type=F32)
    return out.reshape((table.shape[1],) + tuple(np.shape(idx)))


def _t5_table(t5_bias):
    far = t5_bias[int(_t5_bucket(-T5_MAX_DIST))]
    return (t5_bias - far[None, :]) * LOG2E


def _attn_a_bias_tiles(t5_tab, t):
    assert t >= T5_MAX_DIST
    k = np.arange(t)[:, None]
    q = np.arange(t)[None, :]
    visible = (k // CHUNK) <= (q // CHUNK)
    idx = np.stack([_t5_bucket(k - q), _t5_bucket(k - t - q)])
    vis = np.stack([visible, np.ones_like(visible)])
    return jnp.where(vis[None], _lookup(t5_tab, idx), NEG)


def _attn_b_bias_tile(rel_tab, tq, win):
    kw = np.arange(win)[:, None] - BAND_REACH
    q = np.arange(tq)[None, :]
    rel = np.clip(kw - q, -REL_CLIP, REL_CLIP) + REL_CLIP
    dchunk = q // CHUNK - np.floor_divide(kw, CHUNK)
    visible = (dchunk >= 0) & (dchunk <= BAND_PREV_CHUNKS)
    return jnp.where(visible[None], _lookup(rel_tab, rel), NEG)


def _diff_lambda(lam_ref, lam_init):
    lv = lam_ref[...]
    s1 = jnp.sum(lv[0:1] * lv[1:2], axis=-1, keepdims=True)
    s2 = jnp.sum(lv[2:3] * lv[3:4], axis=-1, keepdims=True)
    return jnp.exp(s1) - jnp.exp(s2) + lam_init


def _stack_diff_queries(q):
    lane = lax.broadcasted_iota(jnp.int32, q.shape, 1)
    zero = jnp.zeros_like(q)
    return jnp.concatenate([jnp.where(lane < DK_A, q, zero), jnp.where(lane >= DK_A, q, zero)], axis=0)


def _stack_diff_queries_t(q):
    qt = q.astype(F32).T.astype(BF16)
    row = lax.broadcasted_iota(jnp.int32, qt.shape, 0)
    zero = jnp.zeros_like(qt)
    return jnp.concatenate([jnp.where(row < DK_A, qt, zero), jnp.where(row >= DK_A, qt, zero)], axis=1)


def _attn_a_kernel(q_ref, k_ref, v_ref, bias_ref, lam_ref, g_ref, o_ref,
                   vt_scr, s_scr, acc_scr, m_scr, *, t, nh, lam_init):
    qi = pl.program_id(2)
    nkb = vt_scr.shape[1]
    heads = range(nh)

    @pl.when(qi == 0)
    def _():
        sub = lax.broadcasted_iota(jnp.int32, (ONES_ROWS, t), 0)
        ones_rows = jnp.where(sub == 0, 1.0, 0.0).astype(BF16)
        for hh in heads:
            cols = slice(hh * HEAD_W, (hh + 1) * HEAD_W)
            for kb in range(nkb):
                vt_scr[hh, kb, 0:HEAD_W, :] = v_ref[kb * t:(kb + 1) * t, cols].T.astype(BF16)
                vt_scr[hh, kb, HEAD_W:HEAD_W + ONES_ROWS, :] = ones_rows

    q2t = [_stack_diff_queries_t(q_ref[:, hh * HEAD_W:(hh + 1) * HEAD_W]) for hh in heads]
    acc_scr[...] = jnp.zeros(acc_scr.shape, F32)

    def scores(hh, kb):
        start = pl.multiple_of(kb * t, t)
        s_scr[hh] = jnp.dot(k_ref[pl.ds(start, t), hh * HEAD_W:(hh + 1) * HEAD_W], q2t[hh],
                            preferred_element_type=F32)

    def step(kb, ms, bias_idx, prefetch):
        out = []
        for hh in heads:
            s = s_scr[hh]
            if bias_idx is not None:
                bias = bias_ref[hh, bias_idx]
                s = s + jnp.concatenate([bias, bias], axis=1)
            m_new = jnp.maximum(ms[hh], jnp.max(s, axis=0, keepdims=True))
            alpha = jnp.exp2(ms[hh] - m_new)
            e = jnp.exp2(s - m_new).astype(BF16)
            pv = jnp.dot(vt_scr[hh, kb], e, preferred_element_type=F32)
            if prefetch:
                scores(hh, kb + 1)
            acc_scr[hh] = alpha * acc_scr[hh] + pv
            out.append(m_new)
        return tuple(out)

    def load_m():
        return tuple(m_scr[hh] for hh in heads)

    def store_m(ms):
        for hh in heads:
            m_scr[hh] = ms[hh]

    for hh in heads:
        scores(hh, 0)
    m0 = tuple(jnp.full((1, 2 * t), -jnp.inf, F32) for _ in heads)
    n_far = jnp.maximum(qi - 1, 0)
    store_m(lax.fori_loop(
        0, lax.shift_right_logical(n_far, 1),
        lambda p, ms: step(2 * p + 1, step(2 * p, ms, None, True), None, True), m0))

    def finish(steps):
        ms = load_m()
        for kb, bias_idx, prefetch in steps:
            ms = step(kb, ms, bias_idx, prefetch)
        lam = _diff_lambda(lam_ref, lam_init)
        for hh in heads:
            o2 = acc_scr[hh, 0:HEAD_W, :] / acc_scr[hh, HEAD_W:HEAD_W + 1, :]
            o = o2[:, :t] - lam * o2[:, t:]
            o = o * lax.rsqrt(jnp.mean(o * o, axis=0, keepdims=True) + LN_EPS)
            o_ref[:, hh * HEAD_W:(hh + 1) * HEAD_W] = (o.T * g_ref[...] * (1.0 - lam_init)).astype(BF16)

    previous = (qi - 1, 1, True)
    diagonal = (qi, 0, False)
    odd_far = lax.rem(n_far, 2) == 1

    @pl.when(qi == 0)
    def _():
        finish([diagonal])

    @pl.when((qi >= 1) & jnp.logical_not(odd_far))
    def _():
        finish([previous, diagonal])

    @pl.when((qi >= 1) & odd_far)
    def _():
        finish([(n_far - 1, None, True), previous, diagonal])


def _attn_a_call(qa, ka, va_all, bias_tiles, lam_vecs, g, *, layer, batch, seq, lam_init):
    t, nh = TQ_A, HEADS_PER_STEP_A
    nq = seq // t
    w = nh * HEAD_W
    return pl.pallas_call(
        functools.partial(_attn_a_kernel, t=t, nh=nh, lam_init=lam_init),
        out_shape=jax.ShapeDtypeStruct((batch * seq, MIX_W), BF16),
        grid=(batch, N_HEADS // nh, nq),
        in_specs=[
            pl.BlockSpec((t, w), lambda b, h, qi: (b * nq + qi, h)),
            pl.BlockSpec((seq, w), lambda b, h, qi: (b, h)),
            pl.BlockSpec((None, seq, w), lambda b, h, qi: (layer, b, h)),
            pl.BlockSpec((nh, 2, t, t), lambda b, h, qi: (h, 0, 0, 0)),
            pl.BlockSpec((4, DK_A), lambda b, h, qi: (0, 0)),
            pl.BlockSpec((1, HEAD_W), lambda b, h, qi: (0, 0)),
        ],
        out_specs=pl.BlockSpec((t, w), lambda b, h, qi: (b * nq + qi, h)),
        scratch_shapes=[
            pltpu.VMEM((nh, seq // t, HEAD_W + ONES_ROWS, t), BF16),
            pltpu.VMEM((nh, t, 2 * t), F32),
            pltpu.VMEM((nh, HEAD_W + ONES_ROWS, 2 * t), F32),
            pltpu.VMEM((nh, 1, 2 * t), F32),
        ],
        compiler_params=_cparams(("arbitrary", "arbitrary", "arbitrary")),
        name="attn_a",
    )(qa, ka, va_all, bias_tiles, lam_vecs, g)


def _attn_b_kernel(q_ref, k_ref, v_ref, bias_ref, o_ref, k_scr, vt_scr, *, tq, win, seq, units):
    qi = pl.program_id(2)
    pad_blocks = BAND_REACH // tq
    win_blocks = win // tq

    @pl.when(qi == 0)
    def _():
        k_scr[0:BAND_REACH, :] = jnp.zeros((BAND_REACH, HEAD_W), BF16)
        k_scr[BAND_REACH:BAND_REACH + seq, :] = k_ref[...].astype(BF16)
        for blk in range(pad_blocks):
            vt_scr[blk] = jnp.zeros((HEAD_W, tq), BF16)
        for blk in range(seq // tq):
            vt_scr[pad_blocks + blk] = v_ref[blk * tq:(blk + 1) * tq, :].T.astype(BF16)

    def body(mask_early_keys):
        bias = bias_ref[...]
        raw = []
        for u in range(units):
            start = pl.multiple_of((qi * units + u) * tq, tq)
            raw.append(_nt_dot(k_scr[pl.ds(start, win), :], q_ref[u * tq:(u + 1) * tq, :]))
        for u in range(units):
            blk0 = qi * units + u
            s = raw[u] + bias
            if mask_early_keys:
                row = lax.broadcasted_iota(jnp.int32, (win, tq), 0)
                s = jnp.where(row + blk0 * tq >= BAND_REACH, s, NEG)
            m = jnp.max(s, axis=0, keepdims=True)
            e = jnp.exp2(s - m)
            l = jnp.sum(e, axis=0, keepdims=True)
            vt_win = jnp.concatenate([vt_scr[blk0 + w] for w in range(win_blocks)], axis=1)
            acc = jnp.dot(vt_win, e.astype(BF16), preferred_element_type=F32)
            o_ref[u * tq:(u + 1) * tq, :] = (acc / l).T.astype(BF16)

    early_steps = -(-BAND_REACH // (units * tq))

    @pl.when(qi < early_steps)
    def _():
        body(True)

    @pl.when(qi >= early_steps)
    def _():
        body(False)


def _attn_b_call(qb, kb, vb, bias_tile, *, batch, seq):
    tq, win, units = TQ_B, WIN_B, UNITS_B
    rows = tq * units
    nq = seq // rows
    return pl.pallas_call(
        functools.partial(_attn_b_kernel, tq=tq, win=win, seq=seq, units=units),
        out_shape=jax.ShapeDtypeStruct((batch * seq, MIX_W), BF16),
        grid=(batch, N_HEADS, nq),
        in_specs=[
            pl.BlockSpec((rows, HEAD_W), lambda b, h, qi: (b * nq + qi, h)),
            pl.BlockSpec((seq, HEAD_W), lambda b, h, qi: (b, h)),
            pl.BlockSpec((seq, HEAD_W), lambda b, h, qi: (b, h)),
            pl.BlockSpec((None, win, tq), lambda b, h, qi: (h, 0, 0)),
        ],
        out_specs=pl.BlockSpec((rows, HEAD_W), lambda b, h, qi: (b * nq + qi, h)),
        scratch_shapes=[
            pltpu.VMEM((BAND_REACH + seq, HEAD_W), BF16),
            pltpu.VMEM(((BAND_REACH + seq) // tq, HEAD_W, tq), BF16),
        ],
        compiler_params=_cparams(("arbitrary", "arbitrary", "arbitrary")),
        name="attn_b",
    )(qb, kb, vb, bias_tile)


def _softmax_pv(parts):
    m = None
    for s, _ in parts:
        pm = jnp.max(s, axis=-1, keepdims=True)
        m = pm if m is None else jnp.maximum(m, pm)
    l = None
    o = None
    for s, v in parts:
        e = jnp.exp2(s - m)
        pl_ = jnp.sum(e, axis=-1, keepdims=True)
        po = jnp.dot(e.astype(BF16), v, preferred_element_type=F32)
        l = pl_ if l is None else l + pl_
        o = po if o is None else o + po
    return o / l


def _sample_kernel(qa_ref, ka_ref, va_ref, cakt_ref, cav_ref, qb_ref, kb_ref, vb_ref, cbk_ref, cbv_ref,
                   ba_c_ref, ba_n_ref, bb_c_ref, bb_n_ref, lam_ref, g_ref, oa_ref, ob_ref,
                   *, t_new, lam_init):
    lam = _diff_lambda(lam_ref, lam_init)
    g = g_ref[...]
    past = cakt_ref.shape[-1]

    def head_rows(ref, h):
        return ref[pl.ds(h, ref.shape[0] // N_HEADS, stride=N_HEADS), :].astype(BF16)

    scores_a, scores_b = [], []
    for h in range(N_HEADS):
        cols = slice(h * HEAD_W, (h + 1) * HEAD_W)
        q2 = _stack_diff_queries(qa_ref[:, cols])
        kct = cakt_ref[h].reshape(2 * DK_A, past).astype(BF16)
        scores_a.append((jnp.dot(q2, kct, preferred_element_type=F32),
                         _nt_dot(q2, ka_ref[:, cols].astype(BF16))))
        qh = qb_ref[:, cols]
        scores_b.append((_nt_dot(qh, head_rows(cbk_ref, h)), _nt_dot(qh, kb_ref[:, cols].astype(BF16))))

    for h in range(N_HEADS):
        cols = slice(h * HEAD_W, (h + 1) * HEAD_W)
        s_c, s_n = scores_a[h]
        s_c = (s_c.reshape(2, t_new, -1) + ba_c_ref[h][None]).reshape(2 * t_new, -1)
        s_n = (s_n.reshape(2, t_new, -1) + ba_n_ref[h][None]).reshape(2 * t_new, -1)
        o2 = _softmax_pv([(s_c, head_rows(cav_ref, h)), (s_n, va_ref[:, cols].astype(BF16))])
        o = o2[:t_new] - lam * o2[t_new:]
        o = o * lax.rsqrt(jnp.mean(o * o, axis=-1, keepdims=True) + LN_EPS) * g
        oa_ref[:, cols] = (o * (1.0 - lam_init)).astype(BF16)
        s_c, s_n = scores_b[h]
        ob_ref[:, cols] = _softmax_pv([(s_c + bb_c_ref[h], head_rows(cbv_ref, h)),
                                       (s_n + bb_n_ref[h], vb_ref[:, cols].astype(BF16))]).astype(BF16)


def _sample_call(layer, qa, ka_all, va_all, cache_a_kt, cache_a_v, qb, kb, vb, cache_b_k, cache_b_v,
                 ba_c, ba_n, bb_c, bb_n, lam_vecs, g, *, batch, t_new, lam_init):
    past = cache_a_kt.shape[-1]
    band_past = cache_b_k.shape[2] // N_HEADS
    row = lambda b: (b, 0)
    slab = lambda b: (layer, b, 0)
    cache = lambda b: (layer, b, 0, 0)
    cache_t = lambda b: (layer, b, 0, 0, 0, 0)
    const3 = lambda b: (0, 0, 0)
    const2 = lambda b: (0, 0)
    return pl.pallas_call(
        functools.partial(_sample_kernel, t_new=t_new, lam_init=lam_init),
        out_shape=(jax.ShapeDtypeStruct((batch * t_new, MIX_W), BF16),
                   jax.ShapeDtypeStruct((batch * t_new, MIX_W), BF16)),
        grid=(batch,),
        in_specs=[
            pl.BlockSpec((t_new, MIX_W), row),
            pl.BlockSpec((None, t_new, MIX_W), slab),
            pl.BlockSpec((None, t_new, MIX_W), slab),
            pl.BlockSpec((None, None, N_HEADS, 2, DK_A, past), cache_t),
            pl.BlockSpec((None, None, past * N_HEADS, HEAD_W), cache),
            pl.BlockSpec((t_new, MIX_W), row),
            pl.BlockSpec((t_new, MIX_W), row),
            pl.BlockSpec((t_new, MIX_W), row),
            pl.BlockSpec((None, None, band_past * N_HEADS, HEAD_W), cache),
            pl.BlockSpec((None, None, band_past * N_HEADS, HEAD_W), cache),
            pl.BlockSpec(ba_c.shape, const3),
            pl.BlockSpec(ba_n.shape, const3),
            pl.BlockSpec(bb_c.shape, const3),
            pl.BlockSpec(bb_n.shape, const3),
            pl.BlockSpec((4, DK_A), const2),
            pl.BlockSpec((1, HEAD_W), const2),
        ],
        out_specs=(pl.BlockSpec((t_new, MIX_W), row), pl.BlockSpec((t_new, MIX_W), row)),
        compiler_params=_cparams(("arbitrary",)),
        name="sample_mix",
    )(qa, ka_all, va_all, cache_a_kt, cache_a_v, qb, kb, vb, cache_b_k, cache_b_v,
      ba_c, ba_n, bb_c, bb_n, lam_vecs, g)


def _mixout_kernel(oa_ref, ob_ref, sga_ref, sgb_ref, x_ref, gm_ref, woa_ref, wob_ref, wout_ref,
                   lg_ref, lb_ref, o_ref, *, gpt, alpha):
    a = jnp.dot(oa_ref[...], woa_ref[...], preferred_element_type=F32)
    b = jnp.dot(ob_ref[...], wob_ref[...], preferred_element_type=F32)
    merged = sga_ref[...].astype(F32) * a + sgb_ref[...].astype(F32) * b
    y = jnp.dot(merged.astype(BF16), wout_ref[...], preferred_element_type=F32)
    z = alpha * x_ref[...] + _gate(y, gm_ref[...], gpt)
    o_ref[...] = _ln(z) * lg_ref[...] + lb_ref[...]


def _mixout_call(oa, ob, sga, sgb, x, mod_l, w_oa, w_ob, w_out, ln_g, ln_b,
                 *, layer, rows_per_group, group0, bm, alpha):
    m, d = x.shape
    gpt = max(1, bm // rows_per_group)
    row = lambda i: (i, 0)
    const = lambda i: (0, 0)
    weight = lambda i: (layer, 0, 0)
    resident = pl.Buffered(1)
    return pl.pallas_call(
        functools.partial(_mixout_kernel, gpt=gpt, alpha=alpha),
        out_shape=jax.ShapeDtypeStruct((m, d), F32),
        grid=(m // bm,),
        in_specs=[
            pl.BlockSpec((bm, MIX_W), row),
            pl.BlockSpec((bm, MIX_W), row),
            pl.BlockSpec((bm, d), row),
            pl.BlockSpec((bm, d), row),
            pl.BlockSpec((bm, d), row),
            _mod_spec(gpt, rows_per_group, bm, group0, 2, d),
            pl.BlockSpec((None, MIX_W, d), weight, pipeline_mode=resident),
            pl.BlockSpec((None, MIX_W, d), weight, pipeline_mode=resident),
            pl.BlockSpec((None, d, d), weight, pipeline_mode=resident),
            pl.BlockSpec((1, d), const),
            pl.BlockSpec((1, d), const),
        ],
        out_specs=pl.BlockSpec((bm, d), row),
        compiler_params=_cparams(("arbitrary",)),
        name="mixout",
    )(oa, ob, sga, sgb, x, mod_l, w_oa, w_ob, w_out, ln_g, ln_b)


def _ffn_kernel(x_ref, sc_ref, sh_ref, gf_ref, w1_ref, w3_ref, w2_ref, lg_ref, lb_ref, o_ref,
                h_scr, *, gpt, alpha):
    kf = pl.program_id(1)
    bm = h_scr.shape[0]
    chunk = min(bm, LN_CHUNK_ROWS)
    gpc = max(1, gpt * chunk // bm)

    def chunks():
        for r in range(bm // chunk):
            g0 = r * gpc if gpt > 1 else 0
            yield slice(r * chunk, (r + 1) * chunk), slice(g0, g0 + gpc)

    @pl.when(kf == 0)
    def _():
        for rows, grp in chunks():
            h = _modulate(_ln(x_ref[rows, :]), sc_ref[grp], sh_ref[grp], gpc)
            h_scr[rows, :] = h.astype(BF16)
            o_ref[rows, :] = jnp.zeros((chunk, o_ref.shape[1]), F32)

    sub = min(bm, FFN_SUB_ROWS)
    ab = []
    for r in range(bm // sub):
        h = h_scr[r * sub:(r + 1) * sub, :]
        ab.append((jnp.dot(h, w1_ref[...], preferred_element_type=F32),
                   jnp.dot(h, w3_ref[...], preferred_element_type=F32)))
    for r, (a, b) in enumerate(ab):
        u = (a * _sigmoid(a)) * b
        o_ref[r * sub:(r + 1) * sub, :] += jnp.dot(u.astype(BF16), w2_ref[...], preferred_element_type=F32)

    @pl.when(kf == pl.num_programs(1) - 1)
    def _():
        for rows, grp in chunks():
            z = alpha * x_ref[rows, :] + _gate(o_ref[rows, :], gf_ref[grp], gpc)
            o_ref[rows, :] = _ln(z) * lg_ref[...] + lb_ref[...]


def _ffn_call(x, mod_l, w1, w3, w2, ln_g, ln_b, *, layer, rows_per_group, group0, bm, tf, alpha):
    m, d = x.shape
    dff = w1.shape[2]
    gpt = max(1, bm // rows_per_group)
    return pl.pallas_call(
        functools.partial(_ffn_kernel, gpt=gpt, alpha=alpha),
        out_shape=jax.ShapeDtypeStruct((m, d), F32),
        grid=(m // bm, dff // tf),
        in_specs=[
            pl.BlockSpec((bm, d), lambda i, k: (i, 0)),
            _mod_spec(gpt, rows_per_group, bm, group0, 4, d),
            _mod_spec(gpt, rows_per_group, bm, group0, 3, d),
            _mod_spec(gpt, rows_per_group, bm, group0, 5, d),
            pl.BlockSpec((None, d, tf), lambda i, k: (layer, 0, k)),
            pl.BlockSpec((None, d, tf), lambda i, k: (layer, 0, k)),
            pl.BlockSpec((None, tf, d), lambda i, k: (layer, k, 0)),
            pl.BlockSpec((1, d), lambda i, k: (0, 0)),
            pl.BlockSpec((1, d), lambda i, k: (0, 0)),
        ],
        out_specs=pl.BlockSpec((bm, d), lambda i, k: (i, 0)),
        scratch_shapes=[pltpu.VMEM((bm, d), BF16)],
        compiler_params=_cparams(("arbitrary", "arbitrary")),
        name="ffn",
    )(x, mod_l, mod_l, mod_l, w1, w3, w2, ln_g, ln_b)


def _ff_tile(dff):
    for tf in (512, 256, 128):
        if dff % tf == 0:
            return tf
    raise ValueError(f"unsupported FFN width {dff}")


def kernel(x_prompt, x_sample, cache_a_k, cache_a_v, cache_b_k, cache_b_v, c_prompt, c_sample,
           w_mod, b_mod, w_in, lambda_q1, lambda_k1, lambda_q2, lambda_k2, subln_g, t5_bias,
           rel_bias, w_oa, w_ob, w_out, ln1_g, ln1_b, w1, w3, w2, ln2_g, ln2_b):
    depth = w_mod.shape[0]
    batch, seq, d = x_prompt.shape
    dec_batch, t_new, _ = x_sample.shape
    past = cache_a_k.shape[2]
    band_past = cache_b_k.shape[2]
    assert past % CHUNK == 0 and t_new <= CHUNK and band_past == BAND_REACH and past >= BAND_REACH
    assert seq % TQ_A == 0 and seq % (TQ_B * UNITS_B) == 0 and seq >= BAND_REACH and d % 128 == 0
    alpha = (2 * depth) ** 0.25
    prompt_band = min(BAND_REACH, seq)

    bm_p = min(512, seq)
    bm_s = min(512, dec_batch * t_new)
    bm_in = min(1024, seq)
    tn = math.gcd(MIX_W // 2, d)
    tf = _ff_tile(w1.shape[2])

    c_all = jnp.concatenate([c_sample, c_prompt], axis=0)
    mod = _mod_call(c_all, w_mod, b_mod).reshape(depth, dec_batch + batch, 1, 6 * d)

    t5_tab = _t5_table(t5_bias)
    ta_tiles = _attn_a_bias_tiles(t5_tab, TQ_A)
    q_pos_s = past + np.arange(t_new)
    rel_a = np.arange(past + t_new)[None, :] - q_pos_s[:, None]
    ba_s = _lookup(t5_tab, _t5_bucket(rel_a))
    kb_pos_s = past - band_past + np.arange(band_past + t_new)
    rel_b = np.clip(kb_pos_s[None, :] - q_pos_s[:, None], -REL_CLIP, REL_CLIP) + REL_CLIP

    cakt = jnp.transpose(cache_a_k, (0, 1, 3, 4, 5, 2))
    cav = cache_a_v.reshape(depth, dec_batch, past * N_HEADS, HEAD_W)
    cbk = cache_b_k.reshape(depth, dec_batch, band_past * N_HEADS, HEAD_W)
    cbv = cache_b_v.reshape(depth, dec_batch, band_past * N_HEADS, HEAD_W)

    w_in_b = _to_bf16(w_in)
    w_mix_b = (_to_bf16(w_oa), _to_bf16(w_ob), _to_bf16(w_out))
    w_ffn_b = (_to_bf16(w1), _to_bf16(w3), _to_bf16(w2))

    xp = x_prompt.reshape(batch * seq, d)
    xs = x_sample.reshape(dec_batch * t_new, d)
    kv_p, kv_s = (), ()
    bk_p, bv_p, bk_s, bv_s = [], [], [], []
    for l in range(depth):
        lam_init = 0.8 - 0.6 * math.exp(-0.3 * l)
        lam_vecs = jnp.stack([lambda_q1[l], lambda_k1[l], lambda_q2[l], lambda_k2[l]]).astype(F32)
        g = subln_g[l].reshape(1, HEAD_W).astype(F32)
        mod_l = mod[l:l + 1]
        ln1 = (ln1_g[l].reshape(1, d), ln1_b[l].reshape(1, d))
        ln2 = (ln2_g[l].reshape(1, d), ln2_b[l].reshape(1, d))
        rel_tab = rel_bias[l] * LOG2E
        bb_tile = _attn_b_bias_tile(rel_tab, TQ_B, WIN_B)
        bb_s = _lookup(rel_tab, rel_b)

        grp = dict(rows_per_group=seq, group0=dec_batch, bm=bm_p)
        p = _inproj_call(xp, mod_l, w_in_b, kv_p, layer=l, depth=depth, tn=tn, key_major=True,
                         **dict(grp, bm=bm_in))
        kv_p = (p["ka_all"], p["va_all"])
        oa = _attn_a_call(p["qa"], p["ka"], p["va_all"], ta_tiles, lam_vecs, g,
                          layer=l, batch=batch, seq=seq, lam_init=lam_init)
        ob = _attn_b_call(p["qb"], p["kb"], p["vb"], bb_tile, batch=batch, seq=seq)
        xp = _mixout_call(oa, ob, p["sga"], p["sgb"], xp, mod_l, *w_mix_b, *ln1,
                          layer=l, alpha=alpha, **grp)
        xp = _ffn_call(xp, mod_l, *w_ffn_b, *ln2, layer=l, tf=tf, alpha=alpha, **dict(grp, bm=bm_in))
        bk_p.append(p["kb"].reshape(batch, seq, N_HEADS, HEAD_W)[:, seq - prompt_band:])
        bv_p.append(p["vb"].reshape(batch, seq, N_HEADS, HEAD_W)[:, seq - prompt_band:])

        grp = dict(rows_per_group=t_new, group0=0, bm=bm_s)
        p = _inproj_call(xs, mod_l, w_in_b, kv_s, layer=l, depth=depth, tn=tn, key_major=False, **grp)
        kv_s = (p["ka_all"], p["va_all"])
        oa, ob = _sample_call(l, p["qa"], p["ka_all"], p["va_all"], cakt, cav, p["qb"], p["kb"],
                              p["vb"], cbk, cbv,
                              ba_s[:, :, :past], ba_s[:, :, past:], bb_s[:, :, :band_past],
                              bb_s[:, :, band_past:], lam_vecs, g,
                              batch=dec_batch, t_new=t_new, lam_init=lam_init)
        xs = _mixout_call(oa, ob, p["sga"], p["sgb"], xs, mod_l, *w_mix_b, *ln1,
                          layer=l, alpha=alpha, **grp)
        xs = _ffn_call(xs, mod_l, *w_ffn_b, *ln2, layer=l, tf=tf, alpha=alpha, **grp)
        bk_s.append(p["kb"].reshape(dec_batch, t_new, N_HEADS, HEAD_W))
        bv_s.append(p["vb"].reshape(dec_batch, t_new, N_HEADS, HEAD_W))

    a_k_prompt = jnp.transpose(kv_p[0].reshape(depth, batch, N_HEADS, 2, DK_A, seq), (0, 1, 5, 2, 3, 4))
    return (xp.reshape(batch, seq, d), xs.reshape(dec_batch, t_new, d),
            a_k_prompt,
            kv_p[1].reshape(depth, batch, seq, N_HEADS, HEAD_W),
            jnp.stack(bk_p), jnp.stack(bv_p),
            kv_s[0].reshape(depth, dec_batch, t_new, N_HEADS, 2, DK_A),
            kv_s[1].reshape(depth, dec_batch, t_new, N_HEADS, HEAD_W),
            jnp.stack(bk_s), jnp.stack(bv_s))
```

```python
import functools
import math

import jax
import jax.numpy as jnp
import numpy as np
from jax import lax
from jax.experimental import pallas as pl
from jax.experimental.pallas import tpu as pltpu

F32 = jnp.float32
BF16 = jnp.bfloat16

CHUNK = 64
N_HEADS = 8
HEAD_W = 128
DK_A = 64
BAND_PREV_CHUNKS = 8
BAND_REACH = BAND_PREV_CHUNKS * CHUNK
REL_CLIP = 128
T5_BUCKETS = 32
T5_MAX_DIST = 128
NEG = -1e30
LN_EPS = 1e-5
MIX_W = N_HEADS * HEAD_W
LOG2E = math.log2(math.e)
QA_SCALE = DK_A ** -0.5 * LOG2E
QB_SCALE = HEAD_W ** -0.5 * LOG2E

V7X_VMEM_BYTES = 64 * 1024 * 1024
VMEM_LIMIT = V7X_VMEM_BYTES - 3 * 1024 * 1024

TQ_A = 256
HEADS_PER_STEP_A = 8
FFN_SUB_ROWS = 512
LN_CHUNK_ROWS = 256
CAST_BLOCK_ELEMS = 1 << 21
ONES_ROWS = 16
TQ_B = 128
WIN_B = BAND_REACH + TQ_B
UNITS_B = 8


def _cparams(sem):
    return pltpu.CompilerParams(dimension_semantics=sem, vmem_limit_bytes=VMEM_LIMIT)


def _sigmoid(x):
    return 1.0 / (1.0 + jnp.exp(-x))


def _ln(x):
    mu = jnp.mean(x, axis=-1, keepdims=True)
    xc = x - mu
    var = jnp.mean(xc * xc, axis=-1, keepdims=True)
    return xc * lax.rsqrt(var + LN_EPS)


def _modulate(y, scale, shift, gpt):
    if gpt == 1:
        return y * (1.0 + scale[0]) + shift[0]
    bm, d = y.shape
    y3 = y.reshape(gpt, bm // gpt, d)
    return (y3 * (1.0 + scale) + shift).reshape(bm, d)


def _gate(y, gate, gpt):
    if gpt == 1:
        return y * gate[0]
    bm, d = y.shape
    return (y.reshape(gpt, bm // gpt, d) * gate).reshape(bm, d)


def _nt_dot(a, b):
    return lax.dot_general(a, b, (((1,), (1,)), ((), ())), preferred_element_type=F32)


def _mod_kernel(c_ref, w_ref, b_ref, o_ref):
    c = c_ref[...]
    s = (c * _sigmoid(c)).astype(BF16)
    o_ref[...] = jnp.dot(s, w_ref[...].astype(BF16), preferred_element_type=F32) + b_ref[...]


def _mod_call(c_all, w_mod, b_mod):
    depth, d, n = w_mod.shape
    g = c_all.shape[0]
    tn = math.gcd(n, 1024)
    return pl.pallas_call(
        _mod_kernel,
        out_shape=jax.ShapeDtypeStruct((depth, g, n), F32),
        grid=(depth, n // tn),
        in_specs=[
            pl.BlockSpec((g, d), lambda l, j: (0, 0)),
            pl.BlockSpec((None, d, tn), lambda l, j: (l, 0, j)),
            pl.BlockSpec((None, 1, tn), lambda l, j: (l, 0, j)),
        ],
        out_specs=pl.BlockSpec((None, g, tn), lambda l, j: (l, 0, j)),
        compiler_params=_cparams(("arbitrary", "arbitrary")),
        name="mod",
    )(c_all, w_mod, b_mod.reshape(depth, 1, n))


def _cast_kernel(w_ref, o_ref):
    o_ref[...] = w_ref[...].astype(BF16)


def _to_bf16(w):
    depth, r, c = w.shape
    br = r
    while br % 2 == 0 and br > 16 and br * c > CAST_BLOCK_ELEMS:
        br //= 2
    return pl.pallas_call(
        _cast_kernel,
        out_shape=jax.ShapeDtypeStruct(w.shape, BF16),
        grid=(depth, r // br),
        in_specs=[pl.BlockSpec((None, br, c), lambda l, i: (l, i, 0))],
        out_specs=pl.BlockSpec((None, br, c), lambda l, i: (l, i, 0)),
        compiler_params=_cparams(("arbitrary", "arbitrary")),
        name="to_bf16",
    )(w)


def _inproj_segments(d, tn):
    widths = (MIX_W,) * 6 + (d, d)
    segs, lo = [], 0
    for w in widths:
        assert w % tn == 0
        segs.append((lo, w // tn))
        lo += w // tn
    return tuple(segs)


def _inproj_write(ref, kind, acc):
    if kind == "f32":
        ref[...] = acc
    elif kind == "bf16":
        ref[...] = acc.astype(BF16)
    elif kind == "qa":
        ref[...] = (acc * QA_SCALE).astype(BF16)
    elif kind == "qb":
        ref[...] = (acc * QB_SCALE).astype(BF16)
    elif kind == "gate":
        ref[...] = _sigmoid(acc).astype(BF16)
    elif kind == "key_major":
        for h in range(ref.shape[0]):
            ref[h] = acc[:, h * HEAD_W:(h + 1) * HEAD_W].T
    else:
        raise ValueError(kind)


def _inproj_kernel(*refs, gpt, segs, plan, n_in, n_out):
    x_ref, sc_ref, sh_ref, w_ref = refs[:4]
    out_refs = refs[n_in:n_in + n_out]
    h_scr = refs[n_in + n_out]
    j = pl.program_id(1)

    @pl.when(j == 0)
    def _():
        h = _modulate(_ln(x_ref[...]), sc_ref[...], sh_ref[...], gpt)
        h_scr[...] = h.astype(BF16)

    for (lo, cnt), writers in zip(segs, plan):
        @pl.when((j >= lo) & (j < lo + cnt))
        def _(writers=writers):
            acc = jnp.dot(h_scr[...], w_ref[...], preferred_element_type=F32)
            for idx, kind in writers:
                _inproj_write(out_refs[idx], kind, acc)


def _mod_spec(gpt, rows_per_group, bm, group0, which, d):
    if gpt == 1:
        tiles_per_group = rows_per_group // bm
        return pl.BlockSpec((None, 1, 1, d), lambda i, *_: (0, group0 + i // tiles_per_group, 0, which))
    return pl.BlockSpec((None, gpt, 1, d), lambda i, *_: (0, group0 // gpt + i, 0, which))


def _inproj_call(x, mod_l, w_in, kv_carry, *, layer, depth, rows_per_group, group0, bm, tn, key_major):
    m, d = x.shape
    n = w_in.shape[2]
    gpt = max(1, bm // rows_per_group)
    segs = _inproj_segments(d, tn)
    assert n == tn * (segs[-1][0] + segs[-1][1])
    n_carry = len(kv_carry)

    def col(seg):
        lo, cnt = seg
        return lambda j: jnp.clip(j - lo, 0, cnt - 1)

    def rows(seg, width, dtype):
        c = col(seg)
        return (jax.ShapeDtypeStruct((m, width), dtype), pl.BlockSpec((bm, tn), lambda i, j: (i, c(j))))

    def slab(seg):
        c = col(seg)
        return (jax.ShapeDtypeStruct((depth, m, MIX_W), F32),
                pl.BlockSpec((None, bm, tn), lambda i, j: (layer, i, c(j))))

    def key_major_slab(seg):
        c = col(seg)
        tiles_per_seq = rows_per_group // bm
        return (jax.ShapeDtypeStruct((depth, m // rows_per_group, N_HEADS, HEAD_W, rows_per_group), F32),
                pl.BlockSpec((None, None, tn // HEAD_W, HEAD_W, bm),
                             lambda i, j: (layer, i // tiles_per_seq, c(j), 0, i % tiles_per_seq)))

    outs = {"qa": rows(segs[0], MIX_W, BF16)}
    if key_major:
        assert rows_per_group % bm == 0
        outs["ka"] = rows(segs[1], MIX_W, BF16)
        outs["ka_all"] = key_major_slab(segs[1])
        ka_writers = (("ka", "bf16"), ("ka_all", "key_major"))
        outs["va"] = rows(segs[2], MIX_W, BF16)
        va_writers = (("va_all", "f32"), ("va", "bf16"))
    else:
        outs["ka_all"] = slab(segs[1])
        ka_writers = (("ka_all", "f32"),)
        va_writers = (("va_all", "f32"),)
    outs["va_all"] = slab(segs[2])
    outs["qb"] = rows(segs[3], MIX_W, BF16)
    outs["kb"] = rows(segs[4], MIX_W, F32)
    outs["vb"] = rows(segs[5], MIX_W, F32)
    outs["sga"] = rows(segs[6], d, BF16)
    outs["sgb"] = rows(segs[7], d, BF16)
    names = list(outs)
    plan = tuple(tuple((names.index(nm), kind) for nm, kind in writers) for writers in (
        (("qa", "qa"),), ka_writers, va_writers, (("qb", "qb"),),
        (("kb", "f32"),), (("vb", "f32"),), (("sga", "gate"),), (("sgb", "gate"),)))
    n_in = 4 + n_carry
    aliases = {4 + c: names.index(nm) for c, nm in enumerate(("ka_all", "va_all")[:n_carry])}
    res = pl.pallas_call(
        functools.partial(_inproj_kernel, gpt=gpt, segs=segs, plan=plan, n_in=n_in, n_out=len(names)),
        out_shape=tuple(outs[nm][0] for nm in names),
        grid=(m // bm, n // tn),
        in_specs=[
            pl.BlockSpec((bm, d), lambda i, j: (i, 0)),
            _mod_spec(gpt, rows_per_group, bm, group0, 1, d),
            _mod_spec(gpt, rows_per_group, bm, group0, 0, d),
            pl.BlockSpec((None, d, tn), lambda i, j: (layer, 0, j)),
        ] + [pl.BlockSpec(memory_space=pl.ANY)] * n_carry,
        out_specs=tuple(outs[nm][1] for nm in names),
        scratch_shapes=[pltpu.VMEM((bm, d), BF16)],
        input_output_aliases=aliases,
        compiler_params=_cparams(("arbitrary", "arbitrary")),
        name="inproj",
    )(x, mod_l, mod_l, w_in, *kv_carry)
    return dict(zip(names, res))


def _t5_bucket(rel):
    nb = T5_BUCKETS // 2
    max_exact = nb // 2
    rel = np.asarray(rel, np.int32)
    bucket = (rel > 0).astype(np.int32) * nb
    n = np.abs(rel)
    nf = np.maximum(n, 1).astype(np.float32)
    ratio = np.log(nf / np.float32(max_exact)) / np.float32(math.log(T5_MAX_DIST / max_exact))
    large = max_exact + (ratio * np.float32(nb - max_exact)).astype(np.int32)
    large = np.minimum(large, nb - 1)
    return bucket + np.where(n < max_exact, n, large)


def _lookup(table, idx):
    n = table.shape[0]
    flat = jnp.asarray(np.asarray(idx, np.int32).reshape(-1))
    onehot = (flat[None, :] == jnp.arange(n, dtype=jnp.int32)[:, None]).astype(F32)
    out = lax.dot_general(table.astype(F32), onehot, (((0,), (0,)), ((), ())),
                          precision=lax.Precision.HIGHEST, preferred_element_type=F32)
    return out.reshape((table.shape[1],) + tuple(np.shape(idx)))


def _t5_table(t5_bias):
    far = t5_bias[int(_t5_bucket(-T5_MAX_DIST))]
    return (t5_bias - far[None, :]) * LOG2E


def _attn_a_bias_tiles(t5_tab, t):
    assert t >= T5_MAX_DIST
    k = np.arange(t)[:, None]
    q = np.arange(t)[None, :]
    visible = (k // CHUNK) <= (q // CHUNK)
    idx = np.stack([_t5_bucket(k - q), _t5_bucket(k - t - q)])
    vis = np.stack([visible, np.ones_like(visible)])
    return jnp.where(vis[None], _lookup(t5_tab, idx), NEG)


def _attn_b_bias_tile(rel_tab, tq, win):
    kw = np.arange(win)[:, None] - BAND_REACH
    q = np.arange(tq)[None, :]
    rel = np.clip(kw - q, -REL_CLIP, REL_CLIP) + REL_CLIP
    dchunk = q // CHUNK - np.floor_divide(kw, CHUNK)
    visible = (dchunk >= 0) & (dchunk <= BAND_PREV_CHUNKS)
    return jnp.where(visible[None], _lookup(rel_tab, rel), NEG)


def _diff_lambda(lam_ref, lam_init):
    lv = lam_ref[...]
    s1 = jnp.sum(lv[0:1] * lv[1:2], axis=-1, keepdims=True)
    s2 = jnp.sum(lv[2:3] * lv[3:4], axis=-1, keepdims=True)
    return jnp.exp(s1) - jnp.exp(s2) + lam_init


def _stack_diff_queries(q):
    lane = lax.broadcasted_iota(jnp.int32, q.shape, 1)
    zero = jnp.zeros_like(q)
    return jnp.concatenate([jnp.where(lane < DK_A, q, zero), jnp.where(lane >= DK_A, q, zero)], axis=0)


def _stack_diff_queries_t(q):
    qt = q.astype(F32).T.astype(BF16)
    row = lax.broadcasted_iota(jnp.int32, qt.shape, 0)
    zero = jnp.zeros_like(qt)
    return jnp.concatenate([jnp.where(row < DK_A, qt, zero), jnp.where(row >= DK_A, qt, zero)], axis=1)


def _attn_a_kernel(q_ref, k_ref, v_ref, bias_ref, lam_ref, g_ref, o_ref,
                   vt_scr, s_scr, acc_scr, m_scr, *, t, nh, lam_init):
    qi = pl.program_id(2)
    nkb = vt_scr.shape[1]
    heads = range(nh)

    @pl.when(qi == 0)
    def _():
        sub = lax.broadcasted_iota(jnp.int32, (ONES_ROWS, t), 0)
        ones_rows = jnp.where(sub == 0, 1.0, 0.0).astype(BF16)
        for hh in heads:
            cols = slice(hh * HEAD_W, (hh + 1) * HEAD_W)
            for kb in range(nkb):
                vt_scr[hh, kb, 0:HEAD_W, :] = v_ref[kb * t:(kb + 1) * t, cols].astype(F32).T.astype(BF16)
                vt_scr[hh, kb, HEAD_W:HEAD_W + ONES_ROWS, :] = ones_rows

    q2t = [_stack_diff_queries_t(q_ref[:, hh * HEAD_W:(hh + 1) * HEAD_W]) for hh in heads]
    acc_scr[...] = jnp.zeros(acc_scr.shape, F32)

    def scores(hh, kb):
        start = pl.multiple_of(kb * t, t)
        s_scr[hh] = jnp.dot(k_ref[pl.ds(start, t), hh * HEAD_W:(hh + 1) * HEAD_W], q2t[hh],
                            preferred_element_type=F32)

    def step(kb, ms, bias_idx, prefetch):
        out = []
        for hh in heads:
            s = s_scr[hh]
            if bias_idx is not None:
                bias = bias_ref[hh, bias_idx]
                s = s + jnp.concatenate([bias, bias], axis=1)
            m_new = jnp.maximum(ms[hh], jnp.max(s, axis=0, keepdims=True))
            alpha = jnp.exp2(ms[hh] - m_new)
            e = jnp.exp2(s - m_new).astype(BF16)
            pv = jnp.dot(vt_scr[hh, kb], e, preferred_element_type=F32)
            if prefetch:
                scores(hh, kb + 1)
            acc_scr[hh] = alpha * acc_scr[hh] + pv
            out.append(m_new)
        return tuple(out)

    def load_m():
        return tuple(m_scr[hh] for hh in heads)

    def store_m(ms):
        for hh in heads:
            m_scr[hh] = ms[hh]

    for hh in heads:
        scores(hh, 0)
    m0 = tuple(jnp.full((1, 2 * t), -jnp.inf, F32) for _ in heads)
    n_far = jnp.maximum(qi - 1, 0)
    store_m(lax.fori_loop(
        0, lax.shift_right_logical(n_far, 1),
        lambda p, ms: step(2 * p + 1, step(2 * p, ms, None, True), None, True), m0))

    def finish(steps):
        ms = load_m()
        for kb, bias_idx, prefetch in steps:
            ms = step(kb, ms, bias_idx, prefetch)
        lam = _diff_lambda(lam_ref, lam_init)
        for hh in heads:
            o2 = acc_scr[hh, 0:HEAD_W, :] / acc_scr[hh, HEAD_W:HEAD_W + 1, :]
            o = o2[:, :t] - lam * o2[:, t:]
            o = o * lax.rsqrt(jnp.mean(o * o, axis=0, keepdims=True) + LN_EPS)
            o_ref[:, hh * HEAD_W:(hh + 1) * HEAD_W] = (o.T * g_ref[...] * (1.0 - lam_init)).astype(BF16)

    previous = (qi - 1, 1, True)
    diagonal = (qi, 0, False)
    odd_far = lax.rem(n_far, 2) == 1

    @pl.when(qi == 0)
    def _():
        finish([diagonal])

    @pl.when((qi >= 1) & jnp.logical_not(odd_far))
    def _():
        finish([previous, diagonal])

    @pl.when((qi >= 1) & odd_far)
    def _():
        finish([(n_far - 1, None, True), previous, diagonal])


def _attn_a_call(qa, ka, va, bias_tiles, lam_vecs, g, *, batch, seq, lam_init):
    t, nh = TQ_A, HEADS_PER_STEP_A
    nq = seq // t
    w = nh * HEAD_W
    rare = pl.Buffered(1)
    return pl.pallas_call(
        functools.partial(_attn_a_kernel, t=t, nh=nh, lam_init=lam_init),
        out_shape=jax.ShapeDtypeStruct((batch * seq, MIX_W), BF16),
        grid=(batch, N_HEADS // nh, nq),
        in_specs=[
            pl.BlockSpec((t, w), lambda b, h, qi: (b * nq + qi, h)),
            pl.BlockSpec((seq, w), lambda b, h, qi: (b, h), pipeline_mode=rare),
            pl.BlockSpec((seq, w), lambda b, h, qi: (b, h), pipeline_mode=rare),
            pl.BlockSpec((nh, 2, t, t), lambda b, h, qi: (h, 0, 0, 0), pipeline_mode=rare),
            pl.BlockSpec((4, DK_A), lambda b, h, qi: (0, 0)),
            pl.BlockSpec((1, HEAD_W), lambda b, h, qi: (0, 0)),
        ],
        out_specs=pl.BlockSpec((t, w), lambda b, h, qi: (b * nq + qi, h)),
        scratch_shapes=[
            pltpu.VMEM((nh, seq // t, HEAD_W + ONES_ROWS, t), BF16),
            pltpu.VMEM((nh, t, 2 * t), F32),
            pltpu.VMEM((nh, HEAD_W + ONES_ROWS, 2 * t), F32),
            pltpu.VMEM((nh, 1, 2 * t), F32),
        ],
        compiler_params=_cparams(("arbitrary", "arbitrary", "arbitrary")),
        name="attn_a",
    )(qa, ka, va, bias_tiles, lam_vecs, g)


def _attn_b_kernel(q_ref, k_ref, v_ref, bias_ref, o_ref, k_scr, vt_scr, *, tq, win, seq, units):
    qi = pl.program_id(2)
    pad_blocks = BAND_REACH // tq
    win_blocks = win // tq

    @pl.when(qi == 0)
    def _():
        k_scr[0:BAND_REACH, :] = jnp.zeros((BAND_REACH, HEAD_W), BF16)
        k_scr[BAND_REACH:BAND_REACH + seq, :] = k_ref[...].astype(BF16)
        for blk in range(pad_blocks):
            vt_scr[blk] = jnp.zeros((HEAD_W, tq), BF16)
        for blk in range(seq // tq):
            vt_scr[pad_blocks + blk] = v_ref[blk * tq:(blk + 1) * tq, :].T.astype(BF16)

    def body(mask_early_keys):
        bias = bias_ref[...]
        raw = []
        for u in range(units):
            start = pl.multiple_of((qi * units + u) * tq, tq)
            raw.append(_nt_dot(k_scr[pl.ds(start, win), :], q_ref[u * tq:(u + 1) * tq, :]))
        for u in range(units):
            blk0 = qi * units + u
            s = raw[u] + bias
            if mask_early_keys:
                row = lax.broadcasted_iota(jnp.int32, (win, tq), 0)
                s = jnp.where(row + blk0 * tq >= BAND_REACH, s, NEG)
            m = jnp.max(s, axis=0, keepdims=True)
            e = jnp.exp2(s - m)
            l = jnp.sum(e, axis=0, keepdims=True)
            vt_win = jnp.concatenate([vt_scr[blk0 + w] for w in range(win_blocks)], axis=1)
            acc = jnp.dot(vt_win, e.astype(BF16), preferred_element_type=F32)
            o_ref[u * tq:(u + 1) * tq, :] = (acc / l).T.astype(BF16)

    early_steps = -(-BAND_REACH // (units * tq))

    @pl.when(qi < early_steps)
    def _():
        body(True)

    @pl.when(qi >= early_steps)
    def _():
        body(False)


def _attn_b_call(qb, kb, vb, bias_tile, *, batch, seq):
    tq, win, units = TQ_B, WIN_B, UNITS_B
    rows = tq * units
    nq = seq // rows
    return pl.pallas_call(
        functools.partial(_attn_b_kernel, tq=tq, win=win, seq=seq, units=units),
        out_shape=jax.ShapeDtypeStruct((batch * seq, MIX_W), BF16),
        grid=(batch, N_HEADS, nq),
        in_specs=[
            pl.BlockSpec((rows, HEAD_W), lambda b, h, qi: (b * nq + qi, h)),
            pl.BlockSpec((seq, HEAD_W), lambda b, h, qi: (b, h)),
            pl.BlockSpec((seq, HEAD_W), lambda b, h, qi: (b, h)),
            pl.BlockSpec((None, win, tq), lambda b, h, qi: (h, 0, 0)),
        ],
        out_specs=pl.BlockSpec((rows, HEAD_W), lambda b, h, qi: (b * nq + qi, h)),
        scratch_shapes=[
            pltpu.VMEM((BAND_REACH + seq, HEAD_W), BF16),
            pltpu.VMEM(((BAND_REACH + seq) // tq, HEAD_W, tq), BF16),
        ],
        compiler_params=_cparams(("arbitrary", "arbitrary", "arbitrary")),
        name="attn_b",
    )(qb, kb, vb, bias_tile)


def _softmax_pv(parts):
    m = None
    for s, _ in parts:
        pm = jnp.max(s, axis=-1, keepdims=True)
        m = pm if m is None else jnp.maximum(m, pm)
    l = None
    o = None
    for s, v in parts:
        e = jnp.exp2(s - m)
        pl_ = jnp.sum(e, axis=-1, keepdims=True)
        po = jnp.dot(e.astype(BF16), v, preferred_element_type=F32)
        l = pl_ if l is None else l + pl_
        o = po if o is None else o + po
    return o / l


def _sample_kernel(qa_ref, ka_ref, va_ref, cakt_ref, cav_ref, qb_ref, kb_ref, vb_ref, cbk_ref, cbv_ref,
                   ba_c_ref, ba_n_ref, bb_c_ref, bb_n_ref, lam_ref, g_ref, oa_ref, ob_ref,
                   *, t_new, lam_init):
    lam = _diff_lambda(lam_ref, lam_init)
    g = g_ref[...]
    past = cakt_ref.shape[-1]

    def head_rows(ref, h):
        return ref[pl.ds(h, ref.shape[0] // N_HEADS, stride=N_HEADS), :].astype(BF16)

    scores_a, scores_b = [], []
    for h in range(N_HEADS):
        cols = slice(h * HEAD_W, (h + 1) * HEAD_W)
        q2 = _stack_diff_queries(qa_ref[:, cols])
        kct = cakt_ref[h].reshape(2 * DK_A, past).astype(BF16)
        scores_a.append((jnp.dot(q2, kct, preferred_element_type=F32),
                         _nt_dot(q2, ka_ref[:, cols].astype(BF16))))
        qh = qb_ref[:, cols]
        scores_b.append((_nt_dot(qh, head_rows(cbk_ref, h)), _nt_dot(qh, kb_ref[:, cols].astype(BF16))))

    for h in range(N_HEADS):
        cols = slice(h * HEAD_W, (h + 1) * HEAD_W)
        s_c, s_n = scores_a[h]
        s_c = (s_c.reshape(2, t_new, -1) + ba_c_ref[h][None]).reshape(2 * t_new, -1)
        s_n = (s_n.reshape(2, t_new, -1) + ba_n_ref[h][None]).reshape(2 * t_new, -1)
        o2 = _softmax_pv([(s_c, head_rows(cav_ref, h)), (s_n, va_ref[:, cols].astype(BF16))])
        o = o2[:t_new] - lam * o2[t_new:]
        o = o * lax.rsqrt(jnp.mean(o * o, axis=-1, keepdims=True) + LN_EPS) * g
        oa_ref[:, cols] = (o * (1.0 - lam_init)).astype(BF16)
        s_c, s_n = scores_b[h]
        ob_ref[:, cols] = _softmax_pv([(s_c + bb_c_ref[h], head_rows(cbv_ref, h)),
                                       (s_n + bb_n_ref[h], vb_ref[:, cols].astype(BF16))]).astype(BF16)


def _sample_call(layer, qa, ka_all, va_all, cache_a_kt, cache_a_v, qb, kb, vb, cache_b_k, cache_b_v,
                 ba_c, ba_n, bb_c, bb_n, lam_vecs, g, *, batch, t_new, lam_init):
    past = cache_a_kt.shape[-1]
    band_past = cache_b_k.shape[2] // N_HEADS
    row = lambda b: (b, 0)
    slab = lambda b: (layer, b, 0)
    cache = lambda b: (layer, b, 0, 0)
    cache_t = lambda b: (layer, b, 0, 0, 0, 0)
    const3 = lambda b: (0, 0, 0)
    const2 = lambda b: (0, 0)
    return pl.pallas_call(
        functools.partial(_sample_kernel, t_new=t_new, lam_init=lam_init),
        out_shape=(jax.ShapeDtypeStruct((batch * t_new, MIX_W), BF16),
                   jax.ShapeDtypeStruct((batch * t_new, MIX_W), BF16)),
        grid=(batch,),
        in_specs=[
            pl.BlockSpec((t_new, MIX_W), row),
            pl.BlockSpec((None, t_new, MIX_W), slab),
            pl.BlockSpec((None, t_new, MIX_W), slab),
            pl.BlockSpec((None, None, N_HEADS, 2, DK_A, past), cache_t),
            pl.BlockSpec((None, None, past * N_HEADS, HEAD_W), cache),
            pl.BlockSpec((t_new, MIX_W), row),
            pl.BlockSpec((t_new, MIX_W), row),
            pl.BlockSpec((t_new, MIX_W), row),
            pl.BlockSpec((None, None, band_past * N_HEADS, HEAD_W), cache),
            pl.BlockSpec((None, None, band_past * N_HEADS, HEAD_W), cache),
            pl.BlockSpec(ba_c.shape, const3),
            pl.BlockSpec(ba_n.shape, const3),
            pl.BlockSpec(bb_c.shape, const3),
            pl.BlockSpec(bb_n.shape, const3),
            pl.BlockSpec((4, DK_A), const2),
            pl.BlockSpec((1, HEAD_W), const2),
        ],
        out_specs=(pl.BlockSpec((t_new, MIX_W), row), pl.BlockSpec((t_new, MIX_W), row)),
        compiler_params=_cparams(("arbitrary",)),
        name="sample_mix",
    )(qa, ka_all, va_all, cache_a_kt, cache_a_v, qb, kb, vb, cache_b_k, cache_b_v,
      ba_c, ba_n, bb_c, bb_n, lam_vecs, g)


def _mixout_kernel(oa_ref, ob_ref, sga_ref, sgb_ref, x_ref, gm_ref, woa_ref, wob_ref, wout_ref,
                   lg_ref, lb_ref, o_ref, *, gpt, alpha):
    a = jnp.dot(oa_ref[...], woa_ref[...], preferred_element_type=F32)
    b = jnp.dot(ob_ref[...], wob_ref[...], preferred_element_type=F32)
    merged = sga_ref[...].astype(F32) * a + sgb_ref[...].astype(F32) * b
    y = jnp.dot(merged.astype(BF16), wout_ref[...], preferred_element_type=F32)
    z = alpha * x_ref[...] + _gate(y, gm_ref[...], gpt)
    o_ref[...] = _ln(z) * lg_ref[...] + lb_ref[...]


def _mixout_call(oa, ob, sga, sgb, x, mod_l, w_oa, w_ob, w_out, ln_g, ln_b,
                 *, layer, rows_per_group, group0, bm, alpha):
    m, d = x.shape
    gpt = max(1, bm // rows_per_group)
    row = lambda i: (i, 0)
    const = lambda i: (0, 0)
    weight = lambda i: (layer, 0, 0)
    resident = pl.Buffered(1)
    return pl.pallas_call(
        functools.partial(_mixout_kernel, gpt=gpt, alpha=alpha),
        out_shape=jax.ShapeDtypeStruct((m, d), F32),
        grid=(m // bm,),
        in_specs=[
            pl.BlockSpec((bm, MIX_W), row),
            pl.BlockSpec((bm, MIX_W), row),
            pl.BlockSpec((bm, d), row),
            pl.BlockSpec((bm, d), row),
            pl.BlockSpec((bm, d), row),
            _mod_spec(gpt, rows_per_group, bm, group0, 2, d),
            pl.BlockSpec((None, MIX_W, d), weight, pipeline_mode=resident),
            pl.BlockSpec((None, MIX_W, d), weight, pipeline_mode=resident),
            pl.BlockSpec((None, d, d), weight, pipeline_mode=resident),
            pl.BlockSpec((1, d), const),
            pl.BlockSpec((1, d), const),
        ],
        out_specs=pl.BlockSpec((bm, d), row),
        compiler_params=_cparams(("arbitrary",)),
        name="mixout",
    )(oa, ob, sga, sgb, x, mod_l, w_oa, w_ob, w_out, ln_g, ln_b)


def _ffn_kernel(x_ref, sc_ref, sh_ref, gf_ref, w1_ref, w3_ref, w2_ref, lg_ref, lb_ref, o_ref,
                h_scr, *, gpt, alpha):
    kf = pl.program_id(1)
    bm = h_scr.shape[0]
    chunk = min(bm, LN_CHUNK_ROWS)
    gpc = max(1, gpt * chunk // bm)

    def chunks():
        for r in range(bm // chunk):
            g0 = r * gpc if gpt > 1 else 0
            yield slice(r * chunk, (r + 1) * chunk), slice(g0, g0 + gpc)

    @pl.when(kf == 0)
    def _():
        for rows, grp in chunks():
            h = _modulate(_ln(x_ref[rows, :]), sc_ref[grp], sh_ref[grp], gpc)
            h_scr[rows, :] = h.astype(BF16)
            o_ref[rows, :] = jnp.zeros((chunk, o_ref.shape[1]), F32)

    sub = min(bm, FFN_SUB_ROWS)
    ab = []
    for r in range(bm // sub):
        h = h_scr[r * sub:(r + 1) * sub, :]
        ab.append((jnp.dot(h, w1_ref[...], preferred_element_type=F32),
                   jnp.dot(h, w3_ref[...], preferred_element_type=F32)))
    for r, (a, b) in enumerate(ab):
        u = (a * _sigmoid(a)) * b
        o_ref[r * sub:(r + 1) * sub, :] += jnp.dot(u.astype(BF16), w2_ref[...], preferred_element_type=F32)

    @pl.when(kf == pl.num_programs(1) - 1)
    def _():
        for rows, grp in chunks():
            z = alpha * x_ref[rows, :] + _gate(o_ref[rows, :], gf_ref[grp], gpc)
            o_ref[rows, :] = _ln(z) * lg_ref[...] + lb_ref[...]


def _ffn_call(x, mod_l, w1, w3, w2, ln_g, ln_b, *, layer, rows_per_group, group0, bm, tf, alpha):
    m, d = x.shape
    dff = w1.shape[2]
    gpt = max(1, bm // rows_per_group)
    return pl.pallas_call(
        functools.partial(_ffn_kernel, gpt=gpt, alpha=alpha),
        out_shape=jax.ShapeDtypeStruct((m, d), F32),
        grid=(m // bm, dff // tf),
        in_specs=[
            pl.BlockSpec((bm, d), lambda i, k: (i, 0)),
            _mod_spec(gpt, rows_per_group, bm, group0, 4, d),
            _mod_spec(gpt, rows_per_group, bm, group0, 3, d),
            _mod_spec(gpt, rows_per_group, bm, group0, 5, d),
            pl.BlockSpec((None, d, tf), lambda i, k: (layer, 0, k)),
            pl.BlockSpec((None, d, tf), lambda i, k: (layer, 0, k)),
            pl.BlockSpec((None, tf, d), lambda i, k: (layer, k, 0)),
            pl.BlockSpec((1, d), lambda i, k: (0, 0)),
            pl.BlockSpec((1, d), lambda i, k: (0, 0)),
        ],
        out_specs=pl.BlockSpec((bm, d), lambda i, k: (i, 0)),
        scratch_shapes=[pltpu.VMEM((bm, d), BF16)],
        compiler_params=_cparams(("arbitrary", "arbitrary")),
        name="ffn",
    )(x, mod_l, mod_l, mod_l, w1, w3, w2, ln_g, ln_b)


def _ff_tile(dff):
    for tf in (512, 256, 128):
        if dff % tf == 0:
            return tf
    raise ValueError(f"unsupported FFN width {dff}")


def kernel(x_prompt, x_sample, cache_a_k, cache_a_v, cache_b_k, cache_b_v, c_prompt, c_sample,
           w_mod, b_mod, w_in, lambda_q1, lambda_k1, lambda_q2, lambda_k2, subln_g, t5_bias,
           rel_bias, w_oa, w_ob, w_out, ln1_g, ln1_b, w1, w3, w2, ln2_g, ln2_b):
    depth = w_mod.shape[0]
    batch, seq, d = x_prompt.shape
    dec_batch, t_new, _ = x_sample.shape
    past = cache_a_k.shape[2]
    band_past = cache_b_k.shape[2]
    assert past % CHUNK == 0 and t_new <= CHUNK and band_past == BAND_REACH and past >= BAND_REACH
    assert seq % TQ_A == 0 and seq % (TQ_B * UNITS_B) == 0 and seq >= BAND_REACH and d % 128 == 0
    alpha = (2 * depth) ** 0.25
    prompt_band = min(BAND_REACH, seq)

    bm_p = min(512, seq)
    bm_s = min(512, dec_batch * t_new)
    bm_in = min(1024, seq)
    tn = math.gcd(MIX_W // 2, d)
    tf = _ff_tile(w1.shape[2])

    c_all = jnp.concatenate([c_sample, c_prompt], axis=0)
    mod = _mod_call(c_all, w_mod, b_mod).reshape(depth, dec_batch + batch, 1, 6 * d)

    t5_tab = _t5_table(t5_bias)
    ta_tiles = _attn_a_bias_tiles(t5_tab, TQ_A)
    q_pos_s = past + np.arange(t_new)
    rel_a = np.arange(past + t_new)[None, :] - q_pos_s[:, None]
    ba_s = _lookup(t5_tab, _t5_bucket(rel_a))
    kb_pos_s = past - band_past + np.arange(band_past + t_new)
    rel_b = np.clip(kb_pos_s[None, :] - q_pos_s[:, None], -REL_CLIP, REL_CLIP) + REL_CLIP

    cakt = jnp.transpose(cache_a_k, (0, 1, 3, 4, 5, 2))
    cav = cache_a_v.reshape(depth, dec_batch, past * N_HEADS, HEAD_W)
    cbk = cache_b_k.reshape(depth, dec_batch, band_past * N_HEADS, HEAD_W)
    cbv = cache_b_v.reshape(depth, dec_batch, band_past * N_HEADS, HEAD_W)

    w_in_b = _to_bf16(w_in)
    w_mix_b = (_to_bf16(w_oa), _to_bf16(w_ob), _to_bf16(w_out))
    w_ffn_b = (_to_bf16(w1), _to_bf16(w3), _to_bf16(w2))

    xp = x_prompt.reshape(batch * seq, d)
    xs = x_sample.reshape(dec_batch * t_new, d)
    kv_p, kv_s = (), ()
    bk_p, bv_p, bk_s, bv_s = [], [], [], []
    for l in range(depth):
        lam_init = 0.8 - 0.6 * math.exp(-0.3 * l)
        lam_vecs = jnp.stack([lambda_q1[l], lambda_k1[l], lambda_q2[l], lambda_k2[l]]).astype(F32)
        g = subln_g[l].reshape(1, HEAD_W).astype(F32)
        mod_l = mod[l:l + 1]
        ln1 = (ln1_g[l].reshape(1, d), ln1_b[l].reshape(1, d))
        ln2 = (ln2_g[l].reshape(1, d), ln2_b[l].reshape(1, d))
        rel_tab = rel_bias[l] * LOG2E
        bb_tile = _attn_b_bias_tile(rel_tab, TQ_B, WIN_B)
        bb_s = _lookup(rel_tab, rel_b)

        grp = dict(rows_per_group=seq, group0=dec_batch, bm=bm_p)
        p = _inproj_call(xp, mod_l, w_in_b, kv_p, layer=l, depth=depth, tn=tn, key_major=True,
                         **dict(grp, bm=bm_in))
        kv_p = (p["ka_all"], p["va_all"])
        oa = _attn_a_call(p["qa"], p["ka"], p["va"], ta_tiles, lam_vecs, g,
                          batch=batch, seq=seq, lam_init=lam_init)
        ob = _attn_b_call(p["qb"], p["kb"], p["vb"], bb_tile, batch=batch, seq=seq)
        xp = _mixout_call(oa, ob, p["sga"], p["sgb"], xp, mod_l, *w_mix_b, *ln1,
                          layer=l, alpha=alpha, **grp)
        xp = _ffn_call(xp, mod_l, *w_ffn_b, *ln2, layer=l, tf=tf, alpha=alpha, **dict(grp, bm=bm_in))
        bk_p.append(p["kb"].reshape(batch, seq, N_HEADS, HEAD_W)[:, seq - prompt_band:])
        bv_p.append(p["vb"].reshape(batch, seq, N_HEADS, HEAD_W)[:, seq - prompt_band:])

        grp = dict(rows_per_group=t_new, group0=0, bm=bm_s)
        p = _inproj_call(xs, mod_l, w_in_b, kv_s, layer=l, depth=depth, tn=tn, key_major=False, **grp)
        kv_s = (p["ka_all"], p["va_all"])
        oa, ob = _sample_call(l, p["qa"], p["ka_all"], p["va_all"], cakt, cav, p["qb"], p["kb"],
                              p["vb"], cbk, cbv,
                              ba_s[:, :, :past], ba_s[:, :, past:], bb_s[:, :, :band_past],
                              bb_s[:, :, band_past:], lam_vecs, g,
                              batch=dec_batch, t_new=t_new, lam_init=lam_init)
        xs = _mixout_call(oa, ob, p["sga"], p["sgb"], xs, mod_l, *w_mix_b, *ln1,
                          layer=l, alpha=alpha, **grp)
        xs = _ffn_call(xs, mod_l, *w_ffn_b, *ln2, layer=l, tf=tf, alpha=alpha, **grp)
        bk_s.append(p["kb"].reshape(dec_batch, t_new, N_HEADS, HEAD_W))
        bv_s.append(p["vb"].reshape(dec_batch, t_new, N_HEADS, HEAD_W))

    a_k_prompt = jnp.transpose(kv_p[0].reshape(depth, batch, N_HEADS, 2, DK_A, seq), (0, 1, 5, 2, 3, 4))
    return (xp.reshape(batch, seq, d), xs.reshape(dec_batch, t_new, d),
            a_k_prompt,
            kv_p[1].reshape(depth, batch, seq, N_HEADS, HEAD_W),
            jnp.stack(bk_p), jnp.stack(bv_p),
            kv_s[0].reshape(depth, dec_batch, t_new, N_HEADS, 2, DK_A),
            kv_s[1].reshape(depth, dec_batch, t_new, N_HEADS, HEAD_W),
            jnp.stack(bk_s), jnp.stack(bv_s))
```

```python
import functools
import math

import jax
import jax.numpy as jnp
import numpy as np
from jax import lax
from jax.experimental import pallas as pl
from jax.experimental.pallas import tpu as pltpu

F32 = jnp.float32
BF16 = jnp.bfloat16

CHUNK = 64
N_HEADS = 8
HEAD_W = 128
DK_A = 64
BAND_PREV_CHUNKS = 8
BAND_REACH = BAND_PREV_CHUNKS * CHUNK
REL_CLIP = 128
T5_BUCKETS = 32
T5_MAX_DIST = 128
NEG = -1e30
LN_EPS = 1e-5
MIX_W = N_HEADS * HEAD_W
LOG2E = math.log2(math.e)
QA_SCALE = DK_A ** -0.5 * LOG2E
QB_SCALE = HEAD_W ** -0.5 * LOG2E

V7X_VMEM_BYTES = 64 * 1024 * 1024
VMEM_LIMIT = V7X_VMEM_BYTES - 3 * 1024 * 1024

ROW_TILE = 512
TALL_ROW_TILE = 1024
COL_TILE = 512
FFN_TILES = (512, 256, 128)
MOD_COL_TILE = 1024
TQ_A = 256
HEADS_PER_STEP_A = 8
FFN_SUB_ROWS = 512
LN_CHUNK_ROWS = 256
CAST_BLOCK_ELEMS = 1 << 21
ONES_ROWS = 16
TQ_B = 128
WIN_B = BAND_REACH + TQ_B
UNITS_B = 16


def _cparams(sem):
    return pltpu.CompilerParams(dimension_semantics=sem, vmem_limit_bytes=VMEM_LIMIT)


def _sigmoid(x):
    return 1.0 / (1.0 + jnp.exp(-x))


def _ln(x):
    mu = jnp.mean(x, axis=-1, keepdims=True)
    xc = x - mu
    var = jnp.mean(xc * xc, axis=-1, keepdims=True)
    return xc * lax.rsqrt(var + LN_EPS)


def _modulate(y, scale, shift, gpt):
    if gpt == 1:
        return y * (1.0 + scale[0]) + shift[0]
    bm, d = y.shape
    y3 = y.reshape(gpt, bm // gpt, d)
    return (y3 * (1.0 + scale) + shift).reshape(bm, d)


def _gate(y, gate, gpt):
    if gpt == 1:
        return y * gate[0]
    bm, d = y.shape
    return (y.reshape(gpt, bm // gpt, d) * gate).reshape(bm, d)


def _nt_dot(a, b):
    return lax.dot_general(a, b, (((1,), (1,)), ((), ())), preferred_element_type=F32)


def _mod_kernel(c_ref, w_ref, b_ref, o_ref):
    c = c_ref[...]
    s = (c * _sigmoid(c)).astype(BF16)
    o_ref[...] = jnp.dot(s, w_ref[...].astype(BF16), preferred_element_type=F32) + b_ref[...]


def _mod_call(c_all, w_mod, b_mod):
    depth, d, n = w_mod.shape
    g = c_all.shape[0]
    tn = math.gcd(n, MOD_COL_TILE)
    return pl.pallas_call(
        _mod_kernel,
        out_shape=jax.ShapeDtypeStruct((depth, g, n), F32),
        grid=(depth, n // tn),
        in_specs=[
            pl.BlockSpec((g, d), lambda l, j: (0, 0)),
            pl.BlockSpec((None, d, tn), lambda l, j: (l, 0, j)),
            pl.BlockSpec((None, 1, tn), lambda l, j: (l, 0, j)),
        ],
        out_specs=pl.BlockSpec((None, g, tn), lambda l, j: (l, 0, j)),
        compiler_params=_cparams(("arbitrary", "arbitrary")),
        name="mod",
    )(c_all, w_mod, b_mod.reshape(depth, 1, n))


def _cast_kernel(w_ref, o_ref):
    o_ref[...] = w_ref[...].astype(BF16)


def _to_bf16(w):
    depth, r, c = w.shape
    br = r
    while br % 2 == 0 and br > 16 and br * c > CAST_BLOCK_ELEMS:
        br //= 2
    return pl.pallas_call(
        _cast_kernel,
        out_shape=jax.ShapeDtypeStruct(w.shape, BF16),
        grid=(depth, r // br),
        in_specs=[pl.BlockSpec((None, br, c), lambda l, i: (l, i, 0))],
        out_specs=pl.BlockSpec((None, br, c), lambda l, i: (l, i, 0)),
        compiler_params=_cparams(("arbitrary", "arbitrary")),
        name="to_bf16",
    )(w)


def _inproj_segments(d, tn):
    widths = (MIX_W,) * 6 + (d, d)
    segs, lo = [], 0
    for w in widths:
        assert w % tn == 0
        segs.append((lo, w // tn))
        lo += w // tn
    return tuple(segs)


def _inproj_write(ref, kind, acc):
    if kind == "f32":
        ref[...] = acc
    elif kind == "bf16":
        ref[...] = acc.astype(BF16)
    elif kind == "qa":
        ref[...] = (acc * QA_SCALE).astype(BF16)
    elif kind == "qb":
        ref[...] = (acc * QB_SCALE).astype(BF16)
    elif kind == "gate":
        ref[...] = _sigmoid(acc).astype(BF16)
    elif kind == "key_major":
        for h in range(ref.shape[0]):
            ref[h] = acc[:, h * HEAD_W:(h + 1) * HEAD_W].T
    else:
        raise ValueError(kind)


def _inproj_kernel(*refs, gpt, segs, plan, n_in, n_out):
    x_ref, sc_ref, sh_ref, w_ref = refs[:4]
    out_refs = refs[n_in:n_in + n_out]
    h_scr = refs[n_in + n_out]
    j = pl.program_id(1)

    @pl.when(j == 0)
    def _():
        h = _modulate(_ln(x_ref[...]), sc_ref[...], sh_ref[...], gpt)
        h_scr[...] = h.astype(BF16)

    for (lo, cnt), writers in zip(segs, plan):
        @pl.when((j >= lo) & (j < lo + cnt))
        def _(writers=writers):
            acc = jnp.dot(h_scr[...], w_ref[...], preferred_element_type=F32)
            for idx, kind in writers:
                _inproj_write(out_refs[idx], kind, acc)


def _mod_spec(gpt, rows_per_group, bm, group0, which, d):
    if gpt == 1:
        tiles_per_group = rows_per_group // bm
        return pl.BlockSpec((None, 1, 1, d), lambda i, *_: (0, group0 + i // tiles_per_group, 0, which))
    return pl.BlockSpec((None, gpt, 1, d), lambda i, *_: (0, group0 // gpt + i, 0, which))


def _inproj_call(x, mod_l, w_in, kv_carry, *, layer, depth, rows_per_group, group0, bm, tn, key_major):
    m, d = x.shape
    n = w_in.shape[2]
    gpt = max(1, bm // rows_per_group)
    segs = _inproj_segments(d, tn)
    assert n == tn * (segs[-1][0] + segs[-1][1])
    n_carry = len(kv_carry)

    def col(seg):
        lo, cnt = seg
        return lambda j: jnp.clip(j - lo, 0, cnt - 1)

    def rows(seg, width, dtype):
        c = col(seg)
        return (jax.ShapeDtypeStruct((m, width), dtype), pl.BlockSpec((bm, tn), lambda i, j: (i, c(j))))

    def slab(seg):
        c = col(seg)
        return (jax.ShapeDtypeStruct((depth, m, MIX_W), F32),
                pl.BlockSpec((None, bm, tn), lambda i, j: (layer, i, c(j))))

    def key_major_slab(seg):
        c = col(seg)
        tiles_per_seq = rows_per_group // bm
        return (jax.ShapeDtypeStruct((depth, m // rows_per_group, N_HEADS, HEAD_W, rows_per_group), F32),
                pl.BlockSpec((None, None, tn // HEAD_W, HEAD_W, bm),
                             lambda i, j: (layer, i // tiles_per_seq, c(j), 0, i % tiles_per_seq)))

    outs = {"qa": rows(segs[0], MIX_W, BF16)}
    if key_major:
        assert rows_per_group % bm == 0
        outs["ka"] = rows(segs[1], MIX_W, BF16)
        outs["ka_all"] = key_major_slab(segs[1])
        ka_writers = (("ka", "bf16"), ("ka_all", "key_major"))
        outs["va"] = rows(segs[2], MIX_W, BF16)
        va_writers = (("va_all", "f32"), ("va", "bf16"))
    else:
        outs["ka_all"] = slab(segs[1])
        ka_writers = (("ka_all", "f32"),)
        va_writers = (("va_all", "f32"),)
    outs["va_all"] = slab(segs[2])
    outs["qb"] = rows(segs[3], MIX_W, BF16)
    outs["kb"] = rows(segs[4], MIX_W, F32)
    outs["vb"] = rows(segs[5], MIX_W, F32)
    outs["sga"] = rows(segs[6], d, BF16)
    outs["sgb"] = rows(segs[7], d, BF16)
    names = list(outs)
    plan = tuple(tuple((names.index(nm), kind) for nm, kind in writers) for writers in (
        (("qa", "qa"),), ka_writers, va_writers, (("qb", "qb"),),
        (("kb", "f32"),), (("vb", "f32"),), (("sga", "gate"),), (("sgb", "gate"),)))
    n_in = 4 + n_carry
    aliases = {4 + c: names.index(nm) for c, nm in enumerate(("ka_all", "va_all")[:n_carry])}
    res = pl.pallas_call(
        functools.partial(_inproj_kernel, gpt=gpt, segs=segs, plan=plan, n_in=n_in, n_out=len(names)),
        out_shape=tuple(outs[nm][0] for nm in names),
        grid=(m // bm, n // tn),
        in_specs=[
            pl.BlockSpec((bm, d), lambda i, j: (i, 0)),
            _mod_spec(gpt, rows_per_group, bm, group0, 1, d),
            _mod_spec(gpt, rows_per_group, bm, group0, 0, d),
            pl.BlockSpec((None, d, tn), lambda i, j: (layer, 0, j)),
        ] + [pl.BlockSpec(memory_space=pl.ANY)] * n_carry,
        out_specs=tuple(outs[nm][1] for nm in names),
        scratch_shapes=[pltpu.VMEM((bm, d), BF16)],
        input_output_aliases=aliases,
        compiler_params=_cparams(("arbitrary", "arbitrary")),
        name="inproj",
    )(x, mod_l, mod_l, w_in, *kv_carry)
    return dict(zip(names, res))


def _t5_bucket(rel):
    nb = T5_BUCKETS // 2
    max_exact = nb // 2
    rel = np.asarray(rel, np.int32)
    bucket = (rel > 0).astype(np.int32) * nb
    n = np.abs(rel)
    nf = np.maximum(n, 1).astype(np.float32)
    ratio = np.log(nf / np.float32(max_exact)) / np.float32(math.log(T5_MAX_DIST / max_exact))
    large = max_exact + (ratio * np.float32(nb - max_exact)).astype(np.int32)
    large = np.minimum(large, nb - 1)
    return bucket + np.where(n < max_exact, n, large)


def _lookup(table, idx):
    n = table.shape[0]
    flat = jnp.asarray(np.asarray(idx, np.int32).reshape(-1))
    onehot = (flat[None, :] == jnp.arange(n, dtype=jnp.int32)[:, None]).astype(F32)
    out = lax.dot_general(table.astype(F32), onehot, (((0,), (0,)), ((), ())),
                          precision=lax.Precision.HIGHEST, preferred_element_type=F32)
    return out.reshape((table.shape[1],) + tuple(np.shape(idx)))


def _t5_table(t5_bias):
    far = t5_bias[int(_t5_bucket(-T5_MAX_DIST))]
    return (t5_bias - far[None, :]) * LOG2E


def _attn_a_bias_tiles(t5_tab, t):
    assert t >= T5_MAX_DIST
    k = np.arange(t)[:, None]
    q = np.arange(t)[None, :]
    visible = (k // CHUNK) <= (q // CHUNK)
    idx = np.stack([_t5_bucket(k - q), _t5_bucket(k - t - q)])
    vis = np.stack([visible, np.ones_like(visible)])
    return jnp.where(vis[None], _lookup(t5_tab, idx), NEG)


def _attn_b_bias_tile(rel_tab, tq, win):
    kw = np.arange(win)[:, None] - BAND_REACH
    q = np.arange(tq)[None, :]
    rel = np.clip(kw - q, -REL_CLIP, REL_CLIP) + REL_CLIP
    dchunk = q // CHUNK - np.floor_divide(kw, CHUNK)
    visible = (dchunk >= 0) & (dchunk <= BAND_PREV_CHUNKS)
    return jnp.where(visible[None], _lookup(rel_tab, rel), NEG)


def _diff_lambda(lam_ref, lam_init):
    lv = lam_ref[...]
    s1 = jnp.sum(lv[0:1] * lv[1:2], axis=-1, keepdims=True)
    s2 = jnp.sum(lv[2:3] * lv[3:4], axis=-1, keepdims=True)
    return jnp.exp(s1) - jnp.exp(s2) + lam_init


def _stack_diff_queries(q):
    lane = lax.broadcasted_iota(jnp.int32, q.shape, 1)
    zero = jnp.zeros_like(q)
    return jnp.concatenate([jnp.where(lane < DK_A, q, zero), jnp.where(lane >= DK_A, q, zero)], axis=0)


def _stack_diff_queries_t(q):
    qt = q.astype(F32).T.astype(BF16)
    row = lax.broadcasted_iota(jnp.int32, qt.shape, 0)
    zero = jnp.zeros_like(qt)
    return jnp.concatenate([jnp.where(row < DK_A, qt, zero), jnp.where(row >= DK_A, qt, zero)], axis=1)


def _attn_a_kernel(q_ref, k_ref, v_ref, bias_ref, lam_ref, g_ref, o_ref,
                   vt_scr, s_scr, acc_scr, m_scr, *, t, nh, lam_init):
    qi = pl.program_id(2)
    nkb = vt_scr.shape[1]
    heads = range(nh)

    @pl.when(qi == 0)
    def _():
        sub = lax.broadcasted_iota(jnp.int32, (ONES_ROWS, t), 0)
        ones_rows = jnp.where(sub == 0, 1.0, 0.0).astype(BF16)
        for hh in heads:
            cols = slice(hh * HEAD_W, (hh + 1) * HEAD_W)
            for kb in range(nkb):
                vt_scr[hh, kb, 0:HEAD_W, :] = v_ref[kb * t:(kb + 1) * t, cols].astype(F32).T.astype(BF16)
                vt_scr[hh, kb, HEAD_W:HEAD_W + ONES_ROWS, :] = ones_rows

    q2t = [_stack_diff_queries_t(q_ref[:, hh * HEAD_W:(hh + 1) * HEAD_W]) for hh in heads]
    acc_scr[...] = jnp.zeros(acc_scr.shape, F32)

    def scores(hh, kb):
        start = pl.multiple_of(kb * t, t)
        s_scr[hh] = jnp.dot(k_ref[pl.ds(start, t), hh * HEAD_W:(hh + 1) * HEAD_W], q2t[hh],
                            preferred_element_type=F32)

    def step(kb, ms, bias_idx, prefetch):
        out = []
        for hh in heads:
            s = s_scr[hh]
            if bias_idx is not None:
                bias = bias_ref[hh, bias_idx]
                s = s + jnp.concatenate([bias, bias], axis=1)
            m_new = jnp.maximum(ms[hh], jnp.max(s, axis=0, keepdims=True))
            alpha = jnp.exp2(ms[hh] - m_new)
            e = jnp.exp2(s - m_new).astype(BF16)
            pv = jnp.dot(vt_scr[hh, kb], e, preferred_element_type=F32)
            if prefetch:
                scores(hh, kb + 1)
            acc_scr[hh] = alpha * acc_scr[hh] + pv
            out.append(m_new)
        return tuple(out)

    def load_m():
        return tuple(m_scr[hh] for hh in heads)

    def store_m(ms):
        for hh in heads:
            m_scr[hh] = ms[hh]

    for hh in heads:
        scores(hh, 0)
    m0 = tuple(jnp.full((1, 2 * t), -jnp.inf, F32) for _ in heads)
    n_far = jnp.maximum(qi - 1, 0)
    store_m(lax.fori_loop(
        0, lax.shift_right_logical(n_far, 1),
        lambda p, ms: step(2 * p + 1, step(2 * p, ms, None, True), None, True), m0))

    def finish(steps):
        ms = load_m()
        for kb, bias_idx, prefetch in steps:
            ms = step(kb, ms, bias_idx, prefetch)
        lam = _diff_lambda(lam_ref, lam_init)
        for hh in heads:
            o2 = acc_scr[hh, 0:HEAD_W, :] / acc_scr[hh, HEAD_W:HEAD_W + 1, :]
            o = o2[:, :t] - lam * o2[:, t:]
            o = o * lax.rsqrt(jnp.mean(o * o, axis=0, keepdims=True) + LN_EPS)
            o_ref[:, hh * HEAD_W:(hh + 1) * HEAD_W] = (o.T * g_ref[...] * (1.0 - lam_init)).astype(BF16)

    previous = (qi - 1, 1, True)
    diagonal = (qi, 0, False)
    odd_far = lax.rem(n_far, 2) == 1

    @pl.when(qi == 0)
    def _():
        finish([diagonal])

    @pl.when((qi >= 1) & jnp.logical_not(odd_far))
    def _():
        finish([previous, diagonal])

    @pl.when((qi >= 1) & odd_far)
    def _():
        finish([(n_far - 1, None, True), previous, diagonal])


def _attn_a_call(qa, ka, va, bias_tiles, lam_vecs, g, *, batch, seq, lam_init):
    t, nh = TQ_A, HEADS_PER_STEP_A
    nq = seq // t
    w = nh * HEAD_W
    rare = pl.Buffered(1)
    return pl.pallas_call(
        functools.partial(_attn_a_kernel, t=t, nh=nh, lam_init=lam_init),
        out_shape=jax.ShapeDtypeStruct((batch * seq, MIX_W), BF16),
        grid=(batch, N_HEADS // nh, nq),
        in_specs=[
            pl.BlockSpec((t, w), lambda b, h, qi: (b * nq + qi, h)),
            pl.BlockSpec((seq, w), lambda b, h, qi: (b, h)),
            pl.BlockSpec((seq, w), lambda b, h, qi: (b, h), pipeline_mode=rare),
            pl.BlockSpec((nh, 2, t, t), lambda b, h, qi: (h, 0, 0, 0), pipeline_mode=rare),
            pl.BlockSpec((4, DK_A), lambda b, h, qi: (0, 0)),
            pl.BlockSpec((1, HEAD_W), lambda b, h, qi: (0, 0)),
        ],
        out_specs=pl.BlockSpec((t, w), lambda b, h, qi: (b * nq + qi, h)),
        scratch_shapes=[
            pltpu.VMEM((nh, seq // t, HEAD_W + ONES_ROWS, t), BF16),
            pltpu.VMEM((nh, t, 2 * t), F32),
            pltpu.VMEM((nh, HEAD_W + ONES_ROWS, 2 * t), F32),
            pltpu.VMEM((nh, 1, 2 * t), F32),
        ],
        compiler_params=_cparams(("arbitrary", "arbitrary", "arbitrary")),
        name="attn_a",
    )(qa, ka, va, bias_tiles, lam_vecs, g)


def _attn_b_kernel(q_ref, k_ref, v_ref, bias_ref, o_ref, k_scr, vt_scr, *, tq, win, seq, units):
    qi = pl.program_id(2)
    pad_blocks = BAND_REACH // tq
    win_blocks = win // tq

    @pl.when(qi == 0)
    def _():
        k_scr[0:BAND_REACH, :] = jnp.zeros((BAND_REACH, HEAD_W), BF16)
        k_scr[BAND_REACH:BAND_REACH + seq, :] = k_ref[...].astype(BF16)
        for blk in range(pad_blocks):
            vt_scr[blk] = jnp.zeros((HEAD_W, tq), BF16)
        for blk in range(seq // tq):
            vt_scr[pad_blocks + blk] = v_ref[blk * tq:(blk + 1) * tq, :].T.astype(BF16)

    early_steps = -(-BAND_REACH // (units * tq))

    def body(early):
        bias = bias_ref[...]
        raw = []
        for u in range(units):
            start = pl.multiple_of((qi * units + u) * tq, tq)
            raw.append(_nt_dot(k_scr[pl.ds(start, win), :], q_ref[u * tq:(u + 1) * tq, :]))
        for u in range(units):
            blk0 = qi * units + u
            s = raw[u] + bias
            if early and (early_steps > 1 or u * tq < BAND_REACH):
                row = lax.broadcasted_iota(jnp.int32, (win, tq), 0)
                s = jnp.where(row + blk0 * tq >= BAND_REACH, s, NEG)
            m = jnp.max(s, axis=0, keepdims=True)
            e = jnp.exp2(s - m)
            l = jnp.sum(e, axis=0, keepdims=True)
            vt_win = jnp.concatenate([vt_scr[blk0 + w] for w in range(win_blocks)], axis=1)
            acc = jnp.dot(vt_win, e.astype(BF16), preferred_element_type=F32)
            o_ref[u * tq:(u + 1) * tq, :] = (acc / l).T.astype(BF16)

    @pl.when(qi < early_steps)
    def _():
        body(True)

    @pl.when(qi >= early_steps)
    def _():
        body(False)


def _attn_b_call(qb, kb, vb, bias_tile, *, batch, seq):
    tq, win, units = TQ_B, WIN_B, UNITS_B
    rows = tq * units
    nq = seq // rows
    return pl.pallas_call(
        functools.partial(_attn_b_kernel, tq=tq, win=win, seq=seq, units=units),
        out_shape=jax.ShapeDtypeStruct((batch * seq, MIX_W), BF16),
        grid=(batch, N_HEADS, nq),
        in_specs=[
            pl.BlockSpec((rows, HEAD_W), lambda b, h, qi: (b * nq + qi, h)),
            pl.BlockSpec((seq, HEAD_W), lambda b, h, qi: (b, h)),
            pl.BlockSpec((seq, HEAD_W), lambda b, h, qi: (b, h)),
            pl.BlockSpec((None, win, tq), lambda b, h, qi: (h, 0, 0)),
        ],
        out_specs=pl.BlockSpec((rows, HEAD_W), lambda b, h, qi: (b * nq + qi, h)),
        scratch_shapes=[
            pltpu.VMEM((BAND_REACH + seq, HEAD_W), BF16),
            pltpu.VMEM(((BAND_REACH + seq) // tq, HEAD_W, tq), BF16),
        ],
        compiler_params=_cparams(("arbitrary", "arbitrary", "arbitrary")),
        name="attn_b",
    )(qb, kb, vb, bias_tile)


def _softmax_pv(parts):
    m = None
    for s, _ in parts:
        pm = jnp.max(s, axis=-1, keepdims=True)
        m = pm if m is None else jnp.maximum(m, pm)
    l = None
    o = None
    for s, v in parts:
        e = jnp.exp2(s - m)
        pl_ = jnp.sum(e, axis=-1, keepdims=True)
        po = jnp.dot(e.astype(BF16), v, preferred_element_type=F32)
        l = pl_ if l is None else l + pl_
        o = po if o is None else o + po
    return o / l


def _sample_kernel(qa_ref, ka_ref, va_ref, cakt_ref, cav_ref, qb_ref, kb_ref, vb_ref, cbk_ref, cbv_ref,
                   ba_c_ref, ba_n_ref, bb_c_ref, bb_n_ref, lam_ref, g_ref, oa_ref, ob_ref,
                   *, t_new, lam_init):
    lam = _diff_lambda(lam_ref, lam_init)
    g = g_ref[...]
    past = cakt_ref.shape[-1]

    def head_rows(ref, h):
        return ref[pl.ds(h, ref.shape[0] // N_HEADS, stride=N_HEADS), :].astype(BF16)

    scores_a, scores_b = [], []
    for h in range(N_HEADS):
        cols = slice(h * HEAD_W, (h + 1) * HEAD_W)
        q2 = _stack_diff_queries(qa_ref[:, cols])
        kct = cakt_ref[h].reshape(2 * DK_A, past).astype(BF16)
        scores_a.append((jnp.dot(q2, kct, preferred_element_type=F32),
                         _nt_dot(q2, ka_ref[:, cols].astype(BF16))))
        qh = qb_ref[:, cols]
        scores_b.append((_nt_dot(qh, head_rows(cbk_ref, h)), _nt_dot(qh, kb_ref[:, cols].astype(BF16))))

    for h in range(N_HEADS):
        cols = slice(h * HEAD_W, (h + 1) * HEAD_W)
        s_c, s_n = scores_a[h]
        s_c = (s_c.reshape(2, t_new, -1) + ba_c_ref[h][None]).reshape(2 * t_new, -1)
        s_n = (s_n.reshape(2, t_new, -1) + ba_n_ref[h][None]).reshape(2 * t_new, -1)
        o2 = _softmax_pv([(s_c, head_rows(cav_ref, h)), (s_n, va_ref[:, cols].astype(BF16))])
        o = o2[:t_new] - lam * o2[t_new:]
        o = o * lax.rsqrt(jnp.mean(o * o, axis=-1, keepdims=True) + LN_EPS) * g
        oa_ref[:, cols] = (o * (1.0 - lam_init)).astype(BF16)
        s_c, s_n = scores_b[h]
        ob_ref[:, cols] = _softmax_pv([(s_c + bb_c_ref[h], head_rows(cbv_ref, h)),
                                       (s_n + bb_n_ref[h], vb_ref[:, cols].astype(BF16))]).astype(BF16)


def _sample_call(layer, qa, ka_all, va_all, cache_a_kt, cache_a_v, qb, kb, vb, cache_b_k, cache_b_v,
                 ba_c, ba_n, bb_c, bb_n, lam_vecs, g, *, batch, t_new, lam_init):
    past = cache_a_kt.shape[-1]
    band_past = cache_b_k.shape[2] // N_HEADS
    row = lambda b: (b, 0)
    slab = lambda b: (layer, b, 0)
    cache = lambda b: (layer, b, 0, 0)
    cache_t = lambda b: (layer, b, 0, 0, 0, 0)
    const3 = lambda b: (0, 0, 0)
    const2 = lambda b: (0, 0)
    return pl.pallas_call(
        functools.partial(_sample_kernel, t_new=t_new, lam_init=lam_init),
        out_shape=(jax.ShapeDtypeStruct((batch * t_new, MIX_W), BF16),
                   jax.ShapeDtypeStruct((batch * t_new, MIX_W), BF16)),
        grid=(batch,),
        in_specs=[
            pl.BlockSpec((t_new, MIX_W), row),
            pl.BlockSpec((None, t_new, MIX_W), slab),
            pl.BlockSpec((None, t_new, MIX_W), slab),
            pl.BlockSpec((None, None, N_HEADS, 2, DK_A, past), cache_t),
            pl.BlockSpec((None, None, past * N_HEADS, HEAD_W), cache),
            pl.BlockSpec((t_new, MIX_W), row),
            pl.BlockSpec((t_new, MIX_W), row),
            pl.BlockSpec((t_new, MIX_W), row),
            pl.BlockSpec((None, None, band_past * N_HEADS, HEAD_W), cache),
            pl.BlockSpec((None, None, band_past * N_HEADS, HEAD_W), cache),
            pl.BlockSpec(ba_c.shape, const3),
            pl.BlockSpec(ba_n.shape, const3),
            pl.BlockSpec(bb_c.shape, const3),
            pl.BlockSpec(bb_n.shape, const3),
            pl.BlockSpec((4, DK_A), const2),
            pl.BlockSpec((1, HEAD_W), const2),
        ],
        out_specs=(pl.BlockSpec((t_new, MIX_W), row), pl.BlockSpec((t_new, MIX_W), row)),
        compiler_params=_cparams(("arbitrary",)),
        name="sample_mix",
    )(qa, ka_all, va_all, cache_a_kt, cache_a_v, qb, kb, vb, cache_b_k, cache_b_v,
      ba_c, ba_n, bb_c, bb_n, lam_vecs, g)


def _mixout_kernel(oa_ref, ob_ref, sga_ref, sgb_ref, x_ref, gm_ref, woa_ref, wob_ref, wout_ref,
                   lg_ref, lb_ref, o_ref, *, gpt, alpha):
    a = jnp.dot(oa_ref[...], woa_ref[...], preferred_element_type=F32)
    b = jnp.dot(ob_ref[...], wob_ref[...], preferred_element_type=F32)
    merged = sga_ref[...].astype(F32) * a + sgb_ref[...].astype(F32) * b
    y = jnp.dot(merged.astype(BF16), wout_ref[...], preferred_element_type=F32)
    z = alpha * x_ref[...] + _gate(y, gm_ref[...], gpt)
    o_ref[...] = _ln(z) * lg_ref[...] + lb_ref[...]


def _mixout_call(oa, ob, sga, sgb, x, mod_l, w_oa, w_ob, w_out, ln_g, ln_b,
                 *, layer, rows_per_group, group0, bm, alpha):
    m, d = x.shape
    gpt = max(1, bm // rows_per_group)
    row = lambda i: (i, 0)
    const = lambda i: (0, 0)
    weight = lambda i: (layer, 0, 0)
    resident = pl.Buffered(1)
    return pl.pallas_call(
        functools.partial(_mixout_kernel, gpt=gpt, alpha=alpha),
        out_shape=jax.ShapeDtypeStruct((m, d), F32),
        grid=(m // bm,),
        in_specs=[
            pl.BlockSpec((bm, MIX_W), row),
            pl.BlockSpec((bm, MIX_W), row),
            pl.BlockSpec((bm, d), row),
            pl.BlockSpec((bm, d), row),
            pl.BlockSpec((bm, d), row),
            _mod_spec(gpt, rows_per_group, bm, group0, 2, d),
            pl.BlockSpec((None, MIX_W, d), weight, pipeline_mode=resident),
            pl.BlockSpec((None, MIX_W, d), weight, pipeline_mode=resident),
            pl.BlockSpec((None, d, d), weight, pipeline_mode=resident),
            pl.BlockSpec((1, d), const),
            pl.BlockSpec((1, d), const),
        ],
        out_specs=pl.BlockSpec((bm, d), row),
        compiler_params=_cparams(("arbitrary",)),
        name="mixout",
    )(oa, ob, sga, sgb, x, mod_l, w_oa, w_ob, w_out, ln_g, ln_b)


def _ffn_kernel(x_ref, sc_ref, sh_ref, gf_ref, w1_ref, w3_ref, w2_ref, lg_ref, lb_ref, o_ref,
                h_scr, *, gpt, alpha):
    kf = pl.program_id(1)
    bm = h_scr.shape[0]
    chunk = min(bm, LN_CHUNK_ROWS)
    gpc = max(1, gpt * chunk // bm)

    def chunks():
        for r in range(bm // chunk):
            g0 = r * gpc if gpt > 1 else 0
            yield slice(r * chunk, (r + 1) * chunk), slice(g0, g0 + gpc)

    @pl.when(kf == 0)
    def _():
        for rows, grp in chunks():
            h = _modulate(_ln(x_ref[rows, :]), sc_ref[grp], sh_ref[grp], gpc)
            h_scr[rows, :] = h.astype(BF16)
            o_ref[rows, :] = jnp.zeros((chunk, o_ref.shape[1]), F32)

    sub = min(bm, FFN_SUB_ROWS)
    ab = []
    for r in range(bm // sub):
        h = h_scr[r * sub:(r + 1) * sub, :]
        ab.append((jnp.dot(h, w1_ref[...], preferred_element_type=F32),
                   jnp.dot(h, w3_ref[...], preferred_element_type=F32)))
    for r, (a, b) in enumerate(ab):
        u = (a * _sigmoid(a)) * b
        o_ref[r * sub:(r + 1) * sub, :] += jnp.dot(u.astype(BF16), w2_ref[...], preferred_element_type=F32)

    @pl.when(kf == pl.num_programs(1) - 1)
    def _():
        for rows, grp in chunks():
            z = alpha * x_ref[rows, :] + _gate(o_ref[rows, :], gf_ref[grp], gpc)
            o_ref[rows, :] = _ln(z) * lg_ref[...] + lb_ref[...]


def _ffn_call(x, mod_l, w1, w3, w2, ln_g, ln_b, *, layer, rows_per_group, group0, bm, tf, alpha):
    m, d = x.shape
    dff = w1.shape[2]
    gpt = max(1, bm // rows_per_group)
    return pl.pallas_call(
        functools.partial(_ffn_kernel, gpt=gpt, alpha=alpha),
        out_shape=jax.ShapeDtypeStruct((m, d), F32),
        grid=(m // bm, dff // tf),
        in_specs=[
            pl.BlockSpec((bm, d), lambda i, k: (i, 0)),
            _mod_spec(gpt, rows_per_group, bm, group0, 4, d),
            _mod_spec(gpt, rows_per_group, bm, group0, 3, d),
            _mod_spec(gpt, rows_per_group, bm, group0, 5, d),
            pl.BlockSpec((None, d, tf), lambda i, k: (layer, 0, k)),
            pl.BlockSpec((None, d, tf), lambda i, k: (layer, 0, k)),
            pl.BlockSpec((None, tf, d), lambda i, k: (layer, k, 0)),
            pl.BlockSpec((1, d), lambda i, k: (0, 0)),
            pl.BlockSpec((1, d), lambda i, k: (0, 0)),
        ],
        out_specs=pl.BlockSpec((bm, d), lambda i, k: (i, 0)),
        scratch_shapes=[pltpu.VMEM((bm, d), BF16)],
        compiler_params=_cparams(("arbitrary", "arbitrary")),
        name="ffn",
    )(x, mod_l, mod_l, mod_l, w1, w3, w2, ln_g, ln_b)


def _ff_tile(dff):
    for tf in FFN_TILES:
        if dff % tf == 0:
            return tf
    raise ValueError(f"unsupported FFN width {dff}")


def kernel(x_prompt, x_sample, cache_a_k, cache_a_v, cache_b_k, cache_b_v, c_prompt, c_sample,
           w_mod, b_mod, w_in, lambda_q1, lambda_k1, lambda_q2, lambda_k2, subln_g, t5_bias,
           rel_bias, w_oa, w_ob, w_out, ln1_g, ln1_b, w1, w3, w2, ln2_g, ln2_b):
    depth = w_mod.shape[0]
    batch, seq, d = x_prompt.shape
    dec_batch, t_new, _ = x_sample.shape
    past = cache_a_k.shape[2]
    band_past = cache_b_k.shape[2]
    assert past % CHUNK == 0 and t_new <= CHUNK and band_past == BAND_REACH and past >= BAND_REACH
    assert seq % TQ_A == 0 and seq % (TQ_B * UNITS_B) == 0 and seq >= BAND_REACH and d % 128 == 0
    alpha = (2 * depth) ** 0.25
    prompt_band = min(BAND_REACH, seq)

    bm_p = min(ROW_TILE, seq)
    bm_s = min(ROW_TILE, dec_batch * t_new)
    bm_in = min(TALL_ROW_TILE, seq)
    tn = math.gcd(COL_TILE, d)
    tf = _ff_tile(w1.shape[2])

    c_all = jnp.concatenate([c_sample, c_prompt], axis=0)
    mod = _mod_call(c_all, w_mod, b_mod).reshape(depth, dec_batch + batch, 1, 6 * d)

    t5_tab = _t5_table(t5_bias)
    ta_tiles = _attn_a_bias_tiles(t5_tab, TQ_A)
    q_pos_s = past + np.arange(t_new)
    rel_a = np.arange(past + t_new)[None, :] - q_pos_s[:, None]
    ba_s = _lookup(t5_tab, _t5_bucket(rel_a))
    kb_pos_s = past - band_past + np.arange(band_past + t_new)
    rel_b = np.clip(kb_pos_s[None, :] - q_pos_s[:, None], -REL_CLIP, REL_CLIP) + REL_CLIP

    cakt = jnp.transpose(cache_a_k, (0, 1, 3, 4, 5, 2))
    cav = cache_a_v.reshape(depth, dec_batch, past * N_HEADS, HEAD_W)
    cbk = cache_b_k.reshape(depth, dec_batch, band_past * N_HEADS, HEAD_W)
    cbv = cache_b_v.reshape(depth, dec_batch, band_past * N_HEADS, HEAD_W)

    w_in_b = _to_bf16(w_in)
    w_mix_b = (_to_bf16(w_oa), _to_bf16(w_ob), _to_bf16(w_out))
    w_ffn_b = (_to_bf16(w1), _to_bf16(w3), _to_bf16(w2))

    xp = x_prompt.reshape(batch * seq, d)
    xs = x_sample.reshape(dec_batch * t_new, d)
    kv_p, kv_s = (), ()
    bk_p, bv_p, bk_s, bv_s = [], [], [], []
    for l in range(depth):
        lam_init = 0.8 - 0.6 * math.exp(-0.3 * l)
        lam_vecs = jnp.stack([lambda_q1[l], lambda_k1[l], lambda_q2[l], lambda_k2[l]]).astype(F32)
        g = subln_g[l].reshape(1, HEAD_W).astype(F32)
        mod_l = mod[l:l + 1]
        ln1 = (ln1_g[l].reshape(1, d), ln1_b[l].reshape(1, d))
        ln2 = (ln2_g[l].reshape(1, d), ln2_b[l].reshape(1, d))
        rel_tab = rel_bias[l] * LOG2E
        bb_tile = _attn_b_bias_tile(rel_tab, TQ_B, WIN_B)
        bb_s = _lookup(rel_tab, rel_b)

        grp = dict(rows_per_group=seq, group0=dec_batch, bm=bm_p)
        p = _inproj_call(xp, mod_l, w_in_b, kv_p, layer=l, depth=depth, tn=tn, key_major=True,
                         **dict(grp, bm=bm_in))
        kv_p = (p["ka_all"], p["va_all"])
        oa = _attn_a_call(p["qa"], p["ka"], p["va"], ta_tiles, lam_vecs, g,
                          batch=batch, seq=seq, lam_init=lam_init)
        ob = _attn_b_call(p["qb"], p["kb"], p["vb"], bb_tile, batch=batch, seq=seq)
        xp = _mixout_call(oa, ob, p["sga"], p["sgb"], xp, mod_l, *w_mix_b, *ln1,
                          layer=l, alpha=alpha, **grp)
        xp = _ffn_call(xp, mod_l, *w_ffn_b, *ln2, layer=l, tf=tf, alpha=alpha, **dict(grp, bm=bm_in))
        bk_p.append(p["kb"].reshape(batch, seq, N_HEADS, HEAD_W)[:, seq - prompt_band:])
        bv_p.append(p["vb"].reshape(batch, seq, N_HEADS, HEAD_W)[:, seq - prompt_band:])

        grp = dict(rows_per_group=t_new, group0=0, bm=bm_s)
        p = _inproj_call(xs, mod_l, w_in_b, kv_s, layer=l, depth=depth, tn=tn, key_major=False, **grp)
        kv_s = (p["ka_all"], p["va_all"])
        oa, ob = _sample_call(l, p["qa"], p["ka_all"], p["va_all"], cakt, cav, p["qb"], p["kb"],
                              p["vb"], cbk, cbv,
                              ba_s[:, :, :past], ba_s[:, :, past:], bb_s[:, :, :band_past],
                              bb_s[:, :, band_past:], lam_vecs, g,
                              batch=dec_batch, t_new=t_new, lam_init=lam_init)
        xs = _mixout_call(oa, ob, p["sga"], p["sgb"], xs, mod_l, *w_mix_b, *ln1,
                          layer=l, alpha=alpha, **grp)
        xs = _ffn_call(xs, mod_l, *w_ffn_b, *ln2, layer=l, tf=tf, alpha=alpha, **grp)
        bk_s.append(p["kb"].reshape(dec_batch, t_new, N_HEADS, HEAD_W))
        bv_s.append(p["vb"].reshape(dec_batch, t_new, N_HEADS, HEAD_W))

    a_k_prompt = jnp.transpose(kv_p[0].reshape(depth, batch, N_HEADS, 2, DK_A, seq), (0, 1, 5, 2, 3, 4))
    return (xp.reshape(batch, seq, d), xs.reshape(dec_batch, t_new, d),
            a_k_prompt,
            kv_p[1].reshape(depth, batch, seq, N_HEADS, HEAD_W),
            jnp.stack(bk_p), jnp.stack(bv_p),
            kv_s[0].reshape(depth, dec_batch, t_new, N_HEADS, 2, DK_A),
            kv_s[1].reshape(depth, dec_batch, t_new, N_HEADS, HEAD_W),
            jnp.stack(bk_s), jnp.stack(bv_s))
```

```python
import functools
import math

import jax
import jax.numpy as jnp
import numpy as np
from jax import lax
from jax.experimental import pallas as pl
from jax.experimental.pallas import tpu as pltpu

F32 = jnp.float32
BF16 = jnp.bfloat16

CHUNK = 64
N_HEADS = 8
HEAD_W = 128
DK_A = 64
BAND_PREV_CHUNKS = 8
BAND_REACH = BAND_PREV_CHUNKS * CHUNK
REL_CLIP = 128
T5_BUCKETS = 32
T5_MAX_DIST = 128
NEG = -1e30
LN_EPS = 1e-5
MIX_W = N_HEADS * HEAD_W
LOG2E = math.log2(math.e)
QA_SCALE = DK_A ** -0.5 * LOG2E
QB_SCALE = HEAD_W ** -0.5 * LOG2E

V7X_VMEM_BYTES = 64 * 1024 * 1024
VMEM_LIMIT = V7X_VMEM_BYTES - 3 * 1024 * 1024

ROW_TILE = 512
TALL_ROW_TILE = 1024
COL_TILE = 512
FFN_TILES = (512, 256, 128)
MOD_COL_TILE = 1024
TQ_A = 256
HEADS_PER_STEP_A = 8
FFN_SUB_ROWS = 512
LN_CHUNK_ROWS = 256
CAST_BLOCK_ELEMS = 1 << 21
ONES_ROWS = 16
TQ_B = 128
WIN_B = BAND_REACH + TQ_B
UNITS_B = 32


def _cparams(sem):
    return pltpu.CompilerParams(dimension_semantics=sem, vmem_limit_bytes=VMEM_LIMIT)


def _sigmoid(x):
    return 1.0 / (1.0 + jnp.exp(-x))


def _ln(x):
    mu = jnp.mean(x, axis=-1, keepdims=True)
    xc = x - mu
    var = jnp.mean(xc * xc, axis=-1, keepdims=True)
    return xc * lax.rsqrt(var + LN_EPS)


def _modulate(y, scale, shift, gpt):
    if gpt == 1:
        return y * (1.0 + scale[0]) + shift[0]
    bm, d = y.shape
    y3 = y.reshape(gpt, bm // gpt, d)
    return (y3 * (1.0 + scale) + shift).reshape(bm, d)


def _gate(y, gate, gpt):
    if gpt == 1:
        return y * gate[0]
    bm, d = y.shape
    return (y.reshape(gpt, bm // gpt, d) * gate).reshape(bm, d)


def _nt_dot(a, b):
    return lax.dot_general(a, b, (((1,), (1,)), ((), ())), preferred_element_type=F32)


def _mod_kernel(c_ref, w_ref, b_ref, o_ref):
    c = c_ref[...]
    s = (c * _sigmoid(c)).astype(BF16)
    o_ref[...] = jnp.dot(s, w_ref[...].astype(BF16), preferred_element_type=F32) + b_ref[...]


def _mod_call(c_all, w_mod, b_mod):
    depth, d, n = w_mod.shape
    g = c_all.shape[0]
    tn = math.gcd(n, MOD_COL_TILE)
    return pl.pallas_call(
        _mod_kernel,
        out_shape=jax.ShapeDtypeStruct((depth, g, n), F32),
        grid=(depth, n // tn),
        in_specs=[
            pl.BlockSpec((g, d), lambda l, j: (0, 0)),
            pl.BlockSpec((None, d, tn), lambda l, j: (l, 0, j)),
            pl.BlockSpec((None, 1, tn), lambda l, j: (l, 0, j)),
        ],
        out_specs=pl.BlockSpec((None, g, tn), lambda l, j: (l, 0, j)),
        compiler_params=_cparams(("arbitrary", "arbitrary")),
        name="mod",
    )(c_all, w_mod, b_mod.reshape(depth, 1, n))


def _cast_kernel(w_ref, o_ref):
    o_ref[...] = w_ref[...].astype(BF16)


def _to_bf16(w):
    depth, r, c = w.shape
    br = r
    while br % 2 == 0 and br > 16 and br * c > CAST_BLOCK_ELEMS:
        br //= 2
    return pl.pallas_call(
        _cast_kernel,
        out_shape=jax.ShapeDtypeStruct(w.shape, BF16),
        grid=(depth, r // br),
        in_specs=[pl.BlockSpec((None, br, c), lambda l, i: (l, i, 0))],
        out_specs=pl.BlockSpec((None, br, c), lambda l, i: (l, i, 0)),
        compiler_params=_cparams(("arbitrary", "arbitrary")),
        name="to_bf16",
    )(w)


def _inproj_segments(d, tn):
    widths = (MIX_W,) * 6 + (d, d)
    segs, lo = [], 0
    for w in widths:
        assert w % tn == 0
        segs.append((lo, w // tn))
        lo += w // tn
    return tuple(segs)


def _inproj_write(ref, kind, acc):
    if kind == "f32":
        ref[...] = acc
    elif kind == "bf16":
        ref[...] = acc.astype(BF16)
    elif kind == "qa":
        ref[...] = (acc * QA_SCALE).astype(BF16)
    elif kind == "qb":
        ref[...] = (acc * QB_SCALE).astype(BF16)
    elif kind == "gate":
        ref[...] = (0.5 * jnp.tanh(0.5 * acc) + 0.5).astype(BF16)
    elif kind == "key_major":
        for h in range(ref.shape[0]):
            ref[h] = acc[:, h * HEAD_W:(h + 1) * HEAD_W].T
    else:
        raise ValueError(kind)


def _inproj_kernel(*refs, gpt, segs, plan, n_in, n_out):
    x_ref, sc_ref, sh_ref, w_ref = refs[:4]
    out_refs = refs[n_in:n_in + n_out]
    h_scr = refs[n_in + n_out]
    j = pl.program_id(1)

    @pl.when(j == 0)
    def _():
        h = _modulate(_ln(x_ref[...]), sc_ref[...], sh_ref[...], gpt)
        h_scr[...] = h.astype(BF16)

    for (lo, cnt), writers in zip(segs, plan):
        @pl.when((j >= lo) & (j < lo + cnt))
        def _(writers=writers):
            acc = jnp.dot(h_scr[...], w_ref[...], preferred_element_type=F32)
            for idx, kind in writers:
                _inproj_write(out_refs[idx], kind, acc)


def _mod_spec(gpt, rows_per_group, bm, group0, which, d):
    if gpt == 1:
        tiles_per_group = rows_per_group // bm
        return pl.BlockSpec((None, 1, 1, d), lambda i, *_: (0, group0 + i // tiles_per_group, 0, which))
    return pl.BlockSpec((None, gpt, 1, d), lambda i, *_: (0, group0 // gpt + i, 0, which))


def _inproj_call(x, mod_l, w_in, kv_carry, *, layer, depth, rows_per_group, group0, bm, tn, key_major):
    m, d = x.shape
    n = w_in.shape[2]
    gpt = max(1, bm // rows_per_group)
    segs = _inproj_segments(d, tn)
    assert n == tn * (segs[-1][0] + segs[-1][1])
    n_carry = len(kv_carry)

    def col(seg):
        lo, cnt = seg
        return lambda j: jnp.clip(j - lo, 0, cnt - 1)

    def rows(seg, width, dtype):
        c = col(seg)
        return (jax.ShapeDtypeStruct((m, width), dtype), pl.BlockSpec((bm, tn), lambda i, j: (i, c(j))))

    def slab(seg):
        c = col(seg)
        return (jax.ShapeDtypeStruct((depth, m, MIX_W), F32),
                pl.BlockSpec((None, bm, tn), lambda i, j: (layer, i, c(j))))

    def key_major_slab(seg):
        c = col(seg)
        tiles_per_seq = rows_per_group // bm
        return (jax.ShapeDtypeStruct((depth, m // rows_per_group, N_HEADS, HEAD_W, rows_per_group), F32),
                pl.BlockSpec((None, None, tn // HEAD_W, HEAD_W, bm),
                             lambda i, j: (layer, i // tiles_per_seq, c(j), 0, i % tiles_per_seq)))

    outs = {"qa": rows(segs[0], MIX_W, BF16)}
    if key_major:
        assert rows_per_group % bm == 0
        outs["ka"] = rows(segs[1], MIX_W, BF16)
        outs["ka_all"] = key_major_slab(segs[1])
        ka_writers = (("ka", "bf16"), ("ka_all", "key_major"))
        outs["va"] = rows(segs[2], MIX_W, BF16)
        va_writers = (("va_all", "f32"), ("va", "bf16"))
    else:
        outs["ka_all"] = slab(segs[1])
        ka_writers = (("ka_all", "f32"),)
        va_writers = (("va_all", "f32"),)
    outs["va_all"] = slab(segs[2])
    outs["qb"] = rows(segs[3], MIX_W, BF16)
    outs["kb"] = rows(segs[4], MIX_W, F32)
    outs["vb"] = rows(segs[5], MIX_W, F32)
    outs["sga"] = rows(segs[6], d, BF16)
    outs["sgb"] = rows(segs[7], d, BF16)
    names = list(outs)
    plan = tuple(tuple((names.index(nm), kind) for nm, kind in writers) for writers in (
        (("qa", "qa"),), ka_writers, va_writers, (("qb", "qb"),),
        (("kb", "f32"),), (("vb", "f32"),), (("sga", "gate"),), (("sgb", "gate"),)))
    n_in = 4 + n_carry
    aliases = {4 + c: names.index(nm) for c, nm in enumerate(("ka_all", "va_all")[:n_carry])}
    res = pl.pallas_call(
        functools.partial(_inproj_kernel, gpt=gpt, segs=segs, plan=plan, n_in=n_in, n_out=len(names)),
        out_shape=tuple(outs[nm][0] for nm in names),
        grid=(m // bm, n // tn),
        in_specs=[
            pl.BlockSpec((bm, d), lambda i, j: (i, 0)),
            _mod_spec(gpt, rows_per_group, bm, group0, 1, d),
            _mod_spec(gpt, rows_per_group, bm, group0, 0, d),
            pl.BlockSpec((None, d, tn), lambda i, j: (layer, 0, j)),
        ] + [pl.BlockSpec(memory_space=pl.ANY)] * n_carry,
        out_specs=tuple(outs[nm][1] for nm in names),
        scratch_shapes=[pltpu.VMEM((bm, d), BF16)],
        input_output_aliases=aliases,
        compiler_params=_cparams(("arbitrary", "arbitrary")),
        name="inproj",
    )(x, mod_l, mod_l, w_in, *kv_carry)
    return dict(zip(names, res))


def _t5_bucket(rel):
    nb = T5_BUCKETS // 2
    max_exact = nb // 2
    rel = np.asarray(rel, np.int32)
    bucket = (rel > 0).astype(np.int32) * nb
    n = np.abs(rel)
    nf = np.maximum(n, 1).astype(np.float32)
    ratio = np.log(nf / np.float32(max_exact)) / np.float32(math.log(T5_MAX_DIST / max_exact))
    large = max_exact + (ratio * np.float32(nb - max_exact)).astype(np.int32)
    large = np.minimum(large, nb - 1)
    return bucket + np.where(n < max_exact, n, large)


def _lookup(table, idx):
    n = table.shape[0]
    flat = jnp.asarray(np.asarray(idx, np.int32).reshape(-1))
    onehot = (flat[None, :] == jnp.arange(n, dtype=jnp.int32)[:, None]).astype(F32)
    out = lax.dot_general(table.astype(F32), onehot, (((0,), (0,)), ((), ())),
                          precision=lax.Precision.HIGHEST, preferred_element_type=F32)
    return out.reshape((table.shape[1],) + tuple(np.shape(idx)))


def _t5_table(t5_bias):
    far = t5_bias[int(_t5_bucket(-T5_MAX_DIST))]
    return (t5_bias - far[None, :]) * LOG2E


def _attn_a_bias_tiles(t5_tab, t):
    assert t >= T5_MAX_DIST
    k = np.arange(t)[:, None]
    q = np.arange(t)[None, :]
    visible = (k // CHUNK) <= (q // CHUNK)
    idx = np.stack([_t5_bucket(k - q), _t5_bucket(k - t - q)])
    vis = np.stack([visible, np.ones_like(visible)])
    return jnp.where(vis[None], _lookup(t5_tab, idx), NEG)


def _attn_b_bias_tile(rel_tab, tq, win):
    kw = np.arange(win)[:, None] - BAND_REACH
    q = np.arange(tq)[None, :]
    rel = np.clip(kw - q, -REL_CLIP, REL_CLIP) + REL_CLIP
    dchunk = q // CHUNK - np.floor_divide(kw, CHUNK)
    visible = (dchunk >= 0) & (dchunk <= BAND_PREV_CHUNKS)
    return jnp.where(visible[None], _lookup(rel_tab, rel), NEG)


def _diff_lambda(lam_ref, lam_init):
    lv = lam_ref[...]
    s1 = jnp.sum(lv[0:1] * lv[1:2], axis=-1, keepdims=True)
    s2 = jnp.sum(lv[2:3] * lv[3:4], axis=-1, keepdims=True)
    return jnp.exp(s1) - jnp.exp(s2) + lam_init


def _stack_diff_queries(q):
    lane = lax.broadcasted_iota(jnp.int32, q.shape, 1)
    zero = jnp.zeros_like(q)
    return jnp.concatenate([jnp.where(lane < DK_A, q, zero), jnp.where(lane >= DK_A, q, zero)], axis=0)


def _stack_diff_queries_t(q):
    qt = q.astype(F32).T.astype(BF16)
    row = lax.broadcasted_iota(jnp.int32, qt.shape, 0)
    zero = jnp.zeros_like(qt)
    return jnp.concatenate([jnp.where(row < DK_A, qt, zero), jnp.where(row >= DK_A, qt, zero)], axis=1)


def _attn_a_kernel(q_ref, k_ref, v_ref, bias_ref, lam_ref, g_ref, o_ref,
                   vt_scr, s_scr, acc_scr, m_scr, *, t, nh, lam_init):
    qi = pl.program_id(2)
    nkb = vt_scr.shape[1]
    heads = range(nh)

    @pl.when(qi == 0)
    def _():
        sub = lax.broadcasted_iota(jnp.int32, (ONES_ROWS, t), 0)
        ones_rows = jnp.where(sub == 0, 1.0, 0.0).astype(BF16)
        for hh in heads:
            cols = slice(hh * HEAD_W, (hh + 1) * HEAD_W)
            for kb in range(nkb):
                vt_scr[hh, kb, 0:HEAD_W, :] = v_ref[kb * t:(kb + 1) * t, cols].astype(F32).T.astype(BF16)
                vt_scr[hh, kb, HEAD_W:HEAD_W + ONES_ROWS, :] = ones_rows

    q2t = [_stack_diff_queries_t(q_ref[:, hh * HEAD_W:(hh + 1) * HEAD_W]) for hh in heads]
    acc_scr[...] = jnp.zeros(acc_scr.shape, F32)

    def scores(hh, kb):
        start = pl.multiple_of(kb * t, t)
        s_scr[hh] = jnp.dot(k_ref[pl.ds(start, t), hh * HEAD_W:(hh + 1) * HEAD_W], q2t[hh],
                            preferred_element_type=F32)

    def step(kb, ms, bias_idx, prefetch):
        out = []
        for hh in heads:
            s = s_scr[hh]
            if bias_idx is not None:
                bias = bias_ref[hh, bias_idx]
                s = s + jnp.concatenate([bias, bias], axis=1)
            m_new = jnp.maximum(ms[hh], jnp.max(s, axis=0, keepdims=True))
            alpha = jnp.exp2(ms[hh] - m_new)
            e = jnp.exp2(s - m_new).astype(BF16)
            pv = jnp.dot(vt_scr[hh, kb], e, preferred_element_type=F32)
            if prefetch:
                scores(hh, kb + 1)
            acc_scr[hh] = alpha * acc_scr[hh] + pv
            out.append(m_new)
        return tuple(out)

    def load_m():
        return tuple(m_scr[hh] for hh in heads)

    def store_m(ms):
        for hh in heads:
            m_scr[hh] = ms[hh]

    for hh in heads:
        scores(hh, 0)
    m0 = tuple(jnp.full((1, 2 * t), -jnp.inf, F32) for _ in heads)
    n_far = jnp.maximum(qi - 1, 0)
    store_m(lax.fori_loop(
        0, lax.shift_right_logical(n_far, 1),
        lambda p, ms: step(2 * p + 1, step(2 * p, ms, None, True), None, True), m0))

    def finish(steps):
        ms = load_m()
        for kb, bias_idx, prefetch in steps:
            ms = step(kb, ms, bias_idx, prefetch)
        lam = _diff_lambda(lam_ref, lam_init)
        for hh in heads:
            o2 = acc_scr[hh, 0:HEAD_W, :] / acc_scr[hh, HEAD_W:HEAD_W + 1, :]
            o = o2[:, :t] - lam * o2[:, t:]
            o = o * lax.rsqrt(jnp.mean(o * o, axis=0, keepdims=True) + LN_EPS)
            o_ref[:, hh * HEAD_W:(hh + 1) * HEAD_W] = (o.T * g_ref[...] * (1.0 - lam_init)).astype(BF16)

    previous = (qi - 1, 1, True)
    diagonal = (qi, 0, False)
    odd_far = lax.rem(n_far, 2) == 1

    @pl.when(qi == 0)
    def _():
        finish([diagonal])

    @pl.when((qi >= 1) & jnp.logical_not(odd_far))
    def _():
        finish([previous, diagonal])

    @pl.when((qi >= 1) & odd_far)
    def _():
        finish([(n_far - 1, None, True), previous, diagonal])


def _attn_a_call(qa, ka, va, bias_tiles, lam_vecs, g, *, batch, seq, lam_init):
    t, nh = TQ_A, HEADS_PER_STEP_A
    nq = seq // t
    w = nh * HEAD_W
    rare = pl.Buffered(1)
    return pl.pallas_call(
        functools.partial(_attn_a_kernel, t=t, nh=nh, lam_init=lam_init),
        out_shape=jax.ShapeDtypeStruct((batch * seq, MIX_W), BF16),
        grid=(batch, N_HEADS // nh, nq),
        in_specs=[
            pl.BlockSpec((t, w), lambda b, h, qi: (b * nq + qi, h)),
            pl.BlockSpec((seq, w), lambda b, h, qi: (b, h)),
            pl.BlockSpec((seq, w), lambda b, h, qi: (b, h), pipeline_mode=rare),
            pl.BlockSpec((nh, 2, t, t), lambda b, h, qi: (h, 0, 0, 0), pipeline_mode=rare),
            pl.BlockSpec((4, DK_A), lambda b, h, qi: (0, 0)),
            pl.BlockSpec((1, HEAD_W), lambda b, h, qi: (0, 0)),
        ],
        out_specs=pl.BlockSpec((t, w), lambda b, h, qi: (b * nq + qi, h)),
        scratch_shapes=[
            pltpu.VMEM((nh, seq // t, HEAD_W + ONES_ROWS, t), BF16),
            pltpu.VMEM((nh, t, 2 * t), F32),
            pltpu.VMEM((nh, HEAD_W + ONES_ROWS, 2 * t), F32),
            pltpu.VMEM((nh, 1, 2 * t), F32),
        ],
        compiler_params=_cparams(("arbitrary", "arbitrary", "arbitrary")),
        name="attn_a",
    )(qa, ka, va, bias_tiles, lam_vecs, g)


def _attn_b_kernel(q_ref, k_ref, v_ref, bias_ref, o_ref, k_scr, vt_scr, *, tq, win, seq, units):
    qi = pl.program_id(2)
    pad_blocks = BAND_REACH // tq
    win_blocks = win // tq

    @pl.when(qi == 0)
    def _():
        k_scr[0:BAND_REACH, :] = jnp.zeros((BAND_REACH, HEAD_W), BF16)
        k_scr[BAND_REACH:BAND_REACH + seq, :] = k_ref[...].astype(BF16)
        for blk in range(pad_blocks):
            vt_scr[blk] = jnp.zeros((HEAD_W, tq), BF16)
        for blk in range(seq // tq):
            vt_scr[pad_blocks + blk] = v_ref[blk * tq:(blk + 1) * tq, :].T.astype(BF16)

    early_steps = -(-BAND_REACH // (units * tq))

    def body(early):
        bias = bias_ref[...]
        raw = []
        for u in range(units):
            start = pl.multiple_of((qi * units + u) * tq, tq)
            raw.append(_nt_dot(k_scr[pl.ds(start, win), :], q_ref[u * tq:(u + 1) * tq, :]))
        for u in range(units):
            blk0 = qi * units + u
            s = raw[u] + bias
            if early and (early_steps > 1 or u * tq < BAND_REACH):
                row = lax.broadcasted_iota(jnp.int32, (win, tq), 0)
                s = jnp.where(row + blk0 * tq >= BAND_REACH, s, NEG)
            m = jnp.max(s, axis=0, keepdims=True)
            e = jnp.exp2(s - m)
            l = jnp.sum(e, axis=0, keepdims=True)
            vt_win = jnp.concatenate([vt_scr[blk0 + w] for w in range(win_blocks)], axis=1)
            acc = jnp.dot(vt_win, e.astype(BF16), preferred_element_type=F32)
            o_ref[u * tq:(u + 1) * tq, :] = (acc / l).T.astype(BF16)

    @pl.when(qi < early_steps)
    def _():
        body(True)

    @pl.when(qi >= early_steps)
    def _():
        body(False)


def _attn_b_call(qb, kb, vb, bias_tile, *, batch, seq):
    tq, win, units = TQ_B, WIN_B, UNITS_B
    rows = tq * units
    nq = seq // rows
    return pl.pallas_call(
        functools.partial(_attn_b_kernel, tq=tq, win=win, seq=seq, units=units),
        out_shape=jax.ShapeDtypeStruct((batch * seq, MIX_W), BF16),
        grid=(batch, N_HEADS, nq),
        in_specs=[
            pl.BlockSpec((rows, HEAD_W), lambda b, h, qi: (b * nq + qi, h)),
            pl.BlockSpec((seq, HEAD_W), lambda b, h, qi: (b, h)),
            pl.BlockSpec((seq, HEAD_W), lambda b, h, qi: (b, h)),
            pl.BlockSpec((None, win, tq), lambda b, h, qi: (h, 0, 0)),
        ],
        out_specs=pl.BlockSpec((rows, HEAD_W), lambda b, h, qi: (b * nq + qi, h)),
        scratch_shapes=[
            pltpu.VMEM((BAND_REACH + seq, HEAD_W), BF16),
            pltpu.VMEM(((BAND_REACH + seq) // tq, HEAD_W, tq), BF16),
        ],
        compiler_params=_cparams(("arbitrary", "arbitrary", "arbitrary")),
        name="attn_b",
    )(qb, kb, vb, bias_tile)


def _softmax_pv(parts):
    m = None
    for s, _ in parts:
        pm = jnp.max(s, axis=-1, keepdims=True)
        m = pm if m is None else jnp.maximum(m, pm)
    l = None
    o = None
    for s, v in parts:
        e = jnp.exp2(s - m)
        pl_ = jnp.sum(e, axis=-1, keepdims=True)
        po = jnp.dot(e.astype(BF16), v, preferred_element_type=F32)
        l = pl_ if l is None else l + pl_
        o = po if o is None else o + po
    return o / l


def _sample_kernel(qa_ref, ka_ref, va_ref, cakt_ref, cav_ref, qb_ref, kb_ref, vb_ref, cbk_ref, cbv_ref,
                   ba_c_ref, ba_n_ref, bb_c_ref, bb_n_ref, lam_ref, g_ref, oa_ref, ob_ref,
                   *, t_new, lam_init):
    lam = _diff_lambda(lam_ref, lam_init)
    g = g_ref[...]
    past = cakt_ref.shape[-1]

    def head_rows(ref, h):
        return ref[pl.ds(h, ref.shape[0] // N_HEADS, stride=N_HEADS), :].astype(BF16)

    scores_a, scores_b = [], []
    for h in range(N_HEADS):
        cols = slice(h * HEAD_W, (h + 1) * HEAD_W)
        q2 = _stack_diff_queries(qa_ref[:, cols])
        kct = cakt_ref[h].reshape(2 * DK_A, past).astype(BF16)
        scores_a.append((jnp.dot(q2, kct, preferred_element_type=F32),
                         _nt_dot(q2, ka_ref[:, cols].astype(BF16))))
        qh = qb_ref[:, cols]
        scores_b.append((_nt_dot(qh, head_rows(cbk_ref, h)), _nt_dot(qh, kb_ref[:, cols].astype(BF16))))

    for h in range(N_HEADS):
        cols = slice(h * HEAD_W, (h + 1) * HEAD_W)
        s_c, s_n = scores_a[h]
        s_c = (s_c.reshape(2, t_new, -1) + ba_c_ref[h][None]).reshape(2 * t_new, -1)
        s_n = (s_n.reshape(2, t_new, -1) + ba_n_ref[h][None]).reshape(2 * t_new, -1)
        o2 = _softmax_pv([(s_c, head_rows(cav_ref, h)), (s_n, va_ref[:, cols].astype(BF16))])
        o = o2[:t_new] - lam * o2[t_new:]
        o = o * lax.rsqrt(jnp.mean(o * o, axis=-1, keepdims=True) + LN_EPS) * g
        oa_ref[:, cols] = (o * (1.0 - lam_init)).astype(BF16)
        s_c, s_n = scores_b[h]
        ob_ref[:, cols] = _softmax_pv([(s_c + bb_c_ref[h], head_rows(cbv_ref, h)),
                                       (s_n + bb_n_ref[h], vb_ref[:, cols].astype(BF16))]).astype(BF16)


def _sample_call(layer, qa, ka_all, va_all, cache_a_kt, cache_a_v, qb, kb, vb, cache_b_k, cache_b_v,
                 ba_c, ba_n, bb_c, bb_n, lam_vecs, g, *, batch, t_new, lam_init):
    past = cache_a_kt.shape[-1]
    band_past = cache_b_k.shape[2] // N_HEADS
    row = lambda b: (b, 0)
    slab = lambda b: (layer, b, 0)
    cache = lambda b: (layer, b, 0, 0)
    cache_t = lambda b: (layer, b, 0, 0, 0, 0)
    const3 = lambda b: (0, 0, 0)
    const2 = lambda b: (0, 0)
    return pl.pallas_call(
        functools.partial(_sample_kernel, t_new=t_new, lam_init=lam_init),
        out_shape=(jax.ShapeDtypeStruct((batch * t_new, MIX_W), BF16),
                   jax.ShapeDtypeStruct((batch * t_new, MIX_W), BF16)),
        grid=(batch,),
        in_specs=[
            pl.BlockSpec((t_new, MIX_W), row),
            pl.BlockSpec((None, t_new, MIX_W), slab),
            pl.BlockSpec((None, t_new, MIX_W), slab),
            pl.BlockSpec((None, None, N_HEADS, 2, DK_A, past), cache_t),
            pl.BlockSpec((None, None, past * N_HEADS, HEAD_W), cache),
            pl.BlockSpec((t_new, MIX_W), row),
            pl.BlockSpec((t_new, MIX_W), row),
            pl.BlockSpec((t_new, MIX_W), row),
            pl.BlockSpec((None, None, band_past * N_HEADS, HEAD_W), cache),
            pl.BlockSpec((None, None, band_past * N_HEADS, HEAD_W), cache),
            pl.BlockSpec(ba_c.shape, const3),
            pl.BlockSpec(ba_n.shape, const3),
            pl.BlockSpec(bb_c.shape, const3),
            pl.BlockSpec(bb_n.shape, const3),
            pl.BlockSpec((4, DK_A), const2),
            pl.BlockSpec((1, HEAD_W), const2),
        ],
        out_specs=(pl.BlockSpec((t_new, MIX_W), row), pl.BlockSpec((t_new, MIX_W), row)),
        compiler_params=_cparams(("arbitrary",)),
        name="sample_mix",
    )(qa, ka_all, va_all, cache_a_kt, cache_a_v, qb, kb, vb, cache_b_k, cache_b_v,
      ba_c, ba_n, bb_c, bb_n, lam_vecs, g)


def _mixout_kernel(oa_ref, ob_ref, sga_ref, sgb_ref, x_ref, gm_ref, woa_ref, wob_ref, wout_ref,
                   lg_ref, lb_ref, o_ref, *, gpt, alpha):
    a = jnp.dot(oa_ref[...], woa_ref[...], preferred_element_type=F32)
    b = jnp.dot(ob_ref[...], wob_ref[...], preferred_element_type=F32)
    merged = sga_ref[...].astype(F32) * a + sgb_ref[...].astype(F32) * b
    y = jnp.dot(merged.astype(BF16), wout_ref[...], preferred_element_type=F32)
    z = alpha * x_ref[...] + _gate(y, gm_ref[...], gpt)
    o_ref[...] = _ln(z) * lg_ref[...] + lb_ref[...]


def _mixout_call(oa, ob, sga, sgb, x, mod_l, w_oa, w_ob, w_out, ln_g, ln_b,
                 *, layer, rows_per_group, group0, bm, alpha):
    m, d = x.shape
    gpt = max(1, bm // rows_per_group)
    row = lambda i: (i, 0)
    const = lambda i: (0, 0)
    weight = lambda i: (layer, 0, 0)
    resident = pl.Buffered(1)
    return pl.pallas_call(
        functools.partial(_mixout_kernel, gpt=gpt, alpha=alpha),
        out_shape=jax.ShapeDtypeStruct((m, d), F32),
        grid=(m // bm,),
        in_specs=[
            pl.BlockSpec((bm, MIX_W), row),
            pl.BlockSpec((bm, MIX_W), row),
            pl.BlockSpec((bm, d), row),
            pl.BlockSpec((bm, d), row),
            pl.BlockSpec((bm, d), row),
            _mod_spec(gpt, rows_per_group, bm, group0, 2, d),
            pl.BlockSpec((None, MIX_W, d), weight, pipeline_mode=resident),
            pl.BlockSpec((None, MIX_W, d), weight, pipeline_mode=resident),
            pl.BlockSpec((None, d, d), weight, pipeline_mode=resident),
            pl.BlockSpec((1, d), const),
            pl.BlockSpec((1, d), const),
        ],
        out_specs=pl.BlockSpec((bm, d), row),
        compiler_params=_cparams(("arbitrary",)),
        name="mixout",
    )(oa, ob, sga, sgb, x, mod_l, w_oa, w_ob, w_out, ln_g, ln_b)


def _ffn_kernel(x_ref, sc_ref, sh_ref, gf_ref, w1_ref, w3_ref, w2_ref, lg_ref, lb_ref, o_ref,
                h_scr, *, gpt, alpha):
    kf = pl.program_id(1)
    bm = h_scr.shape[0]
    chunk = min(bm, LN_CHUNK_ROWS)
    gpc = max(1, gpt * chunk // bm)

    def chunks():
        for r in range(bm // chunk):
            g0 = r * gpc if gpt > 1 else 0
            yield slice(r * chunk, (r + 1) * chunk), slice(g0, g0 + gpc)

    @pl.when(kf == 0)
    def _():
        for rows, grp in chunks():
            h = _modulate(_ln(x_ref[rows, :]), sc_ref[grp], sh_ref[grp], gpc)
            h_scr[rows, :] = h.astype(BF16)
            o_ref[rows, :] = jnp.zeros((chunk, o_ref.shape[1]), F32)

    sub = min(bm, FFN_SUB_ROWS)
    ab = []
    for r in range(bm // sub):
        h = h_scr[r * sub:(r + 1) * sub, :]
        ab.append((jnp.dot(h, w1_ref[...], preferred_element_type=F32),
                   jnp.dot(h, w3_ref[...], preferred_element_type=F32)))
    for r, (a, b) in enumerate(ab):
        u = (a * _sigmoid(a)) * b
        o_ref[r * sub:(r + 1) * sub, :] += jnp.dot(u.astype(BF16), w2_ref[...], preferred_element_type=F32)

    @pl.when(kf == pl.num_programs(1) - 1)
    def _():
        for rows, grp in chunks():
            z = alpha * x_ref[rows, :] + _gate(o_ref[rows, :], gf_ref[grp], gpc)
            o_ref[rows, :] = _ln(z) * lg_ref[...] + lb_ref[...]


def _ffn_call(x, mod_l, w1, w3, w2, ln_g, ln_b, *, layer, rows_per_group, group0, bm, tf, alpha):
    m, d = x.shape
    dff = w1.shape[2]
    gpt = max(1, bm // rows_per_group)
    return pl.pallas_call(
        functools.partial(_ffn_kernel, gpt=gpt, alpha=alpha),
        out_shape=jax.ShapeDtypeStruct((m, d), F32),
        grid=(m // bm, dff // tf),
        in_specs=[
            pl.BlockSpec((bm, d), lambda i, k: (i, 0)),
            _mod_spec(gpt, rows_per_group, bm, group0, 4, d),
            _mod_spec(gpt, rows_per_group, bm, group0, 3, d),
            _mod_spec(gpt, rows_per_group, bm, group0, 5, d),
            pl.BlockSpec((None, d, tf), lambda i, k: (layer, 0, k)),
            pl.BlockSpec((None, d, tf), lambda i, k: (layer, 0, k)),
            pl.BlockSpec((None, tf, d), lambda i, k: (layer, k, 0)),
            pl.BlockSpec((1, d), lambda i, k: (0, 0)),
            pl.BlockSpec((1, d), lambda i, k: (0, 0)),
        ],
        out_specs=pl.BlockSpec((bm, d), lambda i, k: (i, 0)),
        scratch_shapes=[pltpu.VMEM((bm, d), BF16)],
        compiler_params=_cparams(("arbitrary", "arbitrary")),
        name="ffn",
    )(x, mod_l, mod_l, mod_l, w1, w3, w2, ln_g, ln_b)


def _ff_tile(dff):
    for tf in FFN_TILES:
        if dff % tf == 0:
            return tf
    raise ValueError(f"unsupported FFN width {dff}")


def kernel(x_prompt, x_sample, cache_a_k, cache_a_v, cache_b_k, cache_b_v, c_prompt, c_sample,
           w_mod, b_mod, w_in, lambda_q1, lambda_k1, lambda_q2, lambda_k2, subln_g, t5_bias,
           rel_bias, w_oa, w_ob, w_out, ln1_g, ln1_b, w1, w3, w2, ln2_g, ln2_b):
    depth = w_mod.shape[0]
    batch, seq, d = x_prompt.shape
    dec_batch, t_new, _ = x_sample.shape
    past = cache_a_k.shape[2]
    band_past = cache_b_k.shape[2]
    assert past % CHUNK == 0 and t_new <= CHUNK and band_past == BAND_REACH and past >= BAND_REACH
    assert seq % TQ_A == 0 and seq % (TQ_B * UNITS_B) == 0 and seq >= BAND_REACH and d % 128 == 0
    alpha = (2 * depth) ** 0.25
    prompt_band = min(BAND_REACH, seq)

    bm_p = min(ROW_TILE, seq)
    bm_s = min(ROW_TILE, dec_batch * t_new)
    bm_in = min(TALL_ROW_TILE, seq)
    tn = math.gcd(COL_TILE, d)
    tf = _ff_tile(w1.shape[2])

    c_all = jnp.concatenate([c_sample, c_prompt], axis=0)
    mod = _mod_call(c_all, w_mod, b_mod).reshape(depth, dec_batch + batch, 1, 6 * d)

    t5_tab = _t5_table(t5_bias)
    ta_tiles = _attn_a_bias_tiles(t5_tab, TQ_A)
    q_pos_s = past + np.arange(t_new)
    rel_a = np.arange(past + t_new)[None, :] - q_pos_s[:, None]
    ba_s = _lookup(t5_tab, _t5_bucket(rel_a))
    kb_pos_s = past - band_past + np.arange(band_past + t_new)
    rel_b = np.clip(kb_pos_s[None, :] - q_pos_s[:, None], -REL_CLIP, REL_CLIP) + REL_CLIP

    cakt = jnp.transpose(cache_a_k, (0, 1, 3, 4, 5, 2))
    cav = cache_a_v.reshape(depth, dec_batch, past * N_HEADS, HEAD_W)
    cbk = cache_b_k.reshape(depth, dec_batch, band_past * N_HEADS, HEAD_W)
    cbv = cache_b_v.reshape(depth, dec_batch, band_past * N_HEADS, HEAD_W)

    w_in_b = _to_bf16(w_in)
    w_mix_b = (_to_bf16(w_oa), _to_bf16(w_ob), _to_bf16(w_out))
    w_ffn_b = (_to_bf16(w1), _to_bf16(w3), _to_bf16(w2))

    xp = x_prompt.reshape(batch * seq, d)
    xs = x_sample.reshape(dec_batch * t_new, d)
    kv_p, kv_s = (), ()
    bk_p, bv_p, bk_s, bv_s = [], [], [], []
    for l in range(depth):
        lam_init = 0.8 - 0.6 * math.exp(-0.3 * l)
        lam_vecs = jnp.stack([lambda_q1[l], lambda_k1[l], lambda_q2[l], lambda_k2[l]]).astype(F32)
        g = subln_g[l].reshape(1, HEAD_W).astype(F32)
        mod_l = mod[l:l + 1]
        ln1 = (ln1_g[l].reshape(1, d), ln1_b[l].reshape(1, d))
        ln2 = (ln2_g[l].reshape(1, d), ln2_b[l].reshape(1, d))
        rel_tab = rel_bias[l] * LOG2E
        bb_tile = _attn_b_bias_tile(rel_tab, TQ_B, WIN_B)
        bb_s = _lookup(rel_tab, rel_b)

        grp = dict(rows_per_group=seq, group0=dec_batch, bm=bm_p)
        p = _inproj_call(xp, mod_l, w_in_b, kv_p, layer=l, depth=depth, tn=tn, key_major=True,
                         **dict(grp, bm=bm_in))
        kv_p = (p["ka_all"], p["va_all"])
        oa = _attn_a_call(p["qa"], p["ka"], p["va"], ta_tiles, lam_vecs, g,
                          batch=batch, seq=seq, lam_init=lam_init)
        ob = _attn_b_call(p["qb"], p["kb"], p["vb"], bb_tile, batch=batch, seq=seq)
        xp = _mixout_call(oa, ob, p["sga"], p["sgb"], xp, mod_l, *w_mix_b, *ln1,
                          layer=l, alpha=alpha, **grp)
        xp = _ffn_call(xp, mod_l, *w_ffn_b, *ln2, layer=l, tf=tf, alpha=alpha, **dict(grp, bm=bm_in))
        bk_p.append(p["kb"].reshape(batch, seq, N_HEADS, HEAD_W)[:, seq - prompt_band:])
        bv_p.append(p["vb"].reshape(batch, seq, N_HEADS, HEAD_W)[:, seq - prompt_band:])

        grp = dict(rows_per_group=t_new, group0=0, bm=bm_s)
        p = _inproj_call(xs, mod_l, w_in_b, kv_s, layer=l, depth=depth, tn=tn, key_major=False, **grp)
        kv_s = (p["ka_all"], p["va_all"])
        oa, ob = _sample_call(l, p["qa"], p["ka_all"], p["va_all"], cakt, cav, p["qb"], p["kb"],
                              p["vb"], cbk, cbv,
                              ba_s[:, :, :past], ba_s[:, :, past:], bb_s[:, :, :band_past],
                              bb_s[:, :, band_past:], lam_vecs, g,
                              batch=dec_batch, t_new=t_new, lam_init=lam_init)
        xs = _mixout_call(oa, ob, p["sga"], p["sgb"], xs, mod_l, *w_mix_b, *ln1,
                          layer=l, alpha=alpha, **grp)
        xs = _ffn_call(xs, mod_l, *w_ffn_b, *ln2, layer=l, tf=tf, alpha=alpha, **grp)
        bk_s.append(p["kb"].reshape(dec_batch, t_new, N_HEADS, HEAD_W))
        bv_s.append(p["vb"].reshape(dec_batch, t_new, N_HEADS, HEAD_W))

    a_k_prompt = jnp.transpose(kv_p[0].reshape(depth, batch, N_HEADS, 2, DK_A, seq), (0, 1, 5, 2, 3, 4))
    return (xp.reshape(batch, seq, d), xs.reshape(dec_batch, t_new, d),
            a_k_prompt,
            kv_p[1].reshape(depth, batch, seq, N_HEADS, HEAD_W),
            jnp.stack(bk_p), jnp.stack(bv_p),
            kv_s[0].reshape(depth, dec_batch, t_new, N_HEADS, 2, DK_A),
            kv_s[1].reshape(depth, dec_batch, t_new, N_HEADS, HEAD_W),
            jnp.stack(bk_s), jnp.stack(bv_s))
```

```python
import functools
import math

import jax
import jax.numpy as jnp
import numpy as np
from jax import lax
from jax.experimental import pallas as pl
from jax.experimental.pallas import tpu as pltpu

F32 = jnp.float32
BF16 = jnp.bfloat16

CHUNK = 64
N_HEADS = 8
HEAD_W = 128
DK_A = 64
BAND_PREV_CHUNKS = 8
BAND_REACH = BAND_PREV_CHUNKS * CHUNK
REL_CLIP = 128
T5_BUCKETS = 32
T5_MAX_DIST = 128
NEG = -1e30
LN_EPS = 1e-5
MIX_W = N_HEADS * HEAD_W
LOG2E = math.log2(math.e)
QA_SCALE = DK_A ** -0.5 * LOG2E
QB_SCALE = HEAD_W ** -0.5 * LOG2E

V7X_VMEM_BYTES = 64 * 1024 * 1024
VMEM_LIMIT = V7X_VMEM_BYTES - 3 * 1024 * 1024

ROW_TILE = 512
TALL_ROW_TILE = 1024
COL_TILE = 512
FFN_TILES = (512, 256, 128)
MOD_COL_TILE = 1024
TQ_A = 256
HEADS_PER_STEP_A = 8
FFN_SUB_ROWS = 512
LN_CHUNK_ROWS = 256
CAST_BLOCK_ELEMS = 1 << 21
ONES_ROWS = 16
TQ_B = 128
WIN_B = BAND_REACH + TQ_B
UNITS_B = 32


def _cparams(sem):
    return pltpu.CompilerParams(dimension_semantics=sem, vmem_limit_bytes=VMEM_LIMIT)


def _sigmoid(x):
    return 1.0 / (1.0 + jnp.exp(-x))


def _ln(x):
    mu = jnp.mean(x, axis=-1, keepdims=True)
    xc = x - mu
    var = jnp.mean(xc * xc, axis=-1, keepdims=True)
    return xc * lax.rsqrt(var + LN_EPS)


def _modulate(y, scale, shift, gpt):
    if gpt == 1:
        return y * (1.0 + scale[0]) + shift[0]
    bm, d = y.shape
    y3 = y.reshape(gpt, bm // gpt, d)
    return (y3 * (1.0 + scale) + shift).reshape(bm, d)


def _gate(y, gate, gpt):
    if gpt == 1:
        return y * gate[0]
    bm, d = y.shape
    return (y.reshape(gpt, bm // gpt, d) * gate).reshape(bm, d)


def _nt_dot(a, b):
    return lax.dot_general(a, b, (((1,), (1,)), ((), ())), preferred_element_type=F32)


def _mod_kernel(c_ref, w_ref, b_ref, o_ref):
    c = c_ref[...]
    s = (c * _sigmoid(c)).astype(BF16)
    o_ref[...] = jnp.dot(s, w_ref[...].astype(BF16), preferred_element_type=F32) + b_ref[...]


def _mod_call(c_all, w_mod, b_mod):
    depth, d, n = w_mod.shape
    g = c_all.shape[0]
    tn = math.gcd(n, MOD_COL_TILE)
    return pl.pallas_call(
        _mod_kernel,
        out_shape=jax.ShapeDtypeStruct((depth, g, n), F32),
        grid=(depth, n // tn),
        in_specs=[
            pl.BlockSpec((g, d), lambda l, j: (0, 0)),
            pl.BlockSpec((None, d, tn), lambda l, j: (l, 0, j)),
            pl.BlockSpec((None, 1, tn), lambda l, j: (l, 0, j)),
        ],
        out_specs=pl.BlockSpec((None, g, tn), lambda l, j: (l, 0, j)),
        compiler_params=_cparams(("arbitrary", "arbitrary")),
        name="mod",
    )(c_all, w_mod, b_mod.reshape(depth, 1, n))


def _cast_kernel(w_ref, o_ref):
    o_ref[...] = w_ref[...].astype(BF16)


def _to_bf16(w):
    depth, r, c = w.shape
    br = r
    while br % 2 == 0 and br > 16 and br * c > CAST_BLOCK_ELEMS:
        br //= 2
    return pl.pallas_call(
        _cast_kernel,
        out_shape=jax.ShapeDtypeStruct(w.shape, BF16),
        grid=(depth, r // br),
        in_specs=[pl.BlockSpec((None, br, c), lambda l, i: (l, i, 0))],
        out_specs=pl.BlockSpec((None, br, c), lambda l, i: (l, i, 0)),
        compiler_params=_cparams(("arbitrary", "arbitrary")),
        name="to_bf16",
    )(w)


def _inproj_segments(d, tn):
    widths = (MIX_W,) * 6 + (d, d)
    segs, lo = [], 0
    for w in widths:
        assert w % tn == 0
        segs.append((lo, w // tn))
        lo += w // tn
    return tuple(segs)


def _inproj_write(ref, kind, acc):
    if kind == "f32":
        ref[...] = acc
    elif kind == "bf16":
        ref[...] = acc.astype(BF16)
    elif kind == "qa":
        ref[...] = (acc * QA_SCALE).astype(BF16)
    elif kind == "qb":
        ref[...] = (acc * QB_SCALE).astype(BF16)
    elif kind == "gate":
        ref[...] = (0.5 * jnp.tanh(0.5 * acc) + 0.5).astype(BF16)
    elif kind == "key_major":
        for h in range(ref.shape[0]):
            ref[h] = acc[:, h * HEAD_W:(h + 1) * HEAD_W].T
    else:
        raise ValueError(kind)


def _inproj_kernel(*refs, gpt, segs, plan, n_in, n_out):
    x_ref, sc_ref, sh_ref, w_ref = refs[:4]
    out_refs = refs[n_in:n_in + n_out]
    h_scr = refs[n_in + n_out]
    j = pl.program_id(1)

    @pl.when(j == 0)
    def _():
        h = _modulate(_ln(x_ref[...]), sc_ref[...], sh_ref[...], gpt)
        h_scr[...] = h.astype(BF16)

    for (lo, cnt), writers in zip(segs, plan):
        @pl.when((j >= lo) & (j < lo + cnt))
        def _(writers=writers):
            acc = jnp.dot(h_scr[...], w_ref[...], preferred_element_type=F32)
            for idx, kind in writers:
                _inproj_write(out_refs[idx], kind, acc)


def _mod_spec(gpt, rows_per_group, bm, group0, which, d):
    if gpt == 1:
        tiles_per_group = rows_per_group // bm
        return pl.BlockSpec((None, 1, 1, d), lambda i, *_: (0, group0 + i // tiles_per_group, 0, which))
    return pl.BlockSpec((None, gpt, 1, d), lambda i, *_: (0, group0 // gpt + i, 0, which))


def _inproj_call(x, mod_l, w_in, kv_carry, *, layer, depth, rows_per_group, group0, bm, tn, key_major):
    m, d = x.shape
    n = w_in.shape[2]
    gpt = max(1, bm // rows_per_group)
    segs = _inproj_segments(d, tn)
    assert n == tn * (segs[-1][0] + segs[-1][1])
    n_carry = len(kv_carry)

    def col(seg):
        lo, cnt = seg
        return lambda j: jnp.clip(j - lo, 0, cnt - 1)

    def rows(seg, width, dtype):
        c = col(seg)
        return (jax.ShapeDtypeStruct((m, width), dtype), pl.BlockSpec((bm, tn), lambda i, j: (i, c(j))))

    def slab(seg):
        c = col(seg)
        return (jax.ShapeDtypeStruct((depth, m, MIX_W), F32),
                pl.BlockSpec((None, bm, tn), lambda i, j: (layer, i, c(j))))

    def key_major_slab(seg):
        c = col(seg)
        tiles_per_seq = rows_per_group // bm
        return (jax.ShapeDtypeStruct((depth, m // rows_per_group, N_HEADS, HEAD_W, rows_per_group), F32),
                pl.BlockSpec((None, None, tn // HEAD_W, HEAD_W, bm),
                             lambda i, j: (layer, i // tiles_per_seq, c(j), 0, i % tiles_per_seq)))

    outs = {"qa": rows(segs[0], MIX_W, BF16)}
    if key_major:
        assert rows_per_group % bm == 0
        outs["ka"] = rows(segs[1], MIX_W, BF16)
        outs["ka_all"] = key_major_slab(segs[1])
        ka_writers = (("ka", "bf16"), ("ka_all", "key_major"))
        outs["va"] = rows(segs[2], MIX_W, BF16)
        va_writers = (("va_all", "f32"), ("va", "bf16"))
    else:
        outs["ka_all"] = slab(segs[1])
        ka_writers = (("ka_all", "f32"),)
        va_writers = (("va_all", "f32"),)
    outs["va_all"] = slab(segs[2])
    outs["qb"] = rows(segs[3], MIX_W, BF16)
    outs["kb"] = rows(segs[4], MIX_W, F32)
    outs["vb"] = rows(segs[5], MIX_W, F32)
    outs["sga"] = rows(segs[6], d, BF16)
    outs["sgb"] = rows(segs[7], d, BF16)
    names = list(outs)
    plan = tuple(tuple((names.index(nm), kind) for nm, kind in writers) for writers in (
        (("qa", "qa"),), ka_writers, va_writers, (("qb", "qb"),),
        (("kb", "f32"),), (("vb", "f32"),), (("sga", "gate"),), (("sgb", "gate"),)))
    n_in = 4 + n_carry
    aliases = {4 + c: names.index(nm) for c, nm in enumerate(("ka_all", "va_all")[:n_carry])}
    res = pl.pallas_call(
        functools.partial(_inproj_kernel, gpt=gpt, segs=segs, plan=plan, n_in=n_in, n_out=len(names)),
        out_shape=tuple(outs[nm][0] for nm in names),
        grid=(m // bm, n // tn),
        in_specs=[
            pl.BlockSpec((bm, d), lambda i, j: (i, 0)),
            _mod_spec(gpt, rows_per_group, bm, group0, 1, d),
            _mod_spec(gpt, rows_per_group, bm, group0, 0, d),
            pl.BlockSpec((None, d, tn), lambda i, j: (layer, 0, j)),
        ] + [pl.BlockSpec(memory_space=pl.ANY)] * n_carry,
        out_specs=tuple(outs[nm][1] for nm in names),
        scratch_shapes=[pltpu.VMEM((bm, d), BF16)],
        input_output_aliases=aliases,
        compiler_params=_cparams(("arbitrary", "arbitrary")),
        name="inproj",
    )(x, mod_l, mod_l, w_in, *kv_carry)
    return dict(zip(names, res))


def _t5_bucket(rel):
    nb = T5_BUCKETS // 2
    max_exact = nb // 2
    rel = np.asarray(rel, np.int32)
    bucket = (rel > 0).astype(np.int32) * nb
    n = np.abs(rel)
    nf = np.maximum(n, 1).astype(np.float32)
    ratio = np.log(nf / np.float32(max_exact)) / np.float32(math.log(T5_MAX_DIST / max_exact))
    large = max_exact + (ratio * np.float32(nb - max_exact)).astype(np.int32)
    large = np.minimum(large, nb - 1)
    return bucket + np.where(n < max_exact, n, large)


def _lookup(table, idx):
    n = table.shape[0]
    flat = jnp.asarray(np.asarray(idx, np.int32).reshape(-1))
    onehot = (flat[None, :] == jnp.arange(n, dtype=jnp.int32)[:, None]).astype(F32)
    out = lax.dot_general(table.astype(F32), onehot, (((0,), (0,)), ((), ())),
                          precision=lax.Precision.HIGHEST, preferred_element_type=F32)
    return out.reshape((table.shape[1],) + tuple(np.shape(idx)))


def _t5_table(t5_bias):
    far = t5_bias[int(_t5_bucket(-T5_MAX_DIST))]
    return (t5_bias - far[None, :]) * LOG2E


def _attn_a_bias_tiles(t5_tab, t):
    assert t >= T5_MAX_DIST
    k = np.arange(t)[:, None]
    q = np.arange(t)[None, :]
    visible = (k // CHUNK) <= (q // CHUNK)
    idx = np.stack([_t5_bucket(k - q), _t5_bucket(k - t - q)])
    vis = np.stack([visible, np.ones_like(visible)])
    return jnp.where(vis[None], _lookup(t5_tab, idx), NEG)


def _attn_b_bias_tile(rel_tab, tq, win):
    kw = np.arange(win)[:, None] - BAND_REACH
    q = np.arange(tq)[None, :]
    rel = np.clip(kw - q, -REL_CLIP, REL_CLIP) + REL_CLIP
    dchunk = q // CHUNK - np.floor_divide(kw, CHUNK)
    visible = (dchunk >= 0) & (dchunk <= BAND_PREV_CHUNKS)
    return jnp.where(visible[None], _lookup(rel_tab, rel), NEG)


def _diff_lambda(lam_ref, lam_init):
    lv = lam_ref[...]
    s1 = jnp.sum(lv[0:1] * lv[1:2], axis=-1, keepdims=True)
    s2 = jnp.sum(lv[2:3] * lv[3:4], axis=-1, keepdims=True)
    return jnp.exp(s1) - jnp.exp(s2) + lam_init


def _stack_diff_queries(q):
    lane = lax.broadcasted_iota(jnp.int32, q.shape, 1)
    zero = jnp.zeros_like(q)
    return jnp.concatenate([jnp.where(lane < DK_A, q, zero), jnp.where(lane >= DK_A, q, zero)], axis=0)


def _stack_diff_queries_t(q):
    qt = q.astype(F32).T.astype(BF16)
    row = lax.broadcasted_iota(jnp.int32, qt.shape, 0)
    zero = jnp.zeros_like(qt)
    return jnp.concatenate([jnp.where(row < DK_A, qt, zero), jnp.where(row >= DK_A, qt, zero)], axis=1)


def _attn_a_kernel(q_ref, k_ref, v_ref, bias_ref, lam_ref, g_ref, o_ref,
                   vt_scr, s_scr, acc_scr, m_scr, *, t, nh, lam_init):
    qi = pl.program_id(2)
    nkb = vt_scr.shape[1]
    heads = range(nh)

    @pl.when(qi == 0)
    def _():
        sub = lax.broadcasted_iota(jnp.int32, (ONES_ROWS, t), 0)
        ones_rows = jnp.where(sub == 0, 1.0, 0.0).astype(BF16)
        for hh in heads:
            cols = slice(hh * HEAD_W, (hh + 1) * HEAD_W)
            for kb in range(nkb):
                vt_scr[hh, kb, 0:HEAD_W, :] = v_ref[kb * t:(kb + 1) * t, cols].astype(F32).T.astype(BF16)
                vt_scr[hh, kb, HEAD_W:HEAD_W + ONES_ROWS, :] = ones_rows

    q2t = [_stack_diff_queries_t(q_ref[:, hh * HEAD_W:(hh + 1) * HEAD_W]) for hh in heads]
    acc_scr[...] = jnp.zeros(acc_scr.shape, F32)

    def scores(hh, kb):
        start = pl.multiple_of(kb * t, t)
        s_scr[hh] = jnp.dot(k_ref[pl.ds(start, t), hh * HEAD_W:(hh + 1) * HEAD_W], q2t[hh],
                            preferred_element_type=F32)

    def step(kb, ms, bias_idx, prefetch):
        out = []
        for hh in heads:
            s = s_scr[hh]
            if bias_idx is not None:
                bias = bias_ref[hh, bias_idx]
                s = s + jnp.concatenate([bias, bias], axis=1)
            m_new = jnp.maximum(ms[hh], jnp.max(s, axis=0, keepdims=True))
            alpha = jnp.exp2(ms[hh] - m_new)
            e = jnp.exp2(s - m_new).astype(BF16)
            pv = jnp.dot(vt_scr[hh, kb], e, preferred_element_type=F32)
            if prefetch:
                scores(hh, kb + 1)
            acc_scr[hh] = alpha * acc_scr[hh] + pv
            out.append(m_new)
        return tuple(out)

    def load_m():
        return tuple(m_scr[hh] for hh in heads)

    def store_m(ms):
        for hh in heads:
            m_scr[hh] = ms[hh]

    for hh in heads:
        scores(hh, 0)
    m0 = tuple(jnp.full((1, 2 * t), -jnp.inf, F32) for _ in heads)
    n_far = jnp.maximum(qi - 1, 0)
    store_m(lax.fori_loop(
        0, lax.shift_right_logical(n_far, 1),
        lambda p, ms: step(2 * p + 1, step(2 * p, ms, None, True), None, True), m0))

    def finish(steps):
        ms = load_m()
        for kb, bias_idx, prefetch in steps:
            ms = step(kb, ms, bias_idx, prefetch)
        lam = _diff_lambda(lam_ref, lam_init)
        for hh in heads:
            o2 = acc_scr[hh, 0:HEAD_W, :] / acc_scr[hh, HEAD_W:HEAD_W + 1, :]
            o = o2[:, :t] - lam * o2[:, t:]
            o = o * lax.rsqrt(jnp.mean(o * o, axis=0, keepdims=True) + LN_EPS)
            o_ref[:, hh * HEAD_W:(hh + 1) * HEAD_W] = (o.T * g_ref[...] * (1.0 - lam_init)).astype(BF16)

    previous = (qi - 1, 1, True)
    diagonal = (qi, 0, False)
    odd_far = lax.rem(n_far, 2) == 1

    @pl.when(qi == 0)
    def _():
        finish([diagonal])

    @pl.when((qi >= 1) & jnp.logical_not(odd_far))
    def _():
        finish([previous, diagonal])

    @pl.when((qi >= 1) & odd_far)
    def _():
        finish([(n_far - 1, None, True), previous, diagonal])


def _attn_a_call(qa, ka, va, bias_tiles, lam_vecs, g, *, batch, seq, lam_init):
    t, nh = TQ_A, HEADS_PER_STEP_A
    nq = seq // t
    w = nh * HEAD_W
    rare = pl.Buffered(1)
    return pl.pallas_call(
        functools.partial(_attn_a_kernel, t=t, nh=nh, lam_init=lam_init),
        out_shape=jax.ShapeDtypeStruct((batch * seq, MIX_W), BF16),
        grid=(batch, N_HEADS // nh, nq),
        in_specs=[
            pl.BlockSpec((t, w), lambda b, h, qi: (b * nq + qi, h)),
            pl.BlockSpec((seq, w), lambda b, h, qi: (b, h)),
            pl.BlockSpec((seq, w), lambda b, h, qi: (b, h), pipeline_mode=rare),
            pl.BlockSpec((nh, 2, t, t), lambda b, h, qi: (h, 0, 0, 0), pipeline_mode=rare),
            pl.BlockSpec((4, DK_A), lambda b, h, qi: (0, 0)),
            pl.BlockSpec((1, HEAD_W), lambda b, h, qi: (0, 0)),
        ],
        out_specs=pl.BlockSpec((t, w), lambda b, h, qi: (b * nq + qi, h)),
        scratch_shapes=[
            pltpu.VMEM((nh, seq // t, HEAD_W + ONES_ROWS, t), BF16),
            pltpu.VMEM((nh, t, 2 * t), F32),
            pltpu.VMEM((nh, HEAD_W + ONES_ROWS, 2 * t), F32),
            pltpu.VMEM((nh, 1, 2 * t), F32),
        ],
        compiler_params=_cparams(("arbitrary", "arbitrary", "arbitrary")),
        name="attn_a",
    )(qa, ka, va, bias_tiles, lam_vecs, g)


def _attn_b_kernel(q_ref, k_ref, v_ref, bias_ref, o_ref, k_scr, vt_scr, *, tq, win, seq, units):
    qi = pl.program_id(2)
    pad_blocks = BAND_REACH // tq
    win_blocks = win // tq

    @pl.when(qi == 0)
    def _():
        k_scr[0:BAND_REACH, :] = jnp.zeros((BAND_REACH, HEAD_W), BF16)
        k_scr[BAND_REACH:BAND_REACH + seq, :] = k_ref[...].astype(BF16)
        for blk in range(pad_blocks):
            vt_scr[blk] = jnp.zeros((HEAD_W, tq), BF16)
        for blk in range(seq // tq):
            vt_scr[pad_blocks + blk] = v_ref[blk * tq:(blk + 1) * tq, :].T.astype(BF16)

    early_steps = -(-BAND_REACH // (units * tq))

    def body(early):
        bias = bias_ref[...]
        raw = []
        for u in range(units):
            start = pl.multiple_of((qi * units + u) * tq, tq)
            raw.append(_nt_dot(k_scr[pl.ds(start, win), :], q_ref[u * tq:(u + 1) * tq, :]))
        for u in range(units):
            blk0 = qi * units + u
            s = raw[u] + bias
            if early and (early_steps > 1 or u * tq < BAND_REACH):
                row = lax.broadcasted_iota(jnp.int32, (win, tq), 0)
                s = jnp.where(row + blk0 * tq >= BAND_REACH, s, NEG)
            m = jnp.max(s, axis=0, keepdims=True)
            e = jnp.exp2(s - m)
            l = jnp.sum(e, axis=0, keepdims=True)
            vt_win = jnp.concatenate([vt_scr[blk0 + w] for w in range(win_blocks)], axis=1)
            acc = jnp.dot(vt_win, e.astype(BF16), preferred_element_type=F32)
            o_ref[u * tq:(u + 1) * tq, :] = (acc / l).T.astype(BF16)

    @pl.when(qi < early_steps)
    def _():
        body(True)

    @pl.when(qi >= early_steps)
    def _():
        body(False)


def _attn_b_call(qb, kb, vb, bias_tile, *, batch, seq):
    tq, win, units = TQ_B, WIN_B, UNITS_B
    rows = tq * units
    nq = seq // rows
    return pl.pallas_call(
        functools.partial(_attn_b_kernel, tq=tq, win=win, seq=seq, units=units),
        out_shape=jax.ShapeDtypeStruct((batch * seq, MIX_W), BF16),
        grid=(batch, N_HEADS, nq),
        in_specs=[
            pl.BlockSpec((rows, HEAD_W), lambda b, h, qi: (b * nq + qi, h)),
            pl.BlockSpec((seq, HEAD_W), lambda b, h, qi: (b, h)),
            pl.BlockSpec((seq, HEAD_W), lambda b, h, qi: (b, h)),
            pl.BlockSpec((None, win, tq), lambda b, h, qi: (h, 0, 0)),
        ],
        out_specs=pl.BlockSpec((rows, HEAD_W), lambda b, h, qi: (b * nq + qi, h)),
        scratch_shapes=[
            pltpu.VMEM((BAND_REACH + seq, HEAD_W), BF16),
            pltpu.VMEM(((BAND_REACH + seq) // tq, HEAD_W, tq), BF16),
        ],
        compiler_params=_cparams(("arbitrary", "arbitrary", "arbitrary")),
        name="attn_b",
    )(qb, kb, vb, bias_tile)


def _softmax_pv(parts):
    m = None
    for s, _ in parts:
        pm = jnp.max(s, axis=-1, keepdims=True)
        m = pm if m is None else jnp.maximum(m, pm)
    l = None
    o = None
    for s, v in parts:
        e = jnp.exp2(s - m)
        pl_ = jnp.sum(e, axis=-1, keepdims=True)
        po = jnp.dot(e.astype(BF16), v, preferred_element_type=F32)
        l = pl_ if l is None else l + pl_
        o = po if o is None else o + po
    return o / l


def _sample_kernel(qa_ref, ka_ref, va_ref, cakt_ref, cav_ref, qb_ref, kb_ref, vb_ref, cbk_ref, cbv_ref,
                   ba_c_ref, ba_n_ref, bb_c_ref, bb_n_ref, lam_ref, g_ref, oa_ref, ob_ref,
                   *, t_new, lam_init):
    lam = _diff_lambda(lam_ref, lam_init)
    g = g_ref[...]
    past = cakt_ref.shape[-1]

    def head_rows(ref, h):
        return ref[pl.ds(h, ref.shape[0] // N_HEADS, stride=N_HEADS), :].astype(BF16)

    scores_a, scores_b = [], []
    for h in range(N_HEADS):
        cols = slice(h * HEAD_W, (h + 1) * HEAD_W)
        q2 = _stack_diff_queries(qa_ref[:, cols])
        kct = cakt_ref[h].reshape(2 * DK_A, past).astype(BF16)
        scores_a.append((jnp.dot(q2, kct, preferred_element_type=F32),
                         _nt_dot(q2, ka_ref[:, cols].astype(BF16))))
        qh = qb_ref[:, cols]
        scores_b.append((_nt_dot(qh, head_rows(cbk_ref, h)), _nt_dot(qh, kb_ref[:, cols].astype(BF16))))

    for h in range(N_HEADS):
        cols = slice(h * HEAD_W, (h + 1) * HEAD_W)
        s_c, s_n = scores_a[h]
        s_c = (s_c.reshape(2, t_new, -1) + ba_c_ref[h][None]).reshape(2 * t_new, -1)
        s_n = (s_n.reshape(2, t_new, -1) + ba_n_ref[h][None]).reshape(2 * t_new, -1)
        o2 = _softmax_pv([(s_c, head_rows(cav_ref, h)), (s_n, va_ref[:, cols].astype(BF16))])
        o = o2[:t_new] - lam * o2[t_new:]
        o = o * lax.rsqrt(jnp.mean(o * o, axis=-1, keepdims=True) + LN_EPS) * g
        oa_ref[:, cols] = (o * (1.0 - lam_init)).astype(BF16)
        s_c, s_n = scores_b[h]
        ob_ref[:, cols] = _softmax_pv([(s_c + bb_c_ref[h], head_rows(cbv_ref, h)),
                                       (s_n + bb_n_ref[h], vb_ref[:, cols].astype(BF16))]).astype(BF16)


def _sample_call(layer, qa, ka_all, va_all, cache_a_kt, cache_a_v, qb, kb, vb, cache_b_k, cache_b_v,
                 ba_c, ba_n, bb_c, bb_n, lam_vecs, g, *, batch, t_new, lam_init):
    past = cache_a_kt.shape[-1]
    band_past = cache_b_k.shape[2] // N_HEADS
    row = lambda b: (b, 0)
    slab = lambda b: (layer, b, 0)
    cache = lambda b: (layer, b, 0, 0)
    cache_t = lambda b: (layer, b, 0, 0, 0, 0)
    const3 = lambda b: (0, 0, 0)
    const2 = lambda b: (0, 0)
    return pl.pallas_call(
        functools.partial(_sample_kernel, t_new=t_new, lam_init=lam_init),
        out_shape=(jax.ShapeDtypeStruct((batch * t_new, MIX_W), BF16),
                   jax.ShapeDtypeStruct((batch * t_new, MIX_W), BF16)),
        grid=(batch,),
        in_specs=[
            pl.BlockSpec((t_new, MIX_W), row),
            pl.BlockSpec((None, t_new, MIX_W), slab),
            pl.BlockSpec((None, t_new, MIX_W), slab),
            pl.BlockSpec((None, None, N_HEADS, 2, DK_A, past), cache_t),
            pl.BlockSpec((None, None, past * N_HEADS, HEAD_W), cache),
            pl.BlockSpec((t_new, MIX_W), row),
            pl.BlockSpec((t_new, MIX_W), row),
            pl.BlockSpec((t_new, MIX_W), row),
            pl.BlockSpec((None, None, band_past * N_HEADS, HEAD_W), cache),
            pl.BlockSpec((None, None, band_past * N_HEADS, HEAD_W), cache),
            pl.BlockSpec(ba_c.shape, const3),
            pl.BlockSpec(ba_n.shape, const3),
            pl.BlockSpec(bb_c.shape, const3),
            pl.BlockSpec(bb_n.shape, const3),
            pl.BlockSpec((4, DK_A), const2),
            pl.BlockSpec((1, HEAD_W), const2),
        ],
        out_specs=(pl.BlockSpec((t_new, MIX_W), row), pl.BlockSpec((t_new, MIX_W), row)),
        compiler_params=_cparams(("arbitrary",)),
        name="sample_mix",
    )(qa, ka_all, va_all, cache_a_kt, cache_a_v, qb, kb, vb, cache_b_k, cache_b_v,
      ba_c, ba_n, bb_c, bb_n, lam_vecs, g)


def _mixout_kernel(oa_ref, ob_ref, sga_ref, sgb_ref, x_ref, gm_ref, woa_ref, wob_ref, wout_ref,
                   lg_ref, lb_ref, o_ref, *, gpt, alpha):
    a = jnp.dot(oa_ref[...], woa_ref[...], preferred_element_type=F32)
    b = jnp.dot(ob_ref[...], wob_ref[...], preferred_element_type=F32)
    merged = sga_ref[...].astype(F32) * a + sgb_ref[...].astype(F32) * b
    y = jnp.dot(merged.astype(BF16), wout_ref[...], preferred_element_type=F32)
    z = alpha * x_ref[...] + _gate(y, gm_ref[...], gpt)
    o_ref[...] = _ln(z) * lg_ref[...] + lb_ref[...]


def _mixout_call(oa, ob, sga, sgb, x, mod_l, w_oa, w_ob, w_out, ln_g, ln_b,
                 *, layer, rows_per_group, group0, bm, alpha):
    m, d = x.shape
    gpt = max(1, bm // rows_per_group)
    row = lambda i: (i, 0)
    const = lambda i: (0, 0)
    weight = lambda i: (layer, 0, 0)
    resident = pl.Buffered(1)
    return pl.pallas_call(
        functools.partial(_mixout_kernel, gpt=gpt, alpha=alpha),
        out_shape=jax.ShapeDtypeStruct((m, d), F32),
        grid=(m // bm,),
        in_specs=[
            pl.BlockSpec((bm, MIX_W), row),
            pl.BlockSpec((bm, MIX_W), row),
            pl.BlockSpec((bm, d), row),
            pl.BlockSpec((bm, d), row),
            pl.BlockSpec((bm, d), row),
            _mod_spec(gpt, rows_per_group, bm, group0, 2, d),
            pl.BlockSpec((None, MIX_W, d), weight, pipeline_mode=resident),
            pl.BlockSpec((None, MIX_W, d), weight, pipeline_mode=resident),
            pl.BlockSpec((None, d, d), weight, pipeline_mode=resident),
            pl.BlockSpec((1, d), const),
            pl.BlockSpec((1, d), const),
        ],
        out_specs=pl.BlockSpec((bm, d), row),
        compiler_params=_cparams(("arbitrary",)),
        name="mixout",
    )(oa, ob, sga, sgb, x, mod_l, w_oa, w_ob, w_out, ln_g, ln_b)


def _ffn_kernel(x_ref, sc_ref, sh_ref, gf_ref, w1_ref, w3_ref, w2_ref, lg_ref, lb_ref, o_ref,
                h_scr, *, gpt, alpha):
    kf = pl.program_id(1)
    bm = h_scr.shape[0]
    chunk = min(bm, LN_CHUNK_ROWS)
    gpc = max(1, gpt * chunk // bm)

    def chunks():
        for r in range(bm // chunk):
            g0 = r * gpc if gpt > 1 else 0
            yield slice(r * chunk, (r + 1) * chunk), slice(g0, g0 + gpc)

    @pl.when(kf == 0)
    def _():
        for rows, grp in chunks():
            h = _modulate(_ln(x_ref[rows, :]), sc_ref[grp], sh_ref[grp], gpc)
            h_scr[rows, :] = h.astype(BF16)
            o_ref[rows, :] = jnp.zeros((chunk, o_ref.shape[1]), F32)

    sub = min(bm, FFN_SUB_ROWS)
    ab = []
    for r in range(bm // sub):
        h = h_scr[r * sub:(r + 1) * sub, :]
        ab.append((jnp.dot(h, w1_ref[...], preferred_element_type=F32),
                   jnp.dot(h, w3_ref[...], preferred_element_type=F32)))
    for r, (a, b) in enumerate(ab):
        u = (a * (0.5 * jnp.tanh(0.5 * a) + 0.5)) * b
        o_ref[r * sub:(r + 1) * sub, :] += jnp.dot(u.astype(BF16), w2_ref[...], preferred_element_type=F32)

    @pl.when(kf == pl.num_programs(1) - 1)
    def _():
        for rows, grp in chunks():
            z = alpha * x_ref[rows, :] + _gate(o_ref[rows, :], gf_ref[grp], gpc)
            o_ref[rows, :] = _ln(z) * lg_ref[...] + lb_ref[...]


def _ffn_call(x, mod_l, w1, w3, w2, ln_g, ln_b, *, layer, rows_per_group, group0, bm, tf, alpha):
    m, d = x.shape
    dff = w1.shape[2]
    gpt = max(1, bm // rows_per_group)
    return pl.pallas_call(
        functools.partial(_ffn_kernel, gpt=gpt, alpha=alpha),
        out_shape=jax.ShapeDtypeStruct((m, d), F32),
        grid=(m // bm, dff // tf),
        in_specs=[
            pl.BlockSpec((bm, d), lambda i, k: (i, 0)),
            _mod_spec(gpt, rows_per_group, bm, group0, 4, d),
            _mod_spec(gpt, rows_per_group, bm, group0, 3, d),
            _mod_spec(gpt, rows_per_group, bm, group0, 5, d),
            pl.BlockSpec((None, d, tf), lambda i, k: (layer, 0, k)),
            pl.BlockSpec((None, d, tf), lambda i, k: (layer, 0, k)),
            pl.BlockSpec((None, tf, d), lambda i, k: (layer, k, 0)),
            pl.BlockSpec((1, d), lambda i, k: (0, 0)),
            pl.BlockSpec((1, d), lambda i, k: (0, 0)),
        ],
        out_specs=pl.BlockSpec((bm, d), lambda i, k: (i, 0)),
        scratch_shapes=[pltpu.VMEM((bm, d), BF16)],
        compiler_params=_cparams(("arbitrary", "arbitrary")),
        name="ffn",
    )(x, mod_l, mod_l, mod_l, w1, w3, w2, ln_g, ln_b)


def _ff_tile(dff):
    for tf in FFN_TILES:
        if dff % tf == 0:
            return tf
    raise ValueError(f"unsupported FFN width {dff}")


def kernel(x_prompt, x_sample, cache_a_k, cache_a_v, cache_b_k, cache_b_v, c_prompt, c_sample,
           w_mod, b_mod, w_in, lambda_q1, lambda_k1, lambda_q2, lambda_k2, subln_g, t5_bias,
           rel_bias, w_oa, w_ob, w_out, ln1_g, ln1_b, w1, w3, w2, ln2_g, ln2_b):
    depth = w_mod.shape[0]
    batch, seq, d = x_prompt.shape
    dec_batch, t_new, _ = x_sample.shape
    past = cache_a_k.shape[2]
    band_past = cache_b_k.shape[2]
    assert past % CHUNK == 0 and t_new <= CHUNK and band_past == BAND_REACH and past >= BAND_REACH
    assert seq % TQ_A == 0 and seq % (TQ_B * UNITS_B) == 0 and seq >= BAND_REACH and d % 128 == 0
    alpha = (2 * depth) ** 0.25
    prompt_band = min(BAND_REACH, seq)

    bm_p = min(ROW_TILE, seq)
    bm_s = min(ROW_TILE, dec_batch * t_new)
    bm_in = min(TALL_ROW_TILE, seq)
    bm_sin = min(TALL_ROW_TILE, dec_batch * t_new)
    tn = math.gcd(COL_TILE, d)
    tf = _ff_tile(w1.shape[2])

    c_all = jnp.concatenate([c_sample, c_prompt], axis=0)
    mod = _mod_call(c_all, w_mod, b_mod).reshape(depth, dec_batch + batch, 1, 6 * d)

    t5_tab = _t5_table(t5_bias)
    ta_tiles = _attn_a_bias_tiles(t5_tab, TQ_A)
    q_pos_s = past + np.arange(t_new)
    rel_a = np.arange(past + t_new)[None, :] - q_pos_s[:, None]
    ba_s = _lookup(t5_tab, _t5_bucket(rel_a))
    kb_pos_s = past - band_past + np.arange(band_past + t_new)
    rel_b = np.clip(kb_pos_s[None, :] - q_pos_s[:, None], -REL_CLIP, REL_CLIP) + REL_CLIP

    cakt = jnp.transpose(cache_a_k, (0, 1, 3, 4, 5, 2))
    cav = cache_a_v.reshape(depth, dec_batch, past * N_HEADS, HEAD_W)
    cbk = cache_b_k.reshape(depth, dec_batch, band_past * N_HEADS, HEAD_W)
    cbv = cache_b_v.reshape(depth, dec_batch, band_past * N_HEADS, HEAD_W)

    w_in_b = _to_bf16(w_in)
    w_mix_b = (_to_bf16(w_oa), _to_bf16(w_ob), _to_bf16(w_out))
    w_ffn_b = (_to_bf16(w1), _to_bf16(w3), _to_bf16(w2))

    xp = x_prompt.reshape(batch * seq, d)
    xs = x_sample.reshape(dec_batch * t_new, d)
    kv_p, kv_s = (), ()
    bk_p, bv_p, bk_s, bv_s = [], [], [], []
    for l in range(depth):
        lam_init = 0.8 - 0.6 * math.exp(-0.3 * l)
        lam_vecs = jnp.stack([lambda_q1[l], lambda_k1[l], lambda_q2[l], lambda_k2[l]]).astype(F32)
        g = subln_g[l].reshape(1, HEAD_W).astype(F32)
        mod_l = mod[l:l + 1]
        ln1 = (ln1_g[l].reshape(1, d), ln1_b[l].reshape(1, d))
        ln2 = (ln2_g[l].reshape(1, d), ln2_b[l].reshape(1, d))
        rel_tab = rel_bias[l] * LOG2E
        bb_tile = _attn_b_bias_tile(rel_tab, TQ_B, WIN_B)
        bb_s = _lookup(rel_tab, rel_b)

        grp = dict(rows_per_group=seq, group0=dec_batch, bm=bm_p)
        p = _inproj_call(xp, mod_l, w_in_b, kv_p, layer=l, depth=depth, tn=tn, key_major=True,
                         **dict(grp, bm=bm_in))
        kv_p = (p["ka_all"], p["va_all"])
        oa = _attn_a_call(p["qa"], p["ka"], p["va"], ta_tiles, lam_vecs, g,
                          batch=batch, seq=seq, lam_init=lam_init)
        ob = _attn_b_call(p["qb"], p["kb"], p["vb"], bb_tile, batch=batch, seq=seq)
        xp = _mixout_call(oa, ob, p["sga"], p["sgb"], xp, mod_l, *w_mix_b, *ln1,
                          layer=l, alpha=alpha, **grp)
        xp = _ffn_call(xp, mod_l, *w_ffn_b, *ln2, layer=l, tf=tf, alpha=alpha, **dict(grp, bm=bm_in))
        bk_p.append(p["kb"].reshape(batch, seq, N_HEADS, HEAD_W)[:, seq - prompt_band:])
        bv_p.append(p["vb"].reshape(batch, seq, N_HEADS, HEAD_W)[:, seq - prompt_band:])

        grp = dict(rows_per_group=t_new, group0=0, bm=bm_s)
        p = _inproj_call(xs, mod_l, w_in_b, kv_s, layer=l, depth=depth, tn=tn, key_major=False,
                         **dict(grp, bm=bm_sin))
        kv_s = (p["ka_all"], p["va_all"])
        oa, ob = _sample_call(l, p["qa"], p["ka_all"], p["va_all"], cakt, cav, p["qb"], p["kb"],
                              p["vb"], cbk, cbv,
                              ba_s[:, :, :past], ba_s[:, :, past:], bb_s[:, :, :band_past],
                              bb_s[:, :, band_past:], lam_vecs, g,
                              batch=dec_batch, t_new=t_new, lam_init=lam_init)
        xs = _mixout_call(oa, ob, p["sga"], p["sgb"], xs, mod_l, *w_mix_b, *ln1,
                          layer=l, alpha=alpha, **grp)
        xs = _ffn_call(xs, mod_l, *w_ffn_b, *ln2, layer=l, tf=tf, alpha=alpha, **grp)
        bk_s.append(p["kb"].reshape(dec_batch, t_new, N_HEADS, HEAD_W))
        bv_s.append(p["vb"].reshape(dec_batch, t_new, N_HEADS, HEAD_W))

    a_k_prompt = jnp.transpose(kv_p[0].reshape(depth, batch, N_HEADS, 2, DK_A, seq), (0, 1, 5, 2, 3, 4))
    return (xp.reshape(batch, seq, d), xs.reshape(dec_batch, t_new, d),
            a_k_prompt,
            kv_p[1].reshape(depth, batch, seq, N_HEADS, HEAD_W),
            jnp.stack(bk_p), jnp.stack(bv_p),
            kv_s[0].reshape(depth, dec_batch, t_new, N_HEADS, 2, DK_A),
            kv_s[1].reshape(depth, dec_batch, t_new, N_HEADS, HEAD_W),
            jnp.stack(bk_s), jnp.stack(bv_s))
```
